```python
import math
import jax, jax.numpy as jnp
from jax import lax
import numpy as np

D_MODEL = 1024
BATCH = 4
SEQ = 4096
DEPTH = 4

N_MEM = 256
MEM_HEADS = 4
MEM_HEAD_DIM = 64
MEM_WIDTH = MEM_HEADS * MEM_HEAD_DIM
SSM_WIDTH = D_MODEL // 2
MLA_WIDTH = D_MODEL // 2
MIX_WIDTH = SSM_WIDTH + MLA_WIDTH
SSM_GROUP = 16
SSM_GROUPS = SSM_WIDTH // SSM_GROUP
SSM_STATE = 64
MLA_HEADS = 8
QK_NOPE = 64
QK_ROPE = 32
QK_DIM = QK_NOPE + QK_ROPE
V_DIM = MLA_WIDTH // MLA_HEADS
Q_LORA = 256
KV_LORA = 128
ROPE_THETA = 10000.0
Q_BLOCK = 128
D_FF = 4 * D_MODEL
IN_COLS = SSM_WIDTH + Q_LORA + KV_LORA + QK_ROPE
EPS = 1e-6

kernel_name = 'hymba_s5_mla_memory_trunk'


def rms_norm(x, gain):
    xf = x.astype(jnp.float32)
    y = xf * lax.rsqrt(jnp.mean(xf * xf, axis=-1, keepdims=True) + EPS)
    return y.astype(x.dtype) * gain


def rope(x, positions):
    half = QK_ROPE // 2
    inv_freq = ROPE_THETA ** (-jnp.arange(half, dtype=jnp.float32) / half)
    ang = positions.astype(jnp.float32)[..., None] * inv_freq
    ang = ang.reshape(ang.shape[:2] + (1,) * (x.ndim - 3) + (half,))
    cos = jnp.cos(ang).astype(x.dtype)
    sin = jnp.sin(ang).astype(x.dtype)
    x1, x2 = x[..., :half], x[..., half:]
    return jnp.concatenate([x1 * cos - x2 * sin, x2 * cos + x1 * sin], axis=-1)


def s5_mixer(u, lam_re, lam_im, log_step, b_re, b_im, c_re, c_im, d, w_glu, b_glu):
    bsz, seq, _ = u.shape
    f32 = jnp.float32
    uf = u.astype(f32).reshape(bsz, seq, SSM_GROUPS, SSM_GROUP)
    lam = lax.complex(lam_re.astype(f32), lam_im.astype(f32))
    step = jnp.exp(log_step.astype(f32))[:, None]
    a_bar = jnp.exp(lam * step)
    b = lax.complex(b_re.astype(f32), b_im.astype(f32))
    b_bar = ((a_bar - 1.0) / lam)[..., None] * b
    c = lax.complex(c_re.astype(f32), c_im.astype(f32))
    bu = jnp.einsum('gph,bsgh->bsgp', b_bar, uf.astype(jnp.complex64))
    a_seq = jnp.broadcast_to(a_bar, bu.shape)

    def combine(e1, e2):
        a1, s1 = e1
        a2, s2 = e2
        return a2 * a1, a2 * s1 + s2

    _, states = lax.associative_scan(combine, (a_seq, bu), axis=1)
    y = jnp.einsum('ghp,bsgp->bsgh', c, states).real + d.astype(f32).reshape(SSM_GROUPS, SSM_GROUP) * uf
    y = jax.nn.gelu(y.reshape(bsz, seq, SSM_WIDTH)).astype(u.dtype)
    return y * jax.nn.sigmoid(y @ w_glu + b_glu)


def causal_block_attention(q, k, v):
    bsz, seq, heads, dq = q.shape
    n_blocks = seq // Q_BLOCK
    scale = 1.0 / math.sqrt(dq)
    qb = q.reshape(bsz, n_blocks, Q_BLOCK, heads, dq).transpose(1, 0, 3, 2, 4)
    kt = k.transpose(0, 2, 1, 3)
    vt = v.transpose(0, 2, 1, 3)
    k_pos = jnp.arange(seq)

    def block(args):
        q_blk, blk = args
        s = jnp.einsum('bhqd,bhkd->bhqk', q_blk, kt).astype(jnp.float32) * scale
        q_pos = blk * Q_BLOCK + jnp.arange(Q_BLOCK)
        s = jnp.where(k_pos[None, :] <= q_pos[:, None], s, -jnp.inf)
        p = jax.nn.softmax(s, axis=-1).astype(vt.dtype)
        return jnp.einsum('bhqk,bhkd->bhqd', p, vt)

    o = lax.map(block, (qb, jnp.arange(n_blocks)))
    return o.transpose(1, 0, 3, 2, 4).reshape(bsz, seq, heads, v.shape[-1])


def mla_mixer(c_q, c_kv, k_rope, positions, q_norm, w_uq, kv_norm, w_ukv, q_gain, k_gain):
    bsz, seq, _ = c_q.shape
    q = (rms_norm(c_q, q_norm) @ w_uq).reshape(bsz, seq, MLA_HEADS, QK_DIM)
    kv = (rms_norm(c_kv, kv_norm) @ w_ukv).reshape(bsz, seq, MLA_HEADS, QK_NOPE + V_DIM)
    k_nope, v = kv[..., :QK_NOPE], kv[..., QK_NOPE:]
    k_pe = jnp.broadcast_to(k_rope[:, :, None, :], (bsz, seq, MLA_HEADS, QK_ROPE))
    k = jnp.concatenate([k_nope, k_pe], axis=-1)
    q = rms_norm(q, q_gain)
    k = rms_norm(k, k_gain)
    q = jnp.concatenate([q[..., :QK_NOPE], rope(q[..., QK_NOPE:], positions)], axis=-1)
    k = jnp.concatenate([k[..., :QK_NOPE], rope(k[..., QK_NOPE:], positions)], axis=-1)
    out = causal_block_attention(q, k, v)
    return out.reshape(bsz, seq, MLA_WIDTH)


def memory_cross_attention(h, mem_h, w_q, w_kv, q_gain, k_gain, w_o):
    bsz, seq, _ = h.shape
    n_mem = mem_h.shape[1]
    q = (h @ w_q).reshape(bsz, seq, MEM_HEADS, MEM_HEAD_DIM)
    kv = (mem_h @ w_kv).reshape(bsz, n_mem, MEM_HEADS, 2 * MEM_HEAD_DIM)
    k, v = kv[..., :MEM_HEAD_DIM], kv[..., MEM_HEAD_DIM:]
    q = rms_norm(q, q_gain)
    k = rms_norm(k, k_gain)
    s = jnp.einsum('bqhd,bkhd->bhqk', q, k).astype(jnp.float32) / math.sqrt(MEM_HEAD_DIM)
    p = jax.nn.softmax(s, axis=-1).astype(v.dtype)
    o = jnp.einsum('bhqk,bkhd->bqhd', p, v).reshape(bsz, seq, MEM_WIDTH)
    return o @ w_o


def setup_inputs(seed: int = 0) -> dict:
    key = jax.random.key(seed)
    ks = jax.random.split(key, 40)
    f32 = jnp.float32

    def nrm(k, shape, scale):
        return jax.random.normal(k, shape, f32) * scale

    def gain(k, shape):
        return 1.0 + 0.01 * jax.random.normal(k, shape, f32)

    L = DEPTH
    G, P, H = SSM_GROUPS, SSM_STATE, SSM_GROUP
    lam_im = jnp.broadcast_to(jnp.pi * jnp.arange(P, dtype=f32), (L, G, P)) + 0.01 * jax.random.normal(ks[4], (L, G, P), f32)
    lam_re = -0.5 + 0.01 * jax.random.normal(ks[3], (L, G, P), f32)
    log_step = jax.random.uniform(ks[5], (L, G), f32, math.log(1e-3), math.log(1e-1))
    return {
        'x': jax.random.normal(ks[0], (BATCH, SEQ, D_MODEL), f32),
        'mem': jax.random.normal(ks[1], (BATCH, N_MEM, D_MODEL), f32),
        'positions': jnp.broadcast_to(jnp.arange(SEQ, dtype=jnp.int32), (BATCH, SEQ)),
        'norm_mix': gain(ks[2], (L, D_MODEL)),
        'w_in': nrm(ks[6], (L, D_MODEL, IN_COLS), D_MODEL ** -0.5),
        'ssm_lambda_re': lam_re,
        'ssm_lambda_im': lam_im,
        'ssm_log_step': log_step,
        'ssm_b_re': nrm(ks[7], (L, G, P, H), (2 * H) ** -0.5),
        'ssm_b_im': nrm(ks[8], (L, G, P, H), (2 * H) ** -0.5),
        'ssm_c_re': nrm(ks[9], (L, G, H, P), (2 * P) ** -0.5),
        'ssm_c_im': nrm(ks[10], (L, G, H, P), (2 * P) ** -0.5),
        'ssm_d': nrm(ks[11], (L, SSM_WIDTH), 1.0),
        'ssm_w_glu': nrm(ks[12], (L, SSM_WIDTH, SSM_WIDTH), SSM_WIDTH ** -0.5),
        'ssm_b_glu': nrm(ks[13], (L, SSM_WIDTH), 0.02),
        'mla_q_norm': gain(ks[14], (L, Q_LORA)),
        'mla_w_uq': nrm(ks[15], (L, Q_LORA, MLA_HEADS * QK_DIM), Q_LORA ** -0.5),
        'mla_kv_norm': gain(ks[16], (L, KV_LORA)),
        'mla_w_ukv': nrm(ks[17], (L, KV_LORA, MLA_HEADS * (QK_NOPE + V_DIM)), KV_LORA ** -0.5),
        'mla_q_gain': gain(ks[18], (L, QK_DIM)),
        'mla_k_gain': gain(ks[19], (L, QK_DIM)),
        'out_norm_ssm': gain(ks[20], (L, SSM_WIDTH)),
        'out_norm_mla': gain(ks[21], (L, MLA_WIDTH)),
        'w_out': nrm(ks[22], (L, MIX_WIDTH, D_MODEL), MIX_WIDTH ** -0.5),
        'norm_mem_q': gain(ks[23], (L, D_MODEL)),
        'norm_mem_kv': gain(ks[24], (L, D_MODEL)),
        'mem_w_q': nrm(ks[25], (L, D_MODEL, MEM_WIDTH), D_MODEL ** -0.5),
        'mem_w_kv': nrm(ks[26], (L, D_MODEL, 2 * MEM_WIDTH), D_MODEL ** -0.5),
        'mem_q_gain': gain(ks[27], (L, MEM_HEAD_DIM)),
        'mem_k_gain': gain(ks[28], (L, MEM_HEAD_DIM)),
        'mem_w_o': nrm(ks[29], (L, MEM_WIDTH, D_MODEL), MEM_WIDTH ** -0.5),
        'norm_mlp': gain(ks[30], (L, D_MODEL)),
        'mlp_w1': nrm(ks[31], (L, D_MODEL, D_FF), D_MODEL ** -0.5),
        'mlp_w2': nrm(ks[32], (L, D_FF, D_MODEL), D_FF ** -0.5),
    }


def reference(x, mem, positions, norm_mix, w_in,
              ssm_lambda_re, ssm_lambda_im, ssm_log_step, ssm_b_re, ssm_b_im,
              ssm_c_re, ssm_c_im, ssm_d, ssm_w_glu, ssm_b_glu,
              mla_q_norm, mla_w_uq, mla_kv_norm, mla_w_ukv, mla_q_gain, mla_k_gain,
              out_norm_ssm, out_norm_mla, w_out,
              norm_mem_q, norm_mem_kv, mem_w_q, mem_w_kv, mem_q_gain, mem_k_gain, mem_w_o,
              norm_mlp, mlp_w1, mlp_w2):
    s1 = SSM_WIDTH
    s2 = s1 + Q_LORA
    s3 = s2 + KV_LORA
    for l in range(DEPTH):
        h = rms_norm(x, norm_mix[l])
        proj = h @ w_in[l]
        u, c_q, c_kv, k_rope = proj[..., :s1], proj[..., s1:s2], proj[..., s2:s3], proj[..., s3:]
        y_ssm = s5_mixer(u, ssm_lambda_re[l], ssm_lambda_im[l], ssm_log_step[l],
                         ssm_b_re[l], ssm_b_im[l], ssm_c_re[l], ssm_c_im[l],
                         ssm_d[l], ssm_w_glu[l], ssm_b_glu[l])
        y_mla = mla_mixer(c_q, c_kv, k_rope, positions, mla_q_norm[l], mla_w_uq[l],
                          mla_kv_norm[l], mla_w_ukv[l], mla_q_gain[l], mla_k_gain[l])
        y = jnp.concatenate([rms_norm(y_ssm, out_norm_ssm[l]), rms_norm(y_mla, out_norm_mla[l])], axis=-1)
        x = x + y @ w_out[l]
        x = x + memory_cross_attention(rms_norm(x, norm_mem_q[l]), rms_norm(mem, norm_mem_kv[l]),
                                       mem_w_q[l], mem_w_kv[l], mem_q_gain[l], mem_k_gain[l], mem_w_o[l])
        h = rms_norm(x, norm_mlp[l])
        x = x + jnp.square(jax.nn.relu(h @ mlp_w1[l])) @ mlp_w2[l]
    return x
```

```python
import functools
import math

import jax
import jax.numpy as jnp
from jax import lax
from jax.experimental import pallas as pl
from jax.experimental.pallas import tpu as pltpu

D_MODEL = 1024
N_MEM = 256
MEM_HEADS = 4
MEM_HEAD_DIM = 64
MEM_WIDTH = MEM_HEADS * MEM_HEAD_DIM
SSM_WIDTH = 512
MLA_WIDTH = 512
SSM_GROUP = 16
SSM_GROUPS = 32
SSM_STATE = 64
MLA_HEADS = 8
QK_NOPE = 64
QK_ROPE = 32
QK_DIM = QK_NOPE + QK_ROPE
V_DIM = 64
Q_LORA = 256
KV_LORA = 128
ROPE_THETA = 10000.0
D_FF = 4 * D_MODEL
EPS = 1e-6

LANES = 128
HEAD_PAD = 128
SSM_CHUNK = 32
CHUNK_W = SSM_CHUNK * SSM_GROUP
VMEM_LIMIT = 56 * 1024 * 1024

F32 = jnp.float32
BF16 = jnp.bfloat16
HIGHEST = lax.Precision.HIGHEST


def _dot(a, b):
    return jnp.dot(a, b, preferred_element_type=F32)


def _dot_nt(a, b):
    return lax.dot_general(a, b, (((1,), (1,)), ((), ())), preferred_element_type=F32)


def _rms(v, gain):
    return v * lax.rsqrt(jnp.mean(v * v, axis=-1, keepdims=True) + EPS) * gain


def _params(*sem):
    return pltpu.CompilerParams(dimension_semantics=sem, vmem_limit_bytes=VMEM_LIMIT)


def _rope_table_kernel(pos_ref, freq_ref, a_ref, b_ref):
    ang = pos_ref[...].astype(F32) * freq_ref[...]
    lane = lax.broadcasted_iota(jnp.int32, ang.shape, 1)
    in_rope = (lane >= QK_NOPE) & (lane < QK_DIM)
    a_ref[...] = jnp.where(lane < QK_NOPE, 1.0, jnp.where(in_rope, jnp.cos(ang), 0.0))
    b_ref[...] = jnp.where(in_rope, jnp.sin(ang), 0.0)


def _rope_tables(positions):
    bsz, seq = positions.shape
    t = bsz * seq
    tm = min(1024, t)
    half = QK_ROPE // 2
    inv_freq = ROPE_THETA ** (-jnp.arange(half, dtype=F32) / half)
    freq = jnp.zeros((1, LANES), F32).at[0, QK_NOPE:QK_DIM].set(jnp.tile(inv_freq, 2))
    a, b = pl.pallas_call(
        _rope_table_kernel,
        grid=(t // tm,),
        in_specs=[pl.BlockSpec((tm, 1), lambda i: (i, 0)),
                  pl.BlockSpec((1, LANES), lambda i: (0, 0))],
        out_specs=[pl.BlockSpec((tm, LANES), lambda i: (i, 0))] * 2,
        out_shape=[jax.ShapeDtypeStruct((t, LANES), F32)] * 2,
        compiler_params=_params("parallel"),
        name="rope_tables",
    )(positions.reshape(t, 1), freq)
    return a.reshape(bsz, seq, LANES), b.reshape(bsz, seq, LANES)


def _inproj_kernel(x_ref, gmix_ref, win_ref, gq_ref, wuq_ref, gkv_ref, wukv_ref,
                   qg_ref, kgn_ref, kgr_ref, ra_ref, rb_ref,
                   u_ref, q_ref, k_ref, v_ref):
    h = _rms(x_ref[0], gmix_ref[...]).astype(BF16)
    proj = _dot(h, win_ref[...])
    u_ref[0] = proj[:, :SSM_WIDTH].astype(BF16)

    ra = ra_ref[0]
    rb = rb_ref[0]
    lane = lax.broadcasted_iota(jnp.int32, ra.shape, 1)
    scale = 1.0 / math.sqrt(QK_DIM)

    c_q = proj[:, SSM_WIDTH:SSM_WIDTH + Q_LORA]
    q = _dot(_rms(c_q, gq_ref[...]).astype(BF16), wuq_ref[...])
    c_kv = proj[:, SSM_WIDTH + Q_LORA:SSM_WIDTH + Q_LORA + KV_LORA]
    kv = _dot(_rms(c_kv, gkv_ref[...]).astype(BF16), wukv_ref[...])

    kr = proj[:, 7 * LANES:8 * LANES]
    kr_ss = jnp.sum(jnp.where(lane < QK_DIM, kr * kr, 0.0), axis=-1, keepdims=True)
    t = kr * kgr_ref[...]
    krot = t * ra + pltpu.roll(t, LANES - QK_ROPE, axis=1) * rb

    for hd in range(MLA_HEADS):
        qh = q[:, hd * HEAD_PAD:(hd + 1) * HEAD_PAD]
        ss = jnp.sum(jnp.where(lane < QK_DIM, qh * qh, 0.0), axis=-1, keepdims=True)
        z = qh * qg_ref[...] * (lax.rsqrt(ss * (1.0 / QK_DIM) + EPS) * scale)
        q_ref[0, hd] = (z * ra + pltpu.roll(z, LANES - QK_ROPE, axis=1) * rb).astype(BF16)

        kvh = kv[:, hd * HEAD_PAD:(hd + 1) * HEAD_PAD]
        ks = jnp.sum(jnp.where(lane < QK_NOPE, kvh * kvh, 0.0), axis=-1, keepdims=True)
        r = lax.rsqrt((ks + kr_ss) * (1.0 / QK_DIM) + EPS)
        k_ref[0, hd] = ((kvh * kgn_ref[...] + krot) * r).astype(BF16)
        v_ref[0, hd] = kvh[:, QK_NOPE:].astype(BF16)


def _inproj(x, lw, ra, rb, tm):
    bsz, seq, _ = x.shape
    full = lambda shape: pl.BlockSpec(shape, lambda b, i: (0,) * len(shape))
    tok = lambda w: pl.BlockSpec((1, tm, w), lambda b, i: (b, i, 0))
    head = lambda w: pl.BlockSpec((1, MLA_HEADS, tm, w), lambda b, i: (b, 0, i, 0))
    return pl.pallas_call(
        _inproj_kernel,
        grid=(bsz, seq // tm),
        in_specs=[tok(D_MODEL), full((1, D_MODEL)), full((D_MODEL, D_MODEL)),
                  full((1, Q_LORA)), full((Q_LORA, MLA_HEADS * HEAD_PAD)),
                  full((1, KV_LORA)), full((KV_LORA, MLA_HEADS * HEAD_PAD)),
                  full((1, HEAD_PAD)), full((1, HEAD_PAD)), full((1, HEAD_PAD)),
                  tok(LANES), tok(LANES)],
        out_specs=[tok(SSM_WIDTH), head(HEAD_PAD), head(HEAD_PAD), head(V_DIM)],
        out_shape=[jax.ShapeDtypeStruct((bsz, seq, SSM_WIDTH), BF16),
                   jax.ShapeDtypeStruct((bsz, MLA_HEADS, seq, HEAD_PAD), BF16),
                   jax.ShapeDtypeStruct((bsz, MLA_HEADS, seq, HEAD_PAD), BF16),
                   jax.ShapeDtypeStruct((bsz, MLA_HEADS, seq, V_DIM), BF16)],
        compiler_params=_params("parallel", "parallel"),
        name="inproj",
    )(x, lw["g_mix"], lw["w_in"], lw["g_q"], lw["w_uq"], lw["g_kv"], lw["w_ukv"],
      lw["q_gain"], lw["k_gain_nope"], lw["k_gain_rope"], ra, rb)


def _s5_prep_kernel(pcol_ref, prow_ref, b_re_ref, b_im_ref, bt_re_ref, bt_im_ref,
                    ct_re_ref, ct_im_ref, sel_ref,
                    m_ref, wst_ref, wo_ref, apa_ref, apb_ref):
    def zoh(lr, li, ls):
        step = jnp.exp(ls)
        mag = jnp.exp(lr * step)
        ang = li * step
        xr = mag * jnp.cos(ang) - 1.0
        xi = mag * jnp.sin(ang)
        den = lr * lr + li * li
        return (xr * lr + xi * li) / den, (xi * lr - xr * li) / den, step

    pc = pcol_ref[0, 0]
    lr, li = pc[:, 0:1], pc[:, 1:2]
    cfr, cfi, step = zoh(lr, li, pc[:, 2:3])
    b_re, b_im = b_re_ref[0, 0], b_im_ref[0, 0]
    bb_re = cfr * b_re - cfi * b_im
    bb_im = cfr * b_im + cfi * b_re

    pr = prow_ref[0, 0]
    lr_row, li_row, ls_row = pr[0:1], pr[1:2], pr[2:3]
    cfr_row, cfi_row, step_row = zoh(lr_row, li_row, ls_row)
    bt_re, bt_im = bt_re_ref[0, 0], bt_im_ref[0, 0]
    bbt_re = cfr_row[:, :SSM_STATE] * bt_re - cfi_row[:, :SSM_STATE] * bt_im
    bbt_im = cfr_row[:, :SSM_STATE] * bt_im + cfi_row[:, :SSM_STATE] * bt_re

    sel = sel_ref[...]
    rep = lambda m: jnp.dot(m, sel, precision=HIGHEST, preferred_element_type=F32)
    brep_re, brep_im = rep(bb_re), rep(bb_im)
    crep_re, crep_im = rep(ct_re_ref[0, 0]), rep(ct_im_ref[0, 0])

    lane = lax.broadcasted_iota(jnp.int32, (SSM_STATE, CHUNK_W), 1)
    tau = (lane // SSM_GROUP).astype(F32)

    def apow(n):
        mag = jnp.exp(lr * step * n)
        th = li * step * n
        return mag * jnp.cos(th), mag * jnp.sin(th)

    e_re, e_im = apow(tau)
    g_re = e_re * crep_re - e_im * crep_im
    g_im = e_re * crep_im + e_im * crep_re
    kt = (jnp.dot(bbt_re, g_re, precision=HIGHEST, preferred_element_type=F32)
          - jnp.dot(bbt_im, g_im, precision=HIGHEST, preferred_element_type=F32))
    klane = lax.broadcasted_iota(jnp.int32, kt.shape, 1)
    m_ref[0, 0, 0:SSM_GROUP, :] = kt.astype(BF16)
    for s in range(1, SSM_CHUNK):
        shifted = jnp.where(klane >= s * SSM_GROUP, pltpu.roll(kt, s * SSM_GROUP, axis=1), 0.0)
        m_ref[0, 0, s * SSM_GROUP:(s + 1) * SSM_GROUP, :] = shifted.astype(BF16)

    e_re, e_im = apow((SSM_CHUNK - 1.0) - tau)
    wst_ref[0, 0, 0:SSM_STATE, :] = (e_re * brep_re - e_im * brep_im).astype(BF16)
    wst_ref[0, 0, SSM_STATE:, :] = (e_re * brep_im + e_im * brep_re).astype(BF16)

    e_re, e_im = apow(tau + 1.0)
    wo_ref[0, 0, 0:SSM_STATE, :] = (e_re * crep_re - e_im * crep_im).astype(BF16)
    wo_ref[0, 0, SSM_STATE:, :] = (-(e_re * crep_im + e_im * crep_re)).astype(BF16)

    krow = lax.broadcasted_iota(jnp.int32, (8, LANES), 0)
    klan = lax.broadcasted_iota(jnp.int32, (8, LANES), 1)
    n = (SSM_CHUNK * jnp.left_shift(1, krow)).astype(F32)
    mag = jnp.exp(lr_row * step_row * n)
    th = li_row * step_row * n
    p_re, p_im = mag * jnp.cos(th), mag * jnp.sin(th)
    apa_ref[0, 0] = p_re
    apb_ref[0, 0] = jnp.where(klan < SSM_STATE, -p_im, p_im)


def _s5_prep(lam_re, lam_im, log_step, b_re, b_im, c_re, c_im):
    nl, g, p = lam_re.shape
    hh = SSM_GROUP
    ls = jnp.broadcast_to(log_step[..., None], (nl, g, p))
    pcol = jnp.zeros((nl, g, p, 8), F32)
    pcol = pcol.at[..., 0].set(lam_re).at[..., 1].set(lam_im).at[..., 2].set(ls)
    dup = lambda v: jnp.concatenate([v, v], axis=-1)
    prow = jnp.zeros((nl, g, 8, LANES), F32)
    prow = prow.at[:, :, 0].set(dup(lam_re)).at[:, :, 1].set(dup(lam_im)).at[:, :, 2].set(dup(ls))
    sel = jnp.tile(jnp.eye(hh, dtype=F32), (1, SSM_CHUNK))
    blk = lambda *s: pl.BlockSpec((1, 1) + s, lambda l, i: (l, i) + (0,) * len(s))
    return pl.pallas_call(
        _s5_prep_kernel,
        grid=(nl, g),
        in_specs=[blk(p, 8), blk(8, LANES), blk(p, hh), blk(p, hh), blk(hh, p), blk(hh, p),
                  blk(p, hh), blk(p, hh), pl.BlockSpec((hh, CHUNK_W), lambda l, i: (0, 0))],
        out_specs=[blk(CHUNK_W, CHUNK_W), blk(2 * p, CHUNK_W), blk(2 * p, CHUNK_W),
                   blk(8, LANES), blk(8, LANES)],
        out_shape=[jax.ShapeDtypeStruct((nl, g, CHUNK_W, CHUNK_W), BF16),
                   jax.ShapeDtypeStruct((nl, g, 2 * p, CHUNK_W), BF16),
                   jax.ShapeDtypeStruct((nl, g, 2 * p, CHUNK_W), BF16),
                   jax.ShapeDtypeStruct((nl, g, 8, LANES), F32),
                   jax.ShapeDtypeStruct((nl, g, 8, LANES), F32)],
        compiler_params=_params("parallel", "parallel"),
        name="s5_prep",
    )(pcol, prow, b_re, b_im, jnp.swapaxes(b_re, -1, -2), jnp.swapaxes(b_im, -1, -2),
      jnp.swapaxes(c_re, -1, -2), jnp.swapaxes(c_im, -1, -2), sel)


def _s5_mix_kernel(x_ref, m_ref, wst_ref, wo_ref, apa_ref, apb_ref, dx_ref, y_ref,
                   *, chunks_per_seq, nsteps):
    x = x_ref[0]
    y = _dot(x, m_ref[0])
    st = _dot_nt(x, wst_ref[0])
    c = lax.broadcasted_iota(jnp.int32, st.shape, 0) & (chunks_per_seq - 1)
    apa, apb = apa_ref[0], apb_ref[0]
    for k in range(nsteps):
        d = 1 << k
        sh = jnp.where(c >= d, pltpu.roll(st, d, axis=0), 0.0)
        st = st + sh * apa[k:k + 1] + pltpu.roll(sh, SSM_STATE, axis=1) * apb[k:k + 1]
    carried = jnp.where(c >= 1, pltpu.roll(st, 1, axis=0), 0.0)
    y = y + _dot(carried.astype(BF16), wo_ref[0]) + dx_ref[0] * x.astype(F32)
    y_ref[0] = jax.nn.gelu(y).astype(BF16)


def _s5_mix(u, prep, dx):
    bsz, seq, _ = u.shape
    g = SSM_GROUPS
    cps = seq // SSM_CHUNK
    assert cps & (cps - 1) == 0 and cps <= 256
    nc = bsz * cps
    x = u.reshape(bsz, cps, SSM_CHUNK, g, SSM_GROUP).transpose(3, 0, 1, 2, 4).reshape(g, nc, CHUNK_W)
    m, wst, wo, apa, apb = prep
    blk = lambda *s: pl.BlockSpec((1,) + s, lambda i: (i,) + (0,) * len(s))
    y = pl.pallas_call(
        functools.partial(_s5_mix_kernel, chunks_per_seq=cps, nsteps=cps.bit_length() - 1),
        grid=(g,),
        in_specs=[blk(nc, CHUNK_W), blk(CHUNK_W, CHUNK_W), blk(2 * SSM_STATE, CHUNK_W),
                  blk(2 * SSM_STATE, CHUNK_W), blk(8, LANES), blk(8, LANES), blk(1, CHUNK_W)],
        out_specs=blk(nc, CHUNK_W),
        out_shape=jax.ShapeDtypeStruct((g, nc, CHUNK_W), BF16),
        compiler_params=_params("parallel"),
        name="s5_mix",
    )(x, m, wst, wo, apa, apb, dx)
    return y.reshape(g, bsz, cps, SSM_CHUNK, SSM_GROUP).transpose(1, 2, 3, 0, 4).reshape(bsz, seq, SSM_WIDTH)


def _flash_kernel(qt_ref, kt_ref, q_ref, k_ref, v_ref, o_ref, m_scr, l_scr, acc_scr):
    p_idx = pl.program_id(2)
    qi = qt_ref[p_idx]
    ki = kt_ref[p_idx]

    @pl.when(ki == 0)
    def _():
        m_scr[...] = jnp.full(m_scr.shape, -jnp.inf, F32)
        l_scr[...] = jnp.zeros(l_scr.shape, F32)
        acc_scr[...] = jnp.zeros(acc_scr.shape, F32)

    def step(masked):
        for j in range(2):
            s = _dot_nt(q_ref[0, j], k_ref[0, j])
            if masked:
                row = lax.broadcasted_iota(jnp.int32, s.shape, 0)
                col = lax.broadcasted_iota(jnp.int32, s.shape, 1)
                s = jnp.where(col <= row, s, -jnp.inf)
            m_prev = m_scr[j]
            m_new = jnp.maximum(m_prev, jnp.max(s, axis=-1, keepdims=True))
            alpha = jnp.exp(m_prev - m_new)
            p = jnp.exp(s - m_new)
            l_scr[j] = alpha * l_scr[j] + jnp.sum(p, axis=-1, keepdims=True)
            acc_scr[j] = alpha * acc_scr[j] + _dot(p.astype(BF16), v_ref[0, j])
            m_scr[j] = m_new

    @pl.when(ki < qi)
    def _():
        step(False)

    @pl.when(ki == qi)
    def _():
        step(True)
        outs = [acc_scr[j] / l_scr[j] for j in range(2)]
        o_ref[0] = jnp.concatenate(outs, axis=-1).astype(BF16)


def _flash(q, k, v, tq):
    bsz, heads, seq, _ = q.shape
    nq = seq // tq
    pairs = [(a, b) for a in range(nq) for b in range(a + 1)]
    qt = jnp.asarray([a for a, _ in pairs], jnp.int32)
    kt = jnp.asarray([b for _, b in pairs], jnp.int32)
    grid_spec = pltpu.PrefetchScalarGridSpec(
        num_scalar_prefetch=2,
        grid=(bsz, heads // 2, len(pairs)),
        in_specs=[pl.BlockSpec((1, 2, tq, HEAD_PAD), lambda b, h, p, qt, kt: (b, h, qt[p], 0)),
                  pl.BlockSpec((1, 2, tq, HEAD_PAD), lambda b, h, p, qt, kt: (b, h, kt[p], 0)),
                  pl.BlockSpec((1, 2, tq, V_DIM), lambda b, h, p, qt, kt: (b, h, kt[p], 0))],
        out_specs=pl.BlockSpec((1, tq, 2 * V_DIM), lambda b, h, p, qt, kt: (b, qt[p], h)),
        scratch_shapes=[pltpu.VMEM((2, tq, 1), F32), pltpu.VMEM((2, tq, 1), F32),
                        pltpu.VMEM((2, tq, V_DIM), F32)])
    return pl.pallas_call(
        _flash_kernel,
        grid_spec=grid_spec,
        out_shape=jax.ShapeDtypeStruct((bsz, seq, heads * V_DIM), BF16),
        compiler_params=_params("parallel", "parallel", "arbitrary"),
        name="flash",
    )(qt, kt, q, k, v)


def _head_sums(v2, lane, width):
    out = jnp.zeros_like(v2)
    for hd in range(v2.shape[-1] // width):
        msk = (lane >= hd * width) & (lane < (hd + 1) * width)
        out = out + jnp.where(msk, jnp.sum(jnp.where(msk, v2, 0.0), axis=-1, keepdims=True), 0.0)
    return out


def _mem_prep_kernel(mem_ref, g_ref, w_ref, kg_ref, kt_ref, vm_ref):
    hm = _rms(mem_ref[0], g_ref[0]).astype(BF16)
    kv = _dot(hm, w_ref[0])
    k, v = kv[:, :MEM_WIDTH], kv[:, MEM_WIDTH:]
    lane = lax.broadcasted_iota(jnp.int32, k.shape, 1)
    ss = _head_sums(k * k, lane, MEM_HEAD_DIM)
    kn = k * lax.rsqrt(ss * (1.0 / MEM_HEAD_DIM) + EPS) * kg_ref[0] * (1.0 / math.sqrt(MEM_HEAD_DIM))
    knt = kn.T
    row = lax.broadcasted_iota(jnp.int32, knt.shape, 0)
    for hd in range(MEM_HEADS):
        lo, hi = hd * MEM_HEAD_DIM, (hd + 1) * MEM_HEAD_DIM
        kt_ref[0, 0, hd] = jnp.where((row >= lo) & (row < hi), knt, 0.0).astype(BF16)
        vm_ref[0, 0, hd] = jnp.where((lane >= lo) & (lane < hi), v, 0.0).astype(BF16)


def _mem_prep(mem, g, w_kv, k_gain):
    nl = g.shape[0]
    bsz = mem.shape[0]
    out = jax.ShapeDtypeStruct((nl, bsz, MEM_HEADS, N_MEM, MEM_WIDTH), BF16)
    return pl.pallas_call(
        _mem_prep_kernel,
        grid=(nl, bsz),
        in_specs=[pl.BlockSpec((1, N_MEM, D_MODEL), lambda l, b: (b, 0, 0)),
                  pl.BlockSpec((1, 1, D_MODEL), lambda l, b: (l, 0, 0)),
                  pl.BlockSpec((1, D_MODEL, 2 * MEM_WIDTH), lambda l, b: (l, 0, 0)),
                  pl.BlockSpec((1, 1, MEM_WIDTH), lambda l, b: (l, 0, 0))],
        out_specs=[pl.BlockSpec((1, 1, MEM_HEADS, MEM_WIDTH, N_MEM), lambda l, b: (l, b, 0, 0, 0)),
                   pl.BlockSpec((1, 1, MEM_HEADS, N_MEM, MEM_WIDTH), lambda l, b: (l, b, 0, 0, 0))],
        out_shape=[out, out],
        compiler_params=_params("parallel", "parallel"),
        name="mem_prep",
    )(mem, g, w_kv, k_gain)


def _outproj_kernel(x_ref, ys_ref, om_ref, wglu_ref, bglu_ref, gs_ref, gm_ref, wout_ref,
                    gmq_ref, wmq_ref, qg_ref, kt_ref, vm_ref, wmo_ref, gmlp_ref,
                    x2_ref, h3_ref):
    ys_b = ys_ref[0]
    ys = ys_b.astype(F32)
    yg = ys * jax.nn.sigmoid(_dot(ys_b, wglu_ref[...]) + bglu_ref[...])
    n1 = _rms(yg, gs_ref[...]).astype(BF16)
    n2 = _rms(om_ref[0].astype(F32), gm_ref[...]).astype(BF16)
    x1 = x_ref[0] + _dot(n1, wout_ref[:SSM_WIDTH, :]) + _dot(n2, wout_ref[SSM_WIDTH:, :])

    q = _dot(_rms(x1, gmq_ref[...]).astype(BF16), wmq_ref[...])
    lane = lax.broadcasted_iota(jnp.int32, q.shape, 1)
    ss = _head_sums(q * q, lane, MEM_HEAD_DIM)
    qn = (q * lax.rsqrt(ss * (1.0 / MEM_HEAD_DIM) + EPS) * qg_ref[...]).astype(BF16)
    o = jnp.zeros(q.shape, F32)
    for hd in range(MEM_HEADS):
        s = _dot(qn, kt_ref[0, hd])
        p = jnp.exp(s - jnp.max(s, axis=-1, keepdims=True))
        inv = 1.0 / jnp.sum(p, axis=-1, keepdims=True)
        o = o + _dot(p.astype(BF16), vm_ref[0, hd]) * inv
    x2 = x1 + _dot(o.astype(BF16), wmo_ref[...])
    x2_ref[0] = x2
    h3_ref[0] = _rms(x2, gmlp_ref[...]).astype(BF16)


def _outproj(x, ys, om, lw, kt, vm, tm):
    bsz, seq, _ = x.shape
    full = lambda *s: pl.BlockSpec(s, lambda b, i: (0,) * len(s))
    tok = lambda w: pl.BlockSpec((1, tm, w), lambda b, i: (b, i, 0))
    memb = pl.BlockSpec((1, MEM_HEADS, N_MEM, MEM_WIDTH), lambda b, i: (b, 0, 0, 0))
    return pl.pallas_call(
        _outproj_kernel,
        grid=(bsz, seq // tm),
        in_specs=[tok(D_MODEL), tok(SSM_WIDTH), tok(MLA_WIDTH),
                  full(SSM_WIDTH, SSM_WIDTH), full(1, SSM_WIDTH), full(1, SSM_WIDTH), full(1, MLA_WIDTH),
                  full(D_MODEL, D_MODEL), full(1, D_MODEL), full(D_MODEL, MEM_WIDTH), full(1, MEM_WIDTH),
                  memb, memb, full(MEM_WIDTH, D_MODEL), full(1, D_MODEL)],
        out_specs=[tok(D_MODEL), tok(D_MODEL)],
        out_shape=[jax.ShapeDtypeStruct((bsz, seq, D_MODEL), F32),
                   jax.ShapeDtypeStruct((bsz, seq, D_MODEL), BF16)],
        compiler_params=_params("parallel", "parallel"),
        name="outproj",
    )(x, ys, om, lw["w_glu"], lw["b_glu"], lw["g_ssm"], lw["g_mla"], lw["w_out"],
      lw["g_memq"], lw["w_mq"], lw["mem_q_gain"], kt, vm, lw["w_mo"], lw["g_mlp"])


def _mlp_kernel(x_ref, h_ref, w1_ref, w2_ref, o_ref, *, ff_tile):
    h = h_ref[...]
    acc = x_ref[...]
    for c in range(D_FF // ff_tile):
        a = jnp.maximum(_dot(h, w1_ref[:, c * ff_tile:(c + 1) * ff_tile]), 0.0)
        acc = acc + _dot((a * a).astype(BF16), w2_ref[c * ff_tile:(c + 1) * ff_tile, :])
    o_ref[...] = acc


def _mlp(x, h, w1, w2, tm):
    t = x.shape[0]
    tok = pl.BlockSpec((tm, D_MODEL), lambda i: (i, 0))
    return pl.pallas_call(
        functools.partial(_mlp_kernel, ff_tile=1024),
        grid=(t // tm,),
        in_specs=[tok, tok,
                  pl.BlockSpec((D_MODEL, D_FF), lambda i: (0, 0)),
                  pl.BlockSpec((D_FF, D_MODEL), lambda i: (0, 0))],
        out_specs=tok,
        out_shape=jax.ShapeDtypeStruct((t, D_MODEL), F32),
        compiler_params=_params("parallel"),
        name="mlp",
    )(x, h, w1, w2)


def _half_swap(w):
    h = w.shape[-1] // 2
    return jnp.concatenate([-w[..., h:], w[..., :h]], axis=-1)


def _half_swap_unsigned(w):
    h = w.shape[-1] // 2
    return jnp.concatenate([w[..., h:], w[..., :h]], axis=-1)


def _trunk(tm, tq, x, mem, positions, norm_mix, w_in, ssm_lambda_re, ssm_lambda_im, ssm_log_step, ssm_b_re, ssm_b_im, ssm_c_re, ssm_c_im, ssm_d, ssm_w_glu, ssm_b_glu, mla_q_norm, mla_w_uq, mla_kv_norm, mla_w_ukv, mla_q_gain, mla_k_gain, out_norm_ssm, out_norm_mla, w_out, norm_mem_q, norm_mem_kv, mem_w_q, mem_w_kv, mem_q_gain, mem_k_gain, mem_w_o, norm_mlp, mlp_w1, mlp_w2):
    bsz, seq, _ = x.shape
    depth = norm_mix.shape[0]

    s3 = SSM_WIDTH + Q_LORA + KV_LORA
    k_rope_w = w_in[..., s3:]
    w_in_x = jnp.concatenate(
        [w_in[..., :s3], jnp.zeros((depth, D_MODEL, QK_NOPE), F32), k_rope_w, _half_swap(k_rope_w)],
        axis=-1).astype(BF16)
    wq = mla_w_uq.reshape(depth, Q_LORA, MLA_HEADS, QK_DIM)
    w_uq_x = jnp.concatenate([wq, _half_swap(wq[..., QK_NOPE:])], axis=-1)
    w_uq_x = w_uq_x.reshape(depth, Q_LORA, MLA_HEADS * HEAD_PAD).astype(BF16)
    q_gain_x = jnp.concatenate([mla_q_gain, _half_swap_unsigned(mla_q_gain[:, QK_NOPE:])], axis=-1)
    zeros64 = jnp.zeros((depth, QK_NOPE), F32)
    k_gain_nope = jnp.concatenate([mla_k_gain[:, :QK_NOPE], zeros64], axis=-1)
    k_gain_rope = jnp.concatenate(
        [zeros64, mla_k_gain[:, QK_NOPE:], _half_swap_unsigned(mla_k_gain[:, QK_NOPE:])], axis=-1)
    wkv = mem_w_kv.reshape(depth, D_MODEL, MEM_HEADS, 2, MEM_HEAD_DIM)
    w_mkv = wkv.transpose(0, 1, 3, 2, 4).reshape(depth, D_MODEL, 2 * MEM_WIDTH).astype(BF16)

    row = lambda a: a[:, None, :]
    ra, rb = _rope_tables(positions)
    prep = _s5_prep(ssm_lambda_re, ssm_lambda_im, ssm_log_step, ssm_b_re, ssm_b_im, ssm_c_re, ssm_c_im)
    kt_all, vm_all = _mem_prep(mem, row(norm_mem_kv), w_mkv, row(jnp.tile(mem_k_gain, (1, MEM_HEADS))))
    dx = jnp.tile(ssm_d.reshape(depth, SSM_GROUPS, 1, SSM_GROUP), (1, 1, 1, SSM_CHUNK))

    w_glu_b = ssm_w_glu.astype(BF16)
    w_ukv_b = mla_w_ukv.astype(BF16)
    w_out_b = w_out.astype(BF16)
    w_mq_b = mem_w_q.astype(BF16)
    w_mo_b = mem_w_o.astype(BF16)
    w1_b = mlp_w1.astype(BF16)
    w2_b = mlp_w2.astype(BF16)
    mem_q_gain_x = jnp.tile(mem_q_gain, (1, MEM_HEADS))

    for l in range(depth):
        lw = dict(g_mix=norm_mix[l:l + 1], w_in=w_in_x[l], g_q=mla_q_norm[l:l + 1], w_uq=w_uq_x[l],
                  g_kv=mla_kv_norm[l:l + 1], w_ukv=w_ukv_b[l], q_gain=q_gain_x[l:l + 1],
                  k_gain_nope=k_gain_nope[l:l + 1], k_gain_rope=k_gain_rope[l:l + 1],
                  w_glu=w_glu_b[l], b_glu=ssm_b_glu[l:l + 1], g_ssm=out_norm_ssm[l:l + 1],
                  g_mla=out_norm_mla[l:l + 1], w_out=w_out_b[l], g_memq=norm_mem_q[l:l + 1],
                  w_mq=w_mq_b[l], mem_q_gain=mem_q_gain_x[l:l + 1], w_mo=w_mo_b[l],
                  g_mlp=norm_mlp[l:l + 1])
        u, q, k, v = _inproj(x, lw, ra, rb, tm)
        ys = _s5_mix(u, tuple(p[l] for p in prep), dx[l])
        om = _flash(q, k, v, tq)
        x2, h3 = _outproj(x, ys, om, lw, kt_all[l], vm_all[l], tm)
        x = _mlp(x2.reshape(bsz * seq, D_MODEL), h3.reshape(bsz * seq, D_MODEL),
                 w1_b[l], w2_b[l], tm).reshape(bsz, seq, D_MODEL)
    return x


def kernel(x, mem, positions, norm_mix, w_in, ssm_lambda_re, ssm_lambda_im, ssm_log_step, ssm_b_re, ssm_b_im, ssm_c_re, ssm_c_im, ssm_d, ssm_w_glu, ssm_b_glu, mla_q_norm, mla_w_uq, mla_kv_norm, mla_w_ukv, mla_q_gain, mla_k_gain, out_norm_ssm, out_norm_mla, w_out, norm_mem_q, norm_mem_kv, mem_w_q, mem_w_kv, mem_q_gain, mem_k_gain, mem_w_o, norm_mlp, mlp_w1, mlp_w2):
    seq = x.shape[1]
    return _trunk(min(512, seq), min(512, seq), x, mem, positions, norm_mix, w_in, ssm_lambda_re, ssm_lambda_im, ssm_log_step, ssm_b_re, ssm_b_im, ssm_c_re, ssm_c_im, ssm_d, ssm_w_glu, ssm_b_glu, mla_q_norm, mla_w_uq, mla_kv_norm, mla_w_ukv, mla_q_gain, mla_k_gain, out_norm_ssm, out_norm_mla, w_out, norm_mem_q, norm_mem_kv, mem_w_q, mem_w_kv, mem_q_gain, mem_k_gain, mem_w_o, norm_mlp, mlp_w1, mlp_w2)
```

```python
import functools
import math

import jax
import jax.numpy as jnp
from jax import lax
from jax.experimental import pallas as pl
from jax.experimental.pallas import tpu as pltpu

D_MODEL = 1024
N_MEM = 256
MEM_HEADS = 4
MEM_HEAD_DIM = 64
MEM_WIDTH = MEM_HEADS * MEM_HEAD_DIM
SSM_WIDTH = 512
MLA_WIDTH = 512
SSM_GROUP = 16
SSM_GROUPS = 32
SSM_STATE = 64
MLA_HEADS = 8
QK_NOPE = 64
QK_ROPE = 32
QK_DIM = QK_NOPE + QK_ROPE
V_DIM = 64
Q_LORA = 256
KV_LORA = 128
ROPE_THETA = 10000.0
D_FF = 4 * D_MODEL
EPS = 1e-6

LANES = 128
HEAD_PAD = 128
SSM_CHUNK = 32
CHUNK_W = SSM_CHUNK * SSM_GROUP
FLASH_HEADS = 4
VMEM_LIMIT = 56 * 1024 * 1024

F32 = jnp.float32
BF16 = jnp.bfloat16
HIGHEST = lax.Precision.HIGHEST


def _dot(a, b):
    return jnp.dot(a, b, preferred_element_type=F32)


def _dot_nt(a, b):
    return lax.dot_general(a, b, (((1,), (1,)), ((), ())), preferred_element_type=F32)


def _rms(v, gain):
    return v * lax.rsqrt(jnp.mean(v * v, axis=-1, keepdims=True) + EPS) * gain


def _params(*sem):
    return pltpu.CompilerParams(dimension_semantics=sem, vmem_limit_bytes=VMEM_LIMIT)


def _rope_table_kernel(pos_ref, freq_ref, a_ref, b_ref):
    ang = pos_ref[...].astype(F32) * freq_ref[...]
    lane = lax.broadcasted_iota(jnp.int32, ang.shape, 1)
    in_rope = (lane >= QK_NOPE) & (lane < QK_DIM)
    a_ref[...] = jnp.where(lane < QK_NOPE, 1.0, jnp.where(in_rope, jnp.cos(ang), 0.0))
    b_ref[...] = jnp.where(in_rope, jnp.sin(ang), 0.0)


def _rope_tables(positions):
    bsz, seq = positions.shape
    t = bsz * seq
    tm = min(1024, t)
    half = QK_ROPE // 2
    inv_freq = ROPE_THETA ** (-jnp.arange(half, dtype=F32) / half)
    freq = jnp.zeros((1, LANES), F32).at[0, QK_NOPE:QK_DIM].set(jnp.tile(inv_freq, 2))
    a, b = pl.pallas_call(
        _rope_table_kernel,
        grid=(t // tm,),
        in_specs=[pl.BlockSpec((tm, 1), lambda i: (i, 0)),
                  pl.BlockSpec((1, LANES), lambda i: (0, 0))],
        out_specs=[pl.BlockSpec((tm, LANES), lambda i: (i, 0))] * 2,
        out_shape=[jax.ShapeDtypeStruct((t, LANES), F32)] * 2,
        compiler_params=_params("parallel"),
        name="rope_tables",
    )(positions.reshape(t, 1), freq)
    return a.reshape(bsz, seq, LANES), b.reshape(bsz, seq, LANES)


def _inproj_kernel(x_ref, gmix_ref, win_ref, gq_ref, wuq_ref, gkv_ref, wukv_ref,
                   qg_ref, kgn_ref, kgr_ref, ra_ref, rb_ref,
                   u_ref, q_ref, k_ref, v_ref):
    h = _rms(x_ref[0], gmix_ref[...]).astype(BF16)
    proj = _dot(h, win_ref[...])
    u_ref[0] = proj[:, :SSM_WIDTH].astype(BF16)

    ra = ra_ref[0]
    rb = rb_ref[0]
    lane = lax.broadcasted_iota(jnp.int32, ra.shape, 1)
    scale = math.log2(math.e) / math.sqrt(QK_DIM)
    ones_col = jnp.where(lane == V_DIM, 1.0, 0.0)

    c_q = proj[:, SSM_WIDTH:SSM_WIDTH + Q_LORA]
    q = _dot(_rms(c_q, gq_ref[...]).astype(BF16), wuq_ref[...])
    c_kv = proj[:, SSM_WIDTH + Q_LORA:SSM_WIDTH + Q_LORA + KV_LORA]
    kv = _dot(_rms(c_kv, gkv_ref[...]).astype(BF16), wukv_ref[...])

    kr = proj[:, 7 * LANES:8 * LANES]
    kr_ss = jnp.sum(jnp.where(lane < QK_DIM, kr * kr, 0.0), axis=-1, keepdims=True)
    t = kr * kgr_ref[...]
    krot = t * ra + pltpu.roll(t, LANES - QK_ROPE, axis=1) * rb

    for hd in range(MLA_HEADS):
        qh = q[:, hd * HEAD_PAD:(hd + 1) * HEAD_PAD]
        ss = jnp.sum(jnp.where(lane < QK_DIM, qh * qh, 0.0), axis=-1, keepdims=True)
        z = qh * qg_ref[...] * (lax.rsqrt(ss * (1.0 / QK_DIM) + EPS) * scale)
        q_ref[0, hd] = (z * ra + pltpu.roll(z, LANES - QK_ROPE, axis=1) * rb).astype(BF16)

        kvh = kv[:, hd * HEAD_PAD:(hd + 1) * HEAD_PAD]
        ks = jnp.sum(jnp.where(lane < QK_NOPE, kvh * kvh, 0.0), axis=-1, keepdims=True)
        r = lax.rsqrt((ks + kr_ss) * (1.0 / QK_DIM) + EPS)
        k_ref[0, hd] = ((kvh * kgn_ref[...] + krot) * r).astype(BF16)
        v_ref[0, hd] = jnp.where(lane < V_DIM, pltpu.roll(kvh, QK_NOPE, axis=1), ones_col).astype(BF16)


def _inproj(x, lw, ra, rb, tm):
    bsz, seq, _ = x.shape
    full = lambda shape: pl.BlockSpec(shape, lambda b, i: (0,) * len(shape))
    tok = lambda w: pl.BlockSpec((1, tm, w), lambda b, i: (b, i, 0))
    head = lambda w: pl.BlockSpec((1, MLA_HEADS, tm, w), lambda b, i: (b, 0, i, 0))
    return pl.pallas_call(
        _inproj_kernel,
        grid=(bsz, seq // tm),
        in_specs=[tok(D_MODEL), full((1, D_MODEL)), full((D_MODEL, D_MODEL)),
                  full((1, Q_LORA)), full((Q_LORA, MLA_HEADS * HEAD_PAD)),
                  full((1, KV_LORA)), full((KV_LORA, MLA_HEADS * HEAD_PAD)),
                  full((1, HEAD_PAD)), full((1, HEAD_PAD)), full((1, HEAD_PAD)),
                  tok(LANES), tok(LANES)],
        out_specs=[tok(SSM_WIDTH), head(HEAD_PAD), head(HEAD_PAD), head(HEAD_PAD)],
        out_shape=[jax.ShapeDtypeStruct((bsz, seq, SSM_WIDTH), BF16),
                   jax.ShapeDtypeStruct((bsz, MLA_HEADS, seq, HEAD_PAD), BF16),
                   jax.ShapeDtypeStruct((bsz, MLA_HEADS, seq, HEAD_PAD), BF16),
                   jax.ShapeDtypeStruct((bsz, MLA_HEADS, seq, HEAD_PAD), BF16)],
        compiler_params=_params("parallel", "parallel"),
        name="inproj",
    )(x, lw["g_mix"], lw["w_in"], lw["g_q"], lw["w_uq"], lw["g_kv"], lw["w_ukv"],
      lw["q_gain"], lw["k_gain_nope"], lw["k_gain_rope"], ra, rb)


def _s5_prep_kernel(pcol_ref, prow_ref, b_re_ref, b_im_ref, bt_re_ref, bt_im_ref,
                    ct_re_ref, ct_im_ref, sel_ref,
                    m_ref, wst_ref, wo_ref, apa_ref, apb_ref):
    def zoh(lr, li, ls):
        step = jnp.exp(ls)
        mag = jnp.exp(lr * step)
        ang = li * step
        xr = mag * jnp.cos(ang) - 1.0
        xi = mag * jnp.sin(ang)
        den = lr * lr + li * li
        return (xr * lr + xi * li) / den, (xi * lr - xr * li) / den, step

    pc = pcol_ref[0, 0]
    lr, li = pc[:, 0:1], pc[:, 1:2]
    cfr, cfi, step = zoh(lr, li, pc[:, 2:3])
    b_re, b_im = b_re_ref[0, 0], b_im_ref[0, 0]
    bb_re = cfr * b_re - cfi * b_im
    bb_im = cfr * b_im + cfi * b_re

    pr = prow_ref[0, 0]
    lr_row, li_row, ls_row = pr[0:1], pr[1:2], pr[2:3]
    cfr_row, cfi_row, step_row = zoh(lr_row, li_row, ls_row)
    bt_re, bt_im = bt_re_ref[0, 0], bt_im_ref[0, 0]
    bbt_re = cfr_row[:, :SSM_STATE] * bt_re - cfi_row[:, :SSM_STATE] * bt_im
    bbt_im = cfr_row[:, :SSM_STATE] * bt_im + cfi_row[:, :SSM_STATE] * bt_re

    sel = sel_ref[...]
    rep = lambda m: jnp.dot(m, sel, precision=HIGHEST, preferred_element_type=F32)
    brep_re, brep_im = rep(bb_re), rep(bb_im)
    crep_re, crep_im = rep(ct_re_ref[0, 0]), rep(ct_im_ref[0, 0])

    lane = lax.broadcasted_iota(jnp.int32, (SSM_STATE, CHUNK_W), 1)
    tau = (lane // SSM_GROUP).astype(F32)

    def apow(n):
        mag = jnp.exp(lr * step * n)
        th = li * step * n
        return mag * jnp.cos(th), mag * jnp.sin(th)

    e_re, e_im = apow(tau)
    g_re = e_re * crep_re - e_im * crep_im
    g_im = e_re * crep_im + e_im * crep_re
    kt = (jnp.dot(bbt_re, g_re, precision=HIGHEST, preferred_element_type=F32)
          - jnp.dot(bbt_im, g_im, precision=HIGHEST, preferred_element_type=F32))
    klane = lax.broadcasted_iota(jnp.int32, kt.shape, 1)
    m_ref[0, 0, 0:SSM_GROUP, :] = kt.astype(BF16)
    for s in range(1, SSM_CHUNK):
        shifted = jnp.where(klane >= s * SSM_GROUP, pltpu.roll(kt, s * SSM_GROUP, axis=1), 0.0)
        m_ref[0, 0, s * SSM_GROUP:(s + 1) * SSM_GROUP, :] = shifted.astype(BF16)

    e_re, e_im = apow((SSM_CHUNK - 1.0) - tau)
    wst_ref[0, 0, 0:SSM_STATE, :] = (e_re * brep_re - e_im * brep_im).astype(BF16)
    wst_ref[0, 0, SSM_STATE:, :] = (e_re * brep_im + e_im * brep_re).astype(BF16)

    e_re, e_im = apow(tau + 1.0)
    wo_ref[0, 0, 0:SSM_STATE, :] = (e_re * crep_re - e_im * crep_im).astype(BF16)
    wo_ref[0, 0, SSM_STATE:, :] = (-(e_re * crep_im + e_im * crep_re)).astype(BF16)

    krow = lax.broadcasted_iota(jnp.int32, (8, LANES), 0)
    klan = lax.broadcasted_iota(jnp.int32, (8, LANES), 1)
    n = (SSM_CHUNK * jnp.left_shift(1, krow)).astype(F32)
    mag = jnp.exp(lr_row * step_row * n)
    th = li_row * step_row * n
    p_re, p_im = mag * jnp.cos(th), mag * jnp.sin(th)
    apa_ref[0, 0] = p_re
    apb_ref[0, 0] = jnp.where(klan < SSM_STATE, -p_im, p_im)


def _s5_prep(lam_re, lam_im, log_step, b_re, b_im, c_re, c_im):
    nl, g, p = lam_re.shape
    hh = SSM_GROUP
    ls = jnp.broadcast_to(log_step[..., None], (nl, g, p))
    pcol = jnp.zeros((nl, g, p, 8), F32)
    pcol = pcol.at[..., 0].set(lam_re).at[..., 1].set(lam_im).at[..., 2].set(ls)
    dup = lambda v: jnp.concatenate([v, v], axis=-1)
    prow = jnp.zeros((nl, g, 8, LANES), F32)
    prow = prow.at[:, :, 0].set(dup(lam_re)).at[:, :, 1].set(dup(lam_im)).at[:, :, 2].set(dup(ls))
    sel = jnp.tile(jnp.eye(hh, dtype=F32), (1, SSM_CHUNK))
    blk = lambda *s: pl.BlockSpec((1, 1) + s, lambda l, i: (l, i) + (0,) * len(s))
    return pl.pallas_call(
        _s5_prep_kernel,
        grid=(nl, g),
        in_specs=[blk(p, 8), blk(8, LANES), blk(p, hh), blk(p, hh), blk(hh, p), blk(hh, p),
                  blk(p, hh), blk(p, hh), pl.BlockSpec((hh, CHUNK_W), lambda l, i: (0, 0))],
        out_specs=[blk(CHUNK_W, CHUNK_W), blk(2 * p, CHUNK_W), blk(2 * p, CHUNK_W),
                   blk(8, LANES), blk(8, LANES)],
        out_shape=[jax.ShapeDtypeStruct((nl, g, CHUNK_W, CHUNK_W), BF16),
                   jax.ShapeDtypeStruct((nl, g, 2 * p, CHUNK_W), BF16),
                   jax.ShapeDtypeStruct((nl, g, 2 * p, CHUNK_W), BF16),
                   jax.ShapeDtypeStruct((nl, g, 8, LANES), F32),
                   jax.ShapeDtypeStruct((nl, g, 8, LANES), F32)],
        compiler_params=_params("parallel", "parallel"),
        name="s5_prep",
    )(pcol, prow, b_re, b_im, jnp.swapaxes(b_re, -1, -2), jnp.swapaxes(b_im, -1, -2),
      jnp.swapaxes(c_re, -1, -2), jnp.swapaxes(c_im, -1, -2), sel)


def _s5_mix_kernel(x_ref, m_ref, wst_ref, wo_ref, apa_ref, apb_ref, dx_ref, y_ref,
                   *, chunks_per_seq, nsteps):
    x = x_ref[0]
    y = _dot(x, m_ref[0])
    st = _dot_nt(x, wst_ref[0])
    c = lax.broadcasted_iota(jnp.int32, st.shape, 0) & (chunks_per_seq - 1)
    apa, apb = apa_ref[0], apb_ref[0]
    for k in range(nsteps):
        d = 1 << k
        sh = jnp.where(c >= d, pltpu.roll(st, d, axis=0), 0.0)
        st = st + sh * apa[k:k + 1] + pltpu.roll(sh, SSM_STATE, axis=1) * apb[k:k + 1]
    carried = jnp.where(c >= 1, pltpu.roll(st, 1, axis=0), 0.0)
    y = y + _dot(carried.astype(BF16), wo_ref[0]) + dx_ref[0] * x.astype(F32)
    y_ref[0] = jax.nn.gelu(y).astype(BF16)


def _s5_mix(u, prep, dx):
    bsz, seq, _ = u.shape
    g = SSM_GROUPS
    cps = seq // SSM_CHUNK
    assert cps & (cps - 1) == 0 and cps <= 256
    nc = bsz * cps
    x = u.reshape(bsz, cps, SSM_CHUNK, g, SSM_GROUP).transpose(3, 0, 1, 2, 4).reshape(g, nc, CHUNK_W)
    m, wst, wo, apa, apb = prep
    blk = lambda *s: pl.BlockSpec((1,) + s, lambda i: (i,) + (0,) * len(s))
    y = pl.pallas_call(
        functools.partial(_s5_mix_kernel, chunks_per_seq=cps, nsteps=cps.bit_length() - 1),
        grid=(g,),
        in_specs=[blk(nc, CHUNK_W), blk(CHUNK_W, CHUNK_W), blk(2 * SSM_STATE, CHUNK_W),
                  blk(2 * SSM_STATE, CHUNK_W), blk(8, LANES), blk(8, LANES), blk(1, CHUNK_W)],
        out_specs=blk(nc, CHUNK_W),
        out_shape=jax.ShapeDtypeStruct((g, nc, CHUNK_W), BF16),
        compiler_params=_params("parallel"),
        name="s5_mix",
    )(x, m, wst, wo, apa, apb, dx)
    return y.reshape(g, bsz, cps, SSM_CHUNK, SSM_GROUP).transpose(1, 2, 3, 0, 4).reshape(bsz, seq, SSM_WIDTH)


def _flash_kernel(qt_ref, kt_ref, q_ref, k_ref, v_ref, o_ref, m_scr, acc_scr, *, heads):
    p_idx = pl.program_id(2)
    qi = qt_ref[p_idx]
    ki = kt_ref[p_idx]
    tk = k_ref.shape[2]

    @pl.when(ki == 0)
    def _():
        m_scr[...] = jnp.full(m_scr.shape, -jnp.inf, F32)
        acc_scr[...] = jnp.zeros(acc_scr.shape, F32)

    def step(masked):
        for j in range(heads):
            s = _dot_nt(q_ref[0, j], k_ref[0, j])
            if masked:
                row = lax.broadcasted_iota(jnp.int32, s.shape, 0)
                col = lax.broadcasted_iota(jnp.int32, s.shape, 1)
                s = jnp.where(col <= row, s, -jnp.inf)
            m_prev = m_scr[j]
            m_new = jnp.maximum(m_prev, jnp.max(s, axis=-1, keepdims=True))
            alpha = jnp.exp2(m_prev - m_new)
            p = jnp.exp2(s - jnp.tile(m_new, (1, tk // LANES)))
            acc_scr[j] = alpha * acc_scr[j] + _dot(p.astype(BF16), v_ref[0, j])
            m_scr[j] = m_new

    @pl.when(ki < qi)
    def _():
        step(False)

    @pl.when(ki == qi)
    def _():
        step(True)
        outs = []
        for j in range(heads):
            acc = acc_scr[j]
            outs.append(acc[:, :V_DIM] / acc[:, V_DIM:V_DIM + 1])
        o_ref[0] = jnp.concatenate(outs, axis=-1).astype(BF16)


def _flash(q, k, v, tq, heads_per_step):
    bsz, heads, seq, _ = q.shape
    hp = heads_per_step
    nq = seq // tq
    pairs = [(a, b) for a in range(nq) for b in range(a + 1)]
    qt = jnp.asarray([a for a, _ in pairs], jnp.int32)
    kt = jnp.asarray([b for _, b in pairs], jnp.int32)
    spec = lambda tbl: pl.BlockSpec((1, hp, tq, HEAD_PAD),
                                    lambda b, h, p, qt, kt: (b, h, (qt, kt)[tbl][p], 0))
    grid_spec = pltpu.PrefetchScalarGridSpec(
        num_scalar_prefetch=2,
        grid=(bsz, heads // hp, len(pairs)),
        in_specs=[spec(0), spec(1), spec(1)],
        out_specs=pl.BlockSpec((1, tq, hp * V_DIM), lambda b, h, p, qt, kt: (b, qt[p], h)),
        scratch_shapes=[pltpu.VMEM((hp, tq, LANES), F32), pltpu.VMEM((hp, tq, HEAD_PAD), F32)])
    return pl.pallas_call(
        functools.partial(_flash_kernel, heads=hp),
        grid_spec=grid_spec,
        out_shape=jax.ShapeDtypeStruct((bsz, seq, heads * V_DIM), BF16),
        compiler_params=_params("parallel", "parallel", "arbitrary"),
        name="flash",
    )(qt, kt, q, k, v)


def _head_sums(v2, lane, width):
    out = jnp.zeros_like(v2)
    for hd in range(v2.shape[-1] // width):
        msk = (lane >= hd * width) & (lane < (hd + 1) * width)
        out = out + jnp.where(msk, jnp.sum(jnp.where(msk, v2, 0.0), axis=-1, keepdims=True), 0.0)
    return out


def _mem_prep_kernel(mem_ref, g_ref, w_ref, kg_ref, kt_ref, vm_ref):
    hm = _rms(mem_ref[0], g_ref[0]).astype(BF16)
    kv = _dot(hm, w_ref[0])
    k, v = kv[:, :MEM_WIDTH], kv[:, MEM_WIDTH:]
    lane = lax.broadcasted_iota(jnp.int32, k.shape, 1)
    ss = _head_sums(k * k, lane, MEM_HEAD_DIM)
    kn = k * lax.rsqrt(ss * (1.0 / MEM_HEAD_DIM) + EPS) * kg_ref[0] * (1.0 / math.sqrt(MEM_HEAD_DIM))
    knt = kn.T
    row = lax.broadcasted_iota(jnp.int32, knt.shape, 0)
    for hd in range(MEM_HEADS):
        lo, hi = hd * MEM_HEAD_DIM, (hd + 1) * MEM_HEAD_DIM
        kt_ref[0, 0, hd] = jnp.where((row >= lo) & (row < hi), knt, 0.0).astype(BF16)
        vm_ref[0, 0, hd] = jnp.where((lane >= lo) & (lane < hi), v, 0.0).astype(BF16)


def _mem_prep(mem, g, w_kv, k_gain):
    nl = g.shape[0]
    bsz = mem.shape[0]
    out = jax.ShapeDtypeStruct((nl, bsz, MEM_HEADS, N_MEM, MEM_WIDTH), BF16)
    return pl.pallas_call(
        _mem_prep_kernel,
        grid=(nl, bsz),
        in_specs=[pl.BlockSpec((1, N_MEM, D_MODEL), lambda l, b: (b, 0, 0)),
                  pl.BlockSpec((1, 1, D_MODEL), lambda l, b: (l, 0, 0)),
                  pl.BlockSpec((1, D_MODEL, 2 * MEM_WIDTH), lambda l, b: (l, 0, 0)),
                  pl.BlockSpec((1, 1, MEM_WIDTH), lambda l, b: (l, 0, 0))],
        out_specs=[pl.BlockSpec((1, 1, MEM_HEADS, MEM_WIDTH, N_MEM), lambda l, b: (l, b, 0, 0, 0)),
                   pl.BlockSpec((1, 1, MEM_HEADS, N_MEM, MEM_WIDTH), lambda l, b: (l, b, 0, 0, 0))],
        out_shape=[out, out],
        compiler_params=_params("parallel", "parallel"),
        name="mem_prep",
    )(mem, g, w_kv, k_gain)


def _outproj_kernel(x_ref, ys_ref, om_ref, wglu_ref, bglu_ref, gs_ref, gm_ref, wout_ref,
                    gmq_ref, wmq_ref, qg_ref, kt_ref, vm_ref, wmo_ref, gmlp_ref,
                    x2_ref, h3_ref):
    ys_b = ys_ref[0]
    ys = ys_b.astype(F32)
    yg = ys * jax.nn.sigmoid(_dot(ys_b, wglu_ref[...]) + bglu_ref[...])
    n1 = _rms(yg, gs_ref[...]).astype(BF16)
    n2 = _rms(om_ref[0].astype(F32), gm_ref[...]).astype(BF16)
    x1 = x_ref[0] + _dot(n1, wout_ref[:SSM_WIDTH, :]) + _dot(n2, wout_ref[SSM_WIDTH:, :])

    q = _dot(_rms(x1, gmq_ref[...]).astype(BF16), wmq_ref[...])
    lane = lax.broadcasted_iota(jnp.int32, q.shape, 1)
    ss = _head_sums(q * q, lane, MEM_HEAD_DIM)
    qn = (q * lax.rsqrt(ss * (1.0 / MEM_HEAD_DIM) + EPS) * qg_ref[...]).astype(BF16)
    o = jnp.zeros(q.shape, F32)
    for hd in range(MEM_HEADS):
        s = _dot(qn, kt_ref[0, hd])
        p = jnp.exp(s - jnp.max(s, axis=-1, keepdims=True))
        inv = 1.0 / jnp.sum(p, axis=-1, keepdims=True)
        o = o + _dot(p.astype(BF16), vm_ref[0, hd]) * inv
    x2 = x1 + _dot(o.astype(BF16), wmo_ref[...])
    x2_ref[0] = x2
    h3_ref[0] = _rms(x2, gmlp_ref[...]).astype(BF16)


def _outproj(x, ys, om, lw, kt, vm, tm):
    bsz, seq, _ = x.shape
    full = lambda *s: pl.BlockSpec(s, lambda b, i: (0,) * len(s))
    tok = lambda w: pl.BlockSpec((1, tm, w), lambda b, i: (b, i, 0))
    memb = pl.BlockSpec((1, MEM_HEADS, N_MEM, MEM_WIDTH), lambda b, i: (b, 0, 0, 0))
    return pl.pallas_call(
        _outproj_kernel,
        grid=(bsz, seq // tm),
        in_specs=[tok(D_MODEL), tok(SSM_WIDTH), tok(MLA_WIDTH),
                  full(SSM_WIDTH, SSM_WIDTH), full(1, SSM_WIDTH), full(1, SSM_WIDTH), full(1, MLA_WIDTH),
                  full(D_MODEL, D_MODEL), full(1, D_MODEL), full(D_MODEL, MEM_WIDTH), full(1, MEM_WIDTH),
                  memb, memb, full(MEM_WIDTH, D_MODEL), full(1, D_MODEL)],
        out_specs=[tok(D_MODEL), tok(D_MODEL)],
        out_shape=[jax.ShapeDtypeStruct((bsz, seq, D_MODEL), F32),
                   jax.ShapeDtypeStruct((bsz, seq, D_MODEL), BF16)],
        compiler_params=_params("parallel", "parallel"),
        name="outproj",
    )(x, ys, om, lw["w_glu"], lw["b_glu"], lw["g_ssm"], lw["g_mla"], lw["w_out"],
      lw["g_memq"], lw["w_mq"], lw["mem_q_gain"], kt, vm, lw["w_mo"], lw["g_mlp"])


def _mlp_kernel(x_ref, h_ref, w1_ref, w2_ref, o_ref, *, ff_tile):
    h = h_ref[...]
    acc = x_ref[...]
    for c in range(D_FF // ff_tile):
        a = jnp.maximum(_dot(h, w1_ref[:, c * ff_tile:(c + 1) * ff_tile]), 0.0)
        acc = acc + _dot((a * a).astype(BF16), w2_ref[c * ff_tile:(c + 1) * ff_tile, :])
    o_ref[...] = acc


def _mlp(x, h, w1, w2, tm):
    t = x.shape[0]
    tok = pl.BlockSpec((tm, D_MODEL), lambda i: (i, 0))
    return pl.pallas_call(
        functools.partial(_mlp_kernel, ff_tile=1024),
        grid=(t // tm,),
        in_specs=[tok, tok,
                  pl.BlockSpec((D_MODEL, D_FF), lambda i: (0, 0)),
                  pl.BlockSpec((D_FF, D_MODEL), lambda i: (0, 0))],
        out_specs=tok,
        out_shape=jax.ShapeDtypeStruct((t, D_MODEL), F32),
        compiler_params=_params("parallel"),
        name="mlp",
    )(x, h, w1, w2)


def _half_swap(w):
    h = w.shape[-1] // 2
    return jnp.concatenate([-w[..., h:], w[..., :h]], axis=-1)


def _half_swap_unsigned(w):
    h = w.shape[-1] // 2
    return jnp.concatenate([w[..., h:], w[..., :h]], axis=-1)


def _trunk(tm, tq, x, mem, positions, norm_mix, w_in, ssm_lambda_re, ssm_lambda_im, ssm_log_step, ssm_b_re, ssm_b_im, ssm_c_re, ssm_c_im, ssm_d, ssm_w_glu, ssm_b_glu, mla_q_norm, mla_w_uq, mla_kv_norm, mla_w_ukv, mla_q_gain, mla_k_gain, out_norm_ssm, out_norm_mla, w_out, norm_mem_q, norm_mem_kv, mem_w_q, mem_w_kv, mem_q_gain, mem_k_gain, mem_w_o, norm_mlp, mlp_w1, mlp_w2):
    bsz, seq, _ = x.shape
    depth = norm_mix.shape[0]

    s3 = SSM_WIDTH + Q_LORA + KV_LORA
    k_rope_w = w_in[..., s3:]
    w_in_x = jnp.concatenate(
        [w_in[..., :s3], jnp.zeros((depth, D_MODEL, QK_NOPE), F32), k_rope_w, _half_swap(k_rope_w)],
        axis=-1).astype(BF16)
    wq = mla_w_uq.reshape(depth, Q_LORA, MLA_HEADS, QK_DIM)
    w_uq_x = jnp.concatenate([wq, _half_swap(wq[..., QK_NOPE:])], axis=-1)
    w_uq_x = w_uq_x.reshape(depth, Q_LORA, MLA_HEADS * HEAD_PAD).astype(BF16)
    q_gain_x = jnp.concatenate([mla_q_gain, _half_swap_unsigned(mla_q_gain[:, QK_NOPE:])], axis=-1)
    zeros64 = jnp.zeros((depth, QK_NOPE), F32)
    k_gain_nope = jnp.concatenate([mla_k_gain[:, :QK_NOPE], zeros64], axis=-1)
    k_gain_rope = jnp.concatenate(
        [zeros64, mla_k_gain[:, QK_NOPE:], _half_swap_unsigned(mla_k_gain[:, QK_NOPE:])], axis=-1)
    wkv = mem_w_kv.reshape(depth, D_MODEL, MEM_HEADS, 2, MEM_HEAD_DIM)
    w_mkv = wkv.transpose(0, 1, 3, 2, 4).reshape(depth, D_MODEL, 2 * MEM_WIDTH).astype(BF16)

    row = lambda a: a[:, None, :]
    ra, rb = _rope_tables(positions)
    prep = _s5_prep(ssm_lambda_re, ssm_lambda_im, ssm_log_step, ssm_b_re, ssm_b_im, ssm_c_re, ssm_c_im)
    kt_all, vm_all = _mem_prep(mem, row(norm_mem_kv), w_mkv, row(jnp.tile(mem_k_gain, (1, MEM_HEADS))))
    dx = jnp.tile(ssm_d.reshape(depth, SSM_GROUPS, 1, SSM_GROUP), (1, 1, 1, SSM_CHUNK))

    w_glu_b = ssm_w_glu.astype(BF16)
    w_ukv_b = mla_w_ukv.astype(BF16)
    w_out_b = w_out.astype(BF16)
    w_mq_b = mem_w_q.astype(BF16)
    w_mo_b = mem_w_o.astype(BF16)
    w1_b = mlp_w1.astype(BF16)
    w2_b = mlp_w2.astype(BF16)
    mem_q_gain_x = jnp.tile(mem_q_gain, (1, MEM_HEADS))

    for l in range(depth):
        lw = dict(g_mix=norm_mix[l:l + 1], w_in=w_in_x[l], g_q=mla_q_norm[l:l + 1], w_uq=w_uq_x[l],
                  g_kv=mla_kv_norm[l:l + 1], w_ukv=w_ukv_b[l], q_gain=q_gain_x[l:l + 1],
                  k_gain_nope=k_gain_nope[l:l + 1], k_gain_rope=k_gain_rope[l:l + 1],
                  w_glu=w_glu_b[l], b_glu=ssm_b_glu[l:l + 1], g_ssm=out_norm_ssm[l:l + 1],
                  g_mla=out_norm_mla[l:l + 1], w_out=w_out_b[l], g_memq=norm_mem_q[l:l + 1],
                  w_mq=w_mq_b[l], mem_q_gain=mem_q_gain_x[l:l + 1], w_mo=w_mo_b[l],
                  g_mlp=norm_mlp[l:l + 1])
        u, q, k, v = _inproj(x, lw, ra, rb, tm)
        ys = _s5_mix(u, tuple(p[l] for p in prep), dx[l])
        om = _flash(q, k, v, tq, FLASH_HEADS)
        x2, h3 = _outproj(x, ys, om, lw, kt_all[l], vm_all[l], tm)
        x = _mlp(x2.reshape(bsz * seq, D_MODEL), h3.reshape(bsz * seq, D_MODEL),
                 w1_b[l], w2_b[l], tm).reshape(bsz, seq, D_MODEL)
    return x


def kernel(x, mem, positions, norm_mix, w_in, ssm_lambda_re, ssm_lambda_im, ssm_log_step, ssm_b_re, ssm_b_im, ssm_c_re, ssm_c_im, ssm_d, ssm_w_glu, ssm_b_glu, mla_q_norm, mla_w_uq, mla_kv_norm, mla_w_ukv, mla_q_gain, mla_k_gain, out_norm_ssm, out_norm_mla, w_out, norm_mem_q, norm_mem_kv, mem_w_q, mem_w_kv, mem_q_gain, mem_k_gain, mem_w_o, norm_mlp, mlp_w1, mlp_w2):
    seq = x.shape[1]
    return _trunk(min(512, seq), min(512, seq), x, mem, positions, norm_mix, w_in, ssm_lambda_re, ssm_lambda_im, ssm_log_step, ssm_b_re, ssm_b_im, ssm_c_re, ssm_c_im, ssm_d, ssm_w_glu, ssm_b_glu, mla_q_norm, mla_w_uq, mla_kv_norm, mla_w_ukv, mla_q_gain, mla_k_gain, out_norm_ssm, out_norm_mla, w_out, norm_mem_q, norm_mem_kv, mem_w_q, mem_w_kv, mem_q_gain, mem_k_gain, mem_w_o, norm_mlp, mlp_w1, mlp_w2)
```

```python
import functools
import math

import jax
import jax.numpy as jnp
from jax import lax
from jax.experimental import pallas as pl
from jax.experimental.pallas import tpu as pltpu

D_MODEL = 1024
N_MEM = 256
MEM_HEADS = 4
MEM_HEAD_DIM = 64
MEM_WIDTH = MEM_HEADS * MEM_HEAD_DIM
SSM_WIDTH = 512
MLA_WIDTH = 512
SSM_GROUP = 16
SSM_GROUPS = 32
SSM_STATE = 64
MLA_HEADS = 8
QK_NOPE = 64
QK_ROPE = 32
QK_DIM = QK_NOPE + QK_ROPE
V_DIM = 64
Q_LORA = 256
KV_LORA = 128
ROPE_THETA = 10000.0
D_FF = 4 * D_MODEL
EPS = 1e-6

LANES = 128
HEAD_PAD = 128
SSM_CHUNK = 32
CHUNK_W = SSM_CHUNK * SSM_GROUP
GROUPS_PER_TILE = LANES // SSM_GROUP
LANE_TILES = SSM_WIDTH // LANES
IN_COLS_PAD = SSM_WIDTH + Q_LORA + KV_LORA + 2 * LANES
FLASH_HEADS = 4
VMEM_LIMIT = 56 * 1024 * 1024

F32 = jnp.float32
BF16 = jnp.bfloat16
HIGHEST = lax.Precision.HIGHEST


def _dot(a, b):
    return jnp.dot(a, b, preferred_element_type=F32)


def _dot_exact(a, b):
    return jnp.dot(a, b, precision=HIGHEST, preferred_element_type=F32)


def _dot_nt(a, b):
    return lax.dot_general(a, b, (((1,), (1,)), ((), ())), preferred_element_type=F32)


def _rms(v, gain):
    return v * lax.rsqrt(jnp.mean(v * v, axis=-1, keepdims=True) + EPS) * gain


def _params(*sem):
    return pltpu.CompilerParams(dimension_semantics=sem, vmem_limit_bytes=VMEM_LIMIT)


def _layer_spec(l, *shape):
    return pl.BlockSpec((None,) + shape, lambda *_: (l,) + (0,) * len(shape))


def _const_spec(*shape):
    return pl.BlockSpec(shape, lambda *_: (0,) * len(shape))


def _rope_table_kernel(pos_ref, freq_ref, a_ref, b_ref):
    ang = pos_ref[...].astype(F32) * freq_ref[...]
    lane = lax.broadcasted_iota(jnp.int32, ang.shape, 1)
    in_rope = (lane >= QK_NOPE) & (lane < QK_DIM)
    a_ref[...] = jnp.where(lane < QK_NOPE, 1.0, jnp.where(in_rope, jnp.cos(ang), 0.0))
    b_ref[...] = jnp.where(in_rope, jnp.sin(ang), 0.0)


def _rope_tables(positions):
    bsz, seq = positions.shape
    t = bsz * seq
    tm = min(1024, t)
    half = QK_ROPE // 2
    inv_freq = ROPE_THETA ** (-jnp.arange(half, dtype=F32) / half)
    freq = jnp.zeros((1, LANES), F32).at[0, QK_NOPE:QK_DIM].set(jnp.tile(inv_freq, 2))
    a, b = pl.pallas_call(
        _rope_table_kernel,
        grid=(t // tm,),
        in_specs=[pl.BlockSpec((tm, 1), lambda i: (i, 0)), _const_spec(1, LANES)],
        out_specs=[pl.BlockSpec((tm, LANES), lambda i: (i, 0))] * 2,
        out_shape=[jax.ShapeDtypeStruct((t, LANES), F32)] * 2,
        compiler_params=_params("parallel"),
        name="rope_tables",
    )(positions.reshape(t, 1), freq)
    return a.reshape(bsz, seq, LANES), b.reshape(bsz, seq, LANES)


def _inproj_kernel(x_ref, gmix_ref, win_ref, gq_ref, wqa_ref, wqb_ref, gkv_ref, wk_ref, wv_ref,
                   gains_ref, bd_ref, ra_ref, rb_ref, u_ref, q_ref, k_ref, v_ref):
    h = _rms(x_ref[...], gmix_ref[...]).astype(BF16)
    proj = _dot(h, win_ref[...])
    for t in range(LANE_TILES):
        u_ref[t] = proj[:, t * LANES:(t + 1) * LANES]

    two = lambda a: jnp.concatenate([a, a], axis=1)
    ra, rb = ra_ref[...], rb_ref[...]
    gains = gains_ref[...]
    bd = bd_ref[...]
    scale = math.log2(math.e) / math.sqrt(QK_DIM)
    inv_d = 1.0 / QK_DIM

    o1 = SSM_WIDTH
    o2 = o1 + Q_LORA
    o3 = o2 + KV_LORA
    hq = _rms(proj[:, o1:o2], gq_ref[...]).astype(BF16)
    qa = _dot(hq, wqa_ref[...])
    qb = _dot(hq, wqb_ref[...])
    hkv = _rms(proj[:, o2:o3], gkv_ref[...]).astype(BF16)
    kn = _dot(hkv, wk_ref[...])
    vm = _dot(hkv, wv_ref[...])

    kr = proj[:, o3:o3 + LANES]
    kr_sw = proj[:, o3 + LANES:o3 + 2 * LANES]
    kr_ss = two(_dot((kr * kr).astype(BF16), bd[:LANES, :LANES]))
    krot = two(kr * (gains[3:4] * ra) + kr_sw * (gains[4:5] * rb))
    ga, gb, gk = two(ra * gains[0:1]), two(rb * gains[1:2]), two(gains[2:3])
    lane = lax.broadcasted_iota(jnp.int32, ga.shape, 1)
    ones_col = jnp.where((lane & (LANES - 1)) == V_DIM, 1.0, 0.0)

    for hp in range(MLA_HEADS // 2):
        sl = slice(2 * hp * HEAD_PAD, (2 * hp + 2) * HEAD_PAD)
        q2 = qa[:, sl]
        rq = lax.rsqrt(_dot((q2 * q2).astype(BF16), bd) * inv_d + EPS) * scale
        qo = ((q2 * ga + qb[:, sl] * gb) * rq).astype(BF16)
        k2 = kn[:, sl]
        rk = lax.rsqrt((_dot((k2 * k2).astype(BF16), bd) + kr_ss) * inv_d + EPS)
        ko = ((k2 * gk + krot) * rk).astype(BF16)
        vo = (vm[:, sl] + ones_col).astype(BF16)
        for j in range(2):
            q_ref[2 * hp + j] = qo[:, j * HEAD_PAD:(j + 1) * HEAD_PAD]
            k_ref[2 * hp + j] = ko[:, j * HEAD_PAD:(j + 1) * HEAD_PAD]
            v_ref[2 * hp + j] = vo[:, j * HEAD_PAD:(j + 1) * HEAD_PAD]


def _inproj(l, x, w, ra, rb, tm):
    bsz, seq, _ = x.shape
    tok = lambda width: pl.BlockSpec((None, tm, width), lambda b, i: (b, i, 0))
    head = pl.BlockSpec((None, MLA_HEADS, tm, HEAD_PAD), lambda b, i: (b, 0, i, 0))
    hw = MLA_HEADS * HEAD_PAD
    head_shape = jax.ShapeDtypeStruct((bsz, MLA_HEADS, seq, HEAD_PAD), BF16)
    return pl.pallas_call(
        _inproj_kernel,
        grid=(bsz, seq // tm),
        in_specs=[tok(D_MODEL), _layer_spec(l, 1, D_MODEL), _layer_spec(l, D_MODEL, IN_COLS_PAD),
                  _layer_spec(l, 1, Q_LORA), _layer_spec(l, Q_LORA, hw), _layer_spec(l, Q_LORA, hw),
                  _layer_spec(l, 1, KV_LORA), _layer_spec(l, KV_LORA, hw), _layer_spec(l, KV_LORA, hw),
                  _layer_spec(l, 8, HEAD_PAD), _const_spec(2 * HEAD_PAD, 2 * HEAD_PAD),
                  tok(LANES), tok(LANES)],
        out_specs=[pl.BlockSpec((LANE_TILES, None, tm, LANES), lambda b, i: (0, b, i, 0)), head, head, head],
        out_shape=[jax.ShapeDtypeStruct((LANE_TILES, bsz, seq, LANES), F32), head_shape, head_shape, head_shape],
        compiler_params=_params("parallel", "parallel"),
        name="inproj",
    )(x, w["g_mix"], w["w_in"], w["g_q"], w["w_qa"], w["w_qb"], w["g_kv"], w["w_k"], w["w_v"],
      w["mla_gains"], w["bd"], ra, rb)


def _s5_prep_kernel(pcol_ref, prow_ref, b_re_ref, b_im_ref, bt_re_ref, bt_im_ref,
                    ct_re_ref, ct_im_ref, sel_ref, reps_ref,
                    m_ref, wst_ref, wo_ref, apa_ref, apb_ref):
    def zoh(lr, li, ls):
        step = jnp.exp(ls)
        mag = jnp.exp(lr * step)
        ang = li * step
        xr = mag * jnp.cos(ang) - 1.0
        xi = mag * jnp.sin(ang)
        den = lr * lr + li * li
        return (xr * lr + xi * li) / den, (xi * lr - xr * li) / den, step

    pc = pcol_ref[...]
    lr, li = pc[:, 0:1], pc[:, 1:2]
    cfr, cfi, step = zoh(lr, li, pc[:, 2:3])
    b_re, b_im = b_re_ref[...], b_im_ref[...]
    bb_re = cfr * b_re - cfi * b_im
    bb_im = cfr * b_im + cfi * b_re

    pr = prow_ref[...]
    lr_row, li_row, ls_row = pr[0:1], pr[1:2], pr[2:3]
    cfr_row, cfi_row, step_row = zoh(lr_row, li_row, ls_row)
    bt_re, bt_im = bt_re_ref[...], bt_im_ref[...]
    bbt_re = cfr_row[:, :SSM_STATE] * bt_re - cfi_row[:, :SSM_STATE] * bt_im
    bbt_im = cfr_row[:, :SSM_STATE] * bt_im + cfi_row[:, :SSM_STATE] * bt_re

    sel = sel_ref[...]
    brep_re, brep_im = _dot_exact(bb_re, sel), _dot_exact(bb_im, sel)
    crep_re, crep_im = _dot_exact(ct_re_ref[...], sel), _dot_exact(ct_im_ref[...], sel)

    n = lax.broadcasted_iota(jnp.int32, (SSM_STATE, LANES), 1).astype(F32)
    mag = jnp.exp(lr * step * n)
    th = li * step * n
    pw_re, pw_im = mag * jnp.cos(th), mag * jnp.sin(th)
    apow = lambda i: (_dot_exact(pw_re, reps_ref[i]), _dot_exact(pw_im, reps_ref[i]))

    e_re, e_im = apow(0)
    g_re = e_re * crep_re - e_im * crep_im
    g_im = e_re * crep_im + e_im * crep_re
    kt = _dot_exact(bbt_re, g_re) - _dot_exact(bbt_im, g_im)
    klane = lax.broadcasted_iota(jnp.int32, kt.shape, 1)
    m_ref[0:SSM_GROUP, :] = kt.astype(BF16)
    for s in range(1, SSM_CHUNK):
        shifted = jnp.where(klane >= s * SSM_GROUP, pltpu.roll(kt, s * SSM_GROUP, axis=1), 0.0)
        m_ref[s * SSM_GROUP:(s + 1) * SSM_GROUP, :] = shifted.astype(BF16)

    e_re, e_im = apow(1)
    wst_ref[0:SSM_STATE, :] = (e_re * brep_re - e_im * brep_im).astype(BF16)
    wst_ref[SSM_STATE:, :] = (e_re * brep_im + e_im * brep_re).astype(BF16)

    e_re, e_im = apow(2)
    wo_ref[0:SSM_STATE, :] = (e_re * crep_re - e_im * crep_im).astype(BF16)
    wo_ref[SSM_STATE:, :] = (-(e_re * crep_im + e_im * crep_re)).astype(BF16)

    krow = lax.broadcasted_iota(jnp.int32, (8, LANES), 0)
    klan = lax.broadcasted_iota(jnp.int32, (8, LANES), 1)
    nn = (SSM_CHUNK * jnp.left_shift(1, krow)).astype(F32)
    mag = jnp.exp(lr_row * step_row * nn)
    th = li_row * step_row * nn
    p_re, p_im = mag * jnp.cos(th), mag * jnp.sin(th)
    apa_ref[...] = p_re
    apb_ref[...] = jnp.where(klan < SSM_STATE, -p_im, p_im)


def _s5_prep(lam_re, lam_im, log_step, b_re, b_im, c_re, c_im):
    nl, g, p = lam_re.shape
    hh = SSM_GROUP
    ls = jnp.broadcast_to(log_step[..., None], (nl, g, p))
    pcol = jnp.zeros((nl, g, p, 8), F32)
    pcol = pcol.at[..., 0].set(lam_re).at[..., 1].set(lam_im).at[..., 2].set(ls)
    dup = lambda v: jnp.concatenate([v, v], axis=-1)
    prow = jnp.zeros((nl, g, 8, LANES), F32)
    prow = prow.at[:, :, 0].set(dup(lam_re)).at[:, :, 1].set(dup(lam_im)).at[:, :, 2].set(dup(ls))
    sel = jnp.tile(jnp.eye(hh, dtype=F32), (1, SSM_CHUNK))
    tau = jnp.arange(CHUNK_W) // hh
    nrow = jnp.arange(LANES)[:, None]
    reps = jnp.stack([nrow == tau, nrow == SSM_CHUNK - 1 - tau, nrow == tau + 1]).astype(F32)
    blk = lambda *s: pl.BlockSpec((None, None) + s, lambda l, i: (l, i) + (0,) * len(s))
    return pl.pallas_call(
        _s5_prep_kernel,
        grid=(nl, g),
        in_specs=[blk(p, 8), blk(8, LANES), blk(p, hh), blk(p, hh), blk(hh, p), blk(hh, p),
                  blk(p, hh), blk(p, hh), _const_spec(hh, CHUNK_W), _const_spec(3, LANES, CHUNK_W)],
        out_specs=[blk(CHUNK_W, CHUNK_W), blk(2 * p, CHUNK_W), blk(2 * p, CHUNK_W),
                   blk(8, LANES), blk(8, LANES)],
        out_shape=[jax.ShapeDtypeStruct((nl, g, CHUNK_W, CHUNK_W), BF16),
                   jax.ShapeDtypeStruct((nl, g, 2 * p, CHUNK_W), BF16),
                   jax.ShapeDtypeStruct((nl, g, 2 * p, CHUNK_W), BF16),
                   jax.ShapeDtypeStruct((nl, g, 8, LANES), F32),
                   jax.ShapeDtypeStruct((nl, g, 8, LANES), F32)],
        compiler_params=_params("parallel", "parallel"),
        name="s5_prep",
    )(pcol, prow, b_re, b_im, jnp.swapaxes(b_re, -1, -2), jnp.swapaxes(b_im, -1, -2),
      jnp.swapaxes(c_re, -1, -2), jnp.swapaxes(c_im, -1, -2), sel, reps)


def _s5_mix_kernel(u_ref, perm_ref, m_ref, wst_ref, wo_ref, apa_ref, apb_ref, dx_ref, y_ref,
                   xs_scr, ys_scr, *, nsteps):
    nc = u_ref.shape[0] // SSM_CHUNK
    gpt = GROUPS_PER_TILE
    perm = perm_ref[...]
    for j in range(SSM_CHUNK // gpt):
        slab = jnp.concatenate(
            [u_ref[pl.ds(gpt * j + s, nc, stride=SSM_CHUNK), :] for s in range(gpt)], axis=1)
        z = _dot(slab.astype(BF16), perm).astype(BF16)
        for g in range(gpt):
            xs_scr[g, :, j * LANES:(j + 1) * LANES] = z[:, g * LANES:(g + 1) * LANES]

    c = lax.broadcasted_iota(jnp.int32, (nc, 2 * SSM_STATE), 0)
    for g in range(gpt):
        x = xs_scr[g]
        y = _dot(x, m_ref[g])
        st = _dot_nt(x, wst_ref[g])
        apa, apb = apa_ref[g], apb_ref[g]
        for k in range(nsteps):
            d = 1 << k
            sh = jnp.where(c >= d, pltpu.roll(st, d, axis=0), 0.0)
            st = st + sh * apa[k:k + 1] + pltpu.roll(sh, SSM_STATE, axis=1) * apb[k:k + 1]
        carried = jnp.where(c >= 1, pltpu.roll(st, 1, axis=0), 0.0)
        y = y + _dot(carried.astype(BF16), wo_ref[g]) + dx_ref[g] * x.astype(F32)
        ys_scr[g] = jax.nn.gelu(y).astype(BF16)

    for j in range(SSM_CHUNK // gpt):
        w = jnp.concatenate([ys_scr[g, :, j * LANES:(j + 1) * LANES] for g in range(gpt)], axis=1)
        o = _dot(w, perm)
        for t in range(gpt):
            y_ref[pl.ds(gpt * j + t, nc, stride=SSM_CHUNK), :] = o[:, t * LANES:(t + 1) * LANES]


def _s5_mix(l, u, prep, dx, perm):
    tiles, bsz, seq, _ = u.shape
    cps = seq // SSM_CHUNK
    assert cps & (cps - 1) == 0 and cps <= 256
    gpt = GROUPS_PER_TILE
    m, wst, wo, apa, apb = prep
    tokens = pl.BlockSpec((None, None, seq, LANES), lambda t, b: (t, b, 0, 0))
    grp = lambda *s: pl.BlockSpec((None, gpt) + s, lambda t, b: (l, t) + (0,) * len(s))
    return pl.pallas_call(
        functools.partial(_s5_mix_kernel, nsteps=cps.bit_length() - 1),
        grid=(tiles, bsz),
        in_specs=[tokens, _const_spec(gpt * LANES, gpt * LANES), grp(CHUNK_W, CHUNK_W),
                  grp(2 * SSM_STATE, CHUNK_W), grp(2 * SSM_STATE, CHUNK_W), grp(8, LANES), grp(8, LANES),
                  grp(1, CHUNK_W)],
        out_specs=tokens,
        out_shape=jax.ShapeDtypeStruct(u.shape, F32),
        scratch_shapes=[pltpu.VMEM((gpt, cps, CHUNK_W), BF16), pltpu.VMEM((gpt, cps, CHUNK_W), BF16)],
        compiler_params=_params("parallel", "parallel"),
        name="s5_mix",
    )(u, perm, m, wst, wo, apa, apb, dx)


def _flash_kernel(qt_ref, kt_ref, q_ref, k_ref, v_ref, o_ref, m_scr, acc_scr, *, heads):
    p_idx = pl.program_id(2)
    qi = qt_ref[p_idx]
    ki = kt_ref[p_idx]
    tk = k_ref.shape[2]

    @pl.when(ki == 0)
    def _():
        m_scr[...] = jnp.full(m_scr.shape, -jnp.inf, F32)
        acc_scr[...] = jnp.zeros(acc_scr.shape, F32)

    def step(masked):
        for j in range(heads):
            s = _dot_nt(q_ref[0, j], k_ref[0, j])
            if masked:
                row = lax.broadcasted_iota(jnp.int32, s.shape, 0)
                col = lax.broadcasted_iota(jnp.int32, s.shape, 1)
                s = jnp.where(col <= row, s, -jnp.inf)
            m_prev = m_scr[j]
            m_new = jnp.maximum(m_prev, jnp.max(s, axis=-1, keepdims=True))
            alpha = jnp.exp2(m_prev - m_new)
            p = jnp.exp2(s - jnp.tile(m_new, (1, tk // LANES)))
            acc_scr[j] = alpha * acc_scr[j] + _dot(p.astype(BF16), v_ref[0, j])
            m_scr[j] = m_new

    @pl.when(ki < qi)
    def _():
        step(False)

    @pl.when(ki == qi)
    def _():
        step(True)
        outs = []
        for j in range(heads):
            acc = acc_scr[j]
            outs.append(acc[:, :V_DIM] / acc[:, V_DIM:V_DIM + 1])
        o_ref[0] = jnp.concatenate(outs, axis=-1).astype(BF16)


def _flash(q, k, v, tq, heads_per_step):
    bsz, heads, seq, _ = q.shape
    hp = heads_per_step
    nq = seq // tq
    pairs = [(a, b) for a in range(nq) for b in range(a + 1)]
    qt = jnp.asarray([a for a, _ in pairs], jnp.int32)
    kt = jnp.asarray([b for _, b in pairs], jnp.int32)
    spec = lambda tbl: pl.BlockSpec((1, hp, tq, HEAD_PAD),
                                    lambda b, h, p, qt, kt: (b, h, (qt, kt)[tbl][p], 0))
    grid_spec = pltpu.PrefetchScalarGridSpec(
        num_scalar_prefetch=2,
        grid=(bsz, heads // hp, len(pairs)),
        in_specs=[spec(0), spec(1), spec(1)],
        out_specs=pl.BlockSpec((1, tq, hp * V_DIM), lambda b, h, p, qt, kt: (b, qt[p], h)),
        scratch_shapes=[pltpu.VMEM((hp, tq, LANES), F32), pltpu.VMEM((hp, tq, HEAD_PAD), F32)])
    return pl.pallas_call(
        functools.partial(_flash_kernel, heads=hp),
        grid_spec=grid_spec,
        out_shape=jax.ShapeDtypeStruct((bsz, seq, heads * V_DIM), BF16),
        compiler_params=_params("parallel", "parallel", "arbitrary"),
        name="flash",
    )(qt, kt, q, k, v)


def _head_sums(v2, lane, width):
    out = jnp.zeros_like(v2)
    for hd in range(v2.shape[-1] // width):
        msk = (lane >= hd * width) & (lane < (hd + 1) * width)
        out = out + jnp.where(msk, jnp.sum(jnp.where(msk, v2, 0.0), axis=-1, keepdims=True), 0.0)
    return out


def _mem_prep_kernel(mem_ref, g_ref, w_ref, kg_ref, kt_ref, vm_ref):
    hm = _rms(mem_ref[0], g_ref[0]).astype(BF16)
    kv = _dot(hm, w_ref[0])
    k, v = kv[:, :MEM_WIDTH], kv[:, MEM_WIDTH:]
    lane = lax.broadcasted_iota(jnp.int32, k.shape, 1)
    ss = _head_sums(k * k, lane, MEM_HEAD_DIM)
    kn = k * lax.rsqrt(ss * (1.0 / MEM_HEAD_DIM) + EPS) * kg_ref[0] * (1.0 / math.sqrt(MEM_HEAD_DIM))
    knt = kn.T
    row = lax.broadcasted_iota(jnp.int32, knt.shape, 0)
    for hd in range(MEM_HEADS):
        lo, hi = hd * MEM_HEAD_DIM, (hd + 1) * MEM_HEAD_DIM
        kt_ref[0, 0, hd] = jnp.where((row >= lo) & (row < hi), knt, 0.0).astype(BF16)
        vm_ref[0, 0, hd] = jnp.where((lane >= lo) & (lane < hi), v, 0.0).astype(BF16)


def _mem_prep(mem, g, w_kv, k_gain):
    nl = g.shape[0]
    bsz = mem.shape[0]
    out = jax.ShapeDtypeStruct((nl, bsz, MEM_HEADS, N_MEM, MEM_WIDTH), BF16)
    return pl.pallas_call(
        _mem_prep_kernel,
        grid=(nl, bsz),
        in_specs=[pl.BlockSpec((1, N_MEM, D_MODEL), lambda l, b: (b, 0, 0)),
                  pl.BlockSpec((1, 1, D_MODEL), lambda l, b: (l, 0, 0)),
                  pl.BlockSpec((1, D_MODEL, 2 * MEM_WIDTH), lambda l, b: (l, 0, 0)),
                  pl.BlockSpec((1, 1, MEM_WIDTH), lambda l, b: (l, 0, 0))],
        out_specs=[pl.BlockSpec((1, 1, MEM_HEADS, MEM_WIDTH, N_MEM), lambda l, b: (l, b, 0, 0, 0)),
                   pl.BlockSpec((1, 1, MEM_HEADS, N_MEM, MEM_WIDTH), lambda l, b: (l, b, 0, 0, 0))],
        out_shape=[out, out],
        compiler_params=_params("parallel", "parallel"),
        name="mem_prep",
    )(mem, g, w_kv, k_gain)


def _outproj_kernel(x_ref, ys_ref, om_ref, wglu_ref, bglu_ref, gs_ref, gm_ref, wout_ref,
                    gmq_ref, wmq_ref, qg_ref, kt_ref, vm_ref, wmo_ref, gmlp_ref,
                    x2_ref, h3_ref):
    ys = jnp.concatenate([ys_ref[t] for t in range(LANE_TILES)], axis=1)
    yg = ys * jax.nn.sigmoid(_dot(ys.astype(BF16), wglu_ref[...]) + bglu_ref[...])
    n1 = _rms(yg, gs_ref[...]).astype(BF16)
    n2 = _rms(om_ref[...].astype(F32), gm_ref[...]).astype(BF16)
    x1 = x_ref[...] + _dot(n1, wout_ref[:SSM_WIDTH, :]) + _dot(n2, wout_ref[SSM_WIDTH:, :])

    q = _dot(_rms(x1, gmq_ref[...]).astype(BF16), wmq_ref[...])
    lane = lax.broadcasted_iota(jnp.int32, q.shape, 1)
    ss = _head_sums(q * q, lane, MEM_HEAD_DIM)
    qn = (q * lax.rsqrt(ss * (1.0 / MEM_HEAD_DIM) + EPS) * qg_ref[...]).astype(BF16)
    o = jnp.zeros(q.shape, F32)
    for hd in range(MEM_HEADS):
        s = _dot(qn, kt_ref[hd])
        p = jnp.exp(s - jnp.max(s, axis=-1, keepdims=True))
        inv = 1.0 / jnp.sum(p, axis=-1, keepdims=True)
        o = o + _dot(p.astype(BF16), vm_ref[hd]) * inv
    x2 = x1 + _dot(o.astype(BF16), wmo_ref[...])
    x2_ref[...] = x2
    h3_ref[...] = _rms(x2, gmlp_ref[...]).astype(BF16)


def _outproj(l, x, ys, om, w, kt, vm, tm):
    bsz, seq, _ = x.shape
    tok = lambda width: pl.BlockSpec((None, tm, width), lambda b, i: (b, i, 0))
    memb = pl.BlockSpec((None, None, MEM_HEADS, N_MEM, MEM_WIDTH), lambda b, i: (l, b, 0, 0, 0))
    return pl.pallas_call(
        _outproj_kernel,
        grid=(bsz, seq // tm),
        in_specs=[tok(D_MODEL), pl.BlockSpec((LANE_TILES, None, tm, LANES), lambda b, i: (0, b, i, 0)),
                  tok(MLA_WIDTH),
                  _layer_spec(l, SSM_WIDTH, SSM_WIDTH), _layer_spec(l, 1, SSM_WIDTH),
                  _layer_spec(l, 1, SSM_WIDTH), _layer_spec(l, 1, MLA_WIDTH),
                  _layer_spec(l, D_MODEL, D_MODEL), _layer_spec(l, 1, D_MODEL),
                  _layer_spec(l, D_MODEL, MEM_WIDTH), _layer_spec(l, 1, MEM_WIDTH),
                  memb, memb, _layer_spec(l, MEM_WIDTH, D_MODEL), _layer_spec(l, 1, D_MODEL)],
        out_specs=[tok(D_MODEL), tok(D_MODEL)],
        out_shape=[jax.ShapeDtypeStruct((bsz, seq, D_MODEL), F32),
                   jax.ShapeDtypeStruct((bsz, seq, D_MODEL), BF16)],
        compiler_params=_params("parallel", "parallel"),
        name="outproj",
    )(x, ys, om, w["w_glu"], w["b_glu"], w["g_ssm"], w["g_mla"], w["w_out"],
      w["g_memq"], w["w_mq"], w["mem_q_gain"], kt, vm, w["w_mo"], w["g_mlp"])


def _mlp_kernel(x_ref, h_ref, w1_ref, w2_ref, o_ref, *, ff_tile):
    h = h_ref[...]
    acc = x_ref[...]
    for c in range(D_FF // ff_tile):
        a = jnp.maximum(_dot(h, w1_ref[:, c * ff_tile:(c + 1) * ff_tile]), 0.0)
        acc = acc + _dot((a * a).astype(BF16), w2_ref[c * ff_tile:(c + 1) * ff_tile, :])
    o_ref[...] = acc


def _mlp(l, x, h, w1, w2, tm):
    t = x.shape[0]
    tok = pl.BlockSpec((tm, D_MODEL), lambda i: (i, 0))
    return pl.pallas_call(
        functools.partial(_mlp_kernel, ff_tile=1024),
        grid=(t // tm,),
        in_specs=[tok, tok, _layer_spec(l, D_MODEL, D_FF), _layer_spec(l, D_FF, D_MODEL)],
        out_specs=tok,
        out_shape=jax.ShapeDtypeStruct((t, D_MODEL), F32),
        compiler_params=_params("parallel"),
        name="mlp",
    )(x, h, w1, w2)


def _half_swap(w):
    h = w.shape[-1] // 2
    return jnp.concatenate([-w[..., h:], w[..., :h]], axis=-1)


def _half_swap_unsigned(w):
    h = w.shape[-1] // 2
    return jnp.concatenate([w[..., h:], w[..., :h]], axis=-1)


def _pad_last(w, before, after):
    pads = [(0, 0)] * (w.ndim - 1) + [(before, after)]
    return jnp.pad(w, pads)


def _trunk(tm, tq, x, mem, positions, norm_mix, w_in, ssm_lambda_re, ssm_lambda_im, ssm_log_step, ssm_b_re, ssm_b_im, ssm_c_re, ssm_c_im, ssm_d, ssm_w_glu, ssm_b_glu, mla_q_norm, mla_w_uq, mla_kv_norm, mla_w_ukv, mla_q_gain, mla_k_gain, out_norm_ssm, out_norm_mla, w_out, norm_mem_q, norm_mem_kv, mem_w_q, mem_w_kv, mem_q_gain, mem_k_gain, mem_w_o, norm_mlp, mlp_w1, mlp_w2):
    bsz, seq, _ = x.shape
    depth = norm_mix.shape[0]
    row = lambda a: a[:, None, :]
    tail = HEAD_PAD - QK_DIM

    s3 = SSM_WIDTH + Q_LORA + KV_LORA
    k_rope_w = w_in[..., s3:]
    w_in_x = jnp.concatenate([w_in[..., :s3], _pad_last(k_rope_w, QK_NOPE, tail),
                              _pad_last(_half_swap(k_rope_w), QK_NOPE, tail)], axis=-1).astype(BF16)
    wq = mla_w_uq.reshape(depth, Q_LORA, MLA_HEADS, QK_DIM)
    hw = MLA_HEADS * HEAD_PAD
    w_qa = _pad_last(wq, 0, tail).reshape(depth, Q_LORA, hw).astype(BF16)
    w_qb = _pad_last(_half_swap(wq[..., QK_NOPE:]), QK_NOPE, tail).reshape(depth, Q_LORA, hw).astype(BF16)
    wkv = mla_w_ukv.reshape(depth, KV_LORA, MLA_HEADS, QK_NOPE + V_DIM)
    w_k = _pad_last(wkv[..., :QK_NOPE], 0, HEAD_PAD - QK_NOPE).reshape(depth, KV_LORA, hw).astype(BF16)
    w_v = _pad_last(wkv[..., QK_NOPE:], 0, HEAD_PAD - V_DIM).reshape(depth, KV_LORA, hw).astype(BF16)
    q_rope_g, k_rope_g = mla_q_gain[:, QK_NOPE:], mla_k_gain[:, QK_NOPE:]
    mla_gains = jnp.stack([
        _pad_last(mla_q_gain, 0, tail),
        _pad_last(_half_swap_unsigned(q_rope_g), QK_NOPE, tail),
        _pad_last(mla_k_gain[:, :QK_NOPE], 0, HEAD_PAD - QK_NOPE),
        _pad_last(k_rope_g, QK_NOPE, tail),
        _pad_last(_half_swap_unsigned(k_rope_g), QK_NOPE, tail)], axis=1)
    mla_gains = jnp.pad(mla_gains, ((0, 0), (0, 3), (0, 0)))
    blk = jnp.arange(2 * HEAD_PAD) // HEAD_PAD
    bd = (blk[:, None] == blk[None, :]).astype(BF16)
    wmkv = mem_w_kv.reshape(depth, D_MODEL, MEM_HEADS, 2, MEM_HEAD_DIM)
    w_mkv = wmkv.transpose(0, 1, 3, 2, 4).reshape(depth, D_MODEL, 2 * MEM_WIDTH).astype(BF16)
    idx = jnp.arange(GROUPS_PER_TILE * LANES)
    dest = ((idx // SSM_GROUP) % GROUPS_PER_TILE) * LANES + (idx // LANES) * SSM_GROUP + idx % SSM_GROUP
    perm = (dest[:, None] == idx[None, :]).astype(BF16)

    w = dict(g_mix=row(norm_mix), w_in=w_in_x, g_q=row(mla_q_norm), w_qa=w_qa, w_qb=w_qb,
             g_kv=row(mla_kv_norm), w_k=w_k, w_v=w_v, mla_gains=mla_gains, bd=bd,
             w_glu=ssm_w_glu.astype(BF16), b_glu=row(ssm_b_glu), g_ssm=row(out_norm_ssm),
             g_mla=row(out_norm_mla), w_out=w_out.astype(BF16), g_memq=row(norm_mem_q),
             w_mq=mem_w_q.astype(BF16), mem_q_gain=row(jnp.tile(mem_q_gain, (1, MEM_HEADS))),
             w_mo=mem_w_o.astype(BF16), g_mlp=row(norm_mlp))
    w1_b = mlp_w1.astype(BF16)
    w2_b = mlp_w2.astype(BF16)

    ra, rb = _rope_tables(positions)
    prep = _s5_prep(ssm_lambda_re, ssm_lambda_im, ssm_log_step, ssm_b_re, ssm_b_im, ssm_c_re, ssm_c_im)
    kt_all, vm_all = _mem_prep(mem, row(norm_mem_kv), w_mkv, row(jnp.tile(mem_k_gain, (1, MEM_HEADS))))
    dx = jnp.tile(ssm_d.reshape(depth, SSM_GROUPS, 1, SSM_GROUP), (1, 1, 1, SSM_CHUNK))

    for l in range(depth):
        u, q, k, v = _inproj(l, x, w, ra, rb, tm)
        ys = _s5_mix(l, u, prep, dx, perm)
        om = _flash(q, k, v, tq, FLASH_HEADS)
        x2, h3 = _outproj(l, x, ys, om, w, kt_all, vm_all, tm)
        x = _mlp(l, x2.reshape(bsz * seq, D_MODEL), h3.reshape(bsz * seq, D_MODEL),
                 w1_b, w2_b, tm).reshape(bsz, seq, D_MODEL)
    return x


def kernel(x, mem, positions, norm_mix, w_in, ssm_lambda_re, ssm_lambda_im, ssm_log_step, ssm_b_re, ssm_b_im, ssm_c_re, ssm_c_im, ssm_d, ssm_w_glu, ssm_b_glu, mla_q_norm, mla_w_uq, mla_kv_norm, mla_w_ukv, mla_q_gain, mla_k_gain, out_norm_ssm, out_norm_mla, w_out, norm_mem_q, norm_mem_kv, mem_w_q, mem_w_kv, mem_q_gain, mem_k_gain, mem_w_o, norm_mlp, mlp_w1, mlp_w2):
    seq = x.shape[1]
    return _trunk(min(512, seq), min(512, seq), x, mem, positions, norm_mix, w_in, ssm_lambda_re, ssm_lambda_im, ssm_log_step, ssm_b_re, ssm_b_im, ssm_c_re, ssm_c_im, ssm_d, ssm_w_glu, ssm_b_glu, mla_q_norm, mla_w_uq, mla_kv_norm, mla_w_ukv, mla_q_gain, mla_k_gain, out_norm_ssm, out_norm_mla, w_out, norm_mem_q, norm_mem_kv, mem_w_q, mem_w_kv, mem_q_gain, mem_k_gain, mem_w_o, norm_mlp, mlp_w1, mlp_w2)
```

```python
import functools
import math

import jax
import jax.numpy as jnp
from jax import lax
from jax.experimental import pallas as pl
from jax.experimental.pallas import tpu as pltpu

D_MODEL = 1024
N_MEM = 256
MEM_HEADS = 4
MEM_HEAD_DIM = 64
MEM_WIDTH = MEM_HEADS * MEM_HEAD_DIM
SSM_WIDTH = 512
MLA_WIDTH = 512
SSM_GROUP = 16
SSM_GROUPS = 32
SSM_STATE = 64
MLA_HEADS = 8
QK_NOPE = 64
QK_ROPE = 32
QK_DIM = QK_NOPE + QK_ROPE
V_DIM = 64
Q_LORA = 256
KV_LORA = 128
ROPE_THETA = 10000.0
D_FF = 4 * D_MODEL
EPS = 1e-6

LANES = 128
HEAD_PAD = 128
SSM_CHUNK = 32
CHUNK_W = SSM_CHUNK * SSM_GROUP
GROUPS_PER_TILE = LANES // SSM_GROUP
LANE_TILES = SSM_WIDTH // LANES
IN_COLS_PAD = SSM_WIDTH + Q_LORA + KV_LORA + 2 * LANES
FLASH_HEADS = 8
VMEM_LIMIT = 56 * 1024 * 1024

F32 = jnp.float32
BF16 = jnp.bfloat16
HIGHEST = lax.Precision.HIGHEST


def _dot(a, b):
    return jnp.dot(a, b, preferred_element_type=F32)


def _dot_exact(a, b):
    return jnp.dot(a, b, precision=HIGHEST, preferred_element_type=F32)


def _select_cols(a, onehot):
    hi = a.astype(BF16)
    r1 = a - hi.astype(F32)
    mid = r1.astype(BF16)
    lo = (r1 - mid.astype(F32)).astype(BF16)
    return _dot(hi, onehot) + _dot(mid, onehot) + _dot(lo, onehot)


def _dot_nt(a, b):
    return lax.dot_general(a, b, (((1,), (1,)), ((), ())), preferred_element_type=F32)


def _rms(v, gain):
    return v * lax.rsqrt(jnp.mean(v * v, axis=-1, keepdims=True) + EPS) * gain


def _params(*sem, flags=None):
    return pltpu.CompilerParams(dimension_semantics=sem, vmem_limit_bytes=VMEM_LIMIT, flags=flags)


def _layer_spec(l, *shape):
    return pl.BlockSpec((None,) + shape, lambda *_: (l,) + (0,) * len(shape))


def _const_spec(*shape):
    return pl.BlockSpec(shape, lambda *_: (0,) * len(shape))


def _rope_table_kernel(pos_ref, freq_ref, a_ref, b_ref):
    ang = pos_ref[...].astype(F32) * freq_ref[...]
    lane = lax.broadcasted_iota(jnp.int32, ang.shape, 1)
    in_rope = (lane >= QK_NOPE) & (lane < QK_DIM)
    a_ref[...] = jnp.where(lane < QK_NOPE, 1.0, jnp.where(in_rope, jnp.cos(ang), 0.0))
    b_ref[...] = jnp.where(in_rope, jnp.sin(ang), 0.0)


def _rope_tables(positions):
    bsz, seq = positions.shape
    t = bsz * seq
    tm = min(1024, t)
    half = QK_ROPE // 2
    inv_freq = ROPE_THETA ** (-jnp.arange(half, dtype=F32) / half)
    freq = jnp.zeros((1, LANES), F32).at[0, QK_NOPE:QK_DIM].set(jnp.tile(inv_freq, 2))
    a, b = pl.pallas_call(
        _rope_table_kernel,
        grid=(t // tm,),
        in_specs=[pl.BlockSpec((tm, 1), lambda i: (i, 0)), _const_spec(1, LANES)],
        out_specs=[pl.BlockSpec((tm, LANES), lambda i: (i, 0))] * 2,
        out_shape=[jax.ShapeDtypeStruct((t, LANES), F32)] * 2,
        compiler_params=_params("parallel"),
        name="rope_tables",
    )(positions.reshape(t, 1), freq)
    return a.reshape(bsz, seq, LANES), b.reshape(bsz, seq, LANES)


def _inproj_kernel(x_ref, gmix_ref, win_ref, gq_ref, wqa_ref, wqb_ref, gkv_ref, wk_ref, wv_ref,
                   gains_ref, bd_ref, ra_ref, rb_ref, u_ref, q_ref, k_ref, v_ref):
    h = _rms(x_ref[...], gmix_ref[...]).astype(BF16)
    proj = _dot(h, win_ref[...])
    for t in range(LANE_TILES):
        u_ref[t] = proj[:, t * LANES:(t + 1) * LANES]

    two = lambda a: jnp.concatenate([a, a], axis=1)
    ra, rb = ra_ref[...], rb_ref[...]
    gains = gains_ref[...]
    bd = bd_ref[...]
    scale = math.log2(math.e) / math.sqrt(QK_DIM)
    inv_d = 1.0 / QK_DIM

    o1 = SSM_WIDTH
    o2 = o1 + Q_LORA
    o3 = o2 + KV_LORA
    hq = _rms(proj[:, o1:o2], gq_ref[...]).astype(BF16)
    qa = _dot(hq, wqa_ref[...])
    qb = _dot(hq, wqb_ref[...])
    hkv = _rms(proj[:, o2:o3], gkv_ref[...]).astype(BF16)
    kn = _dot(hkv, wk_ref[...])
    vm = _dot(hkv, wv_ref[...])

    kr = proj[:, o3:o3 + LANES]
    kr_sw = proj[:, o3 + LANES:o3 + 2 * LANES]
    kr_ss = two(_dot((kr * kr).astype(BF16), bd[:LANES, :LANES]))
    krot = two(kr * (gains[3:4] * ra) + kr_sw * (gains[4:5] * rb))
    ga, gb, gk = two(ra * gains[0:1]), two(rb * gains[1:2]), two(gains[2:3])
    lane = lax.broadcasted_iota(jnp.int32, ga.shape, 1)
    ones_col = jnp.where((lane & (LANES - 1)) == V_DIM, 1.0, 0.0)

    for hp in range(MLA_HEADS // 2):
        sl = slice(2 * hp * HEAD_PAD, (2 * hp + 2) * HEAD_PAD)
        q2 = qa[:, sl]
        rq = lax.rsqrt(_dot((q2 * q2).astype(BF16), bd) * inv_d + EPS) * scale
        qo = ((q2 * ga + qb[:, sl] * gb) * rq).astype(BF16)
        k2 = kn[:, sl]
        rk = lax.rsqrt((_dot((k2 * k2).astype(BF16), bd) + kr_ss) * inv_d + EPS)
        ko = ((k2 * gk + krot) * rk).astype(BF16)
        vo = (vm[:, sl] + ones_col).astype(BF16)
        for j in range(2):
            q_ref[2 * hp + j] = qo[:, j * HEAD_PAD:(j + 1) * HEAD_PAD]
            k_ref[2 * hp + j] = ko[:, j * HEAD_PAD:(j + 1) * HEAD_PAD]
            v_ref[2 * hp + j] = vo[:, j * HEAD_PAD:(j + 1) * HEAD_PAD]


def _inproj(l, x, w, ra, rb, tm):
    bsz, seq, _ = x.shape
    tok = lambda width: pl.BlockSpec((None, tm, width), lambda b, i: (b, i, 0))
    head = pl.BlockSpec((None, MLA_HEADS, tm, HEAD_PAD), lambda b, i: (b, 0, i, 0))
    hw = MLA_HEADS * HEAD_PAD
    head_shape = jax.ShapeDtypeStruct((bsz, MLA_HEADS, seq, HEAD_PAD), BF16)
    return pl.pallas_call(
        _inproj_kernel,
        grid=(bsz, seq // tm),
        in_specs=[tok(D_MODEL), _layer_spec(l, 1, D_MODEL), _layer_spec(l, D_MODEL, IN_COLS_PAD),
                  _layer_spec(l, 1, Q_LORA), _layer_spec(l, Q_LORA, hw), _layer_spec(l, Q_LORA, hw),
                  _layer_spec(l, 1, KV_LORA), _layer_spec(l, KV_LORA, hw), _layer_spec(l, KV_LORA, hw),
                  _layer_spec(l, 8, HEAD_PAD), _const_spec(2 * HEAD_PAD, 2 * HEAD_PAD),
                  tok(LANES), tok(LANES)],
        out_specs=[pl.BlockSpec((LANE_TILES, None, tm, LANES), lambda b, i: (0, b, i, 0)), head, head, head],
        out_shape=[jax.ShapeDtypeStruct((LANE_TILES, bsz, seq, LANES), F32), head_shape, head_shape, head_shape],
        compiler_params=_params("parallel", "parallel"),
        name="inproj",
    )(x, w["g_mix"], w["w_in"], w["g_q"], w["w_qa"], w["w_qb"], w["g_kv"], w["w_k"], w["w_v"],
      w["mla_gains"], w["bd"], ra, rb)


S5_PREP_GROUPS = 4


def _s5_prep_kernel(*refs):
    per_group_in, shared, outs = refs[:8], refs[8:10], refs[10:]
    for g in range(S5_PREP_GROUPS):
        _s5_prep_group(*[r.at[g] for r in per_group_in], *shared, *[r.at[g] for r in outs])


def _s5_prep_group(pcol_ref, prow_ref, b_re_ref, b_im_ref, bt_re_ref, bt_im_ref,
                   ct_re_ref, ct_im_ref, sel_ref, reps_ref,
                   m_ref, wst_ref, wo_ref, apa_ref, apb_ref):
    def zoh_coeff(a_re, a_im, lr, li):
        xr = a_re - 1.0
        den = lr * lr + li * li
        return (xr * lr + a_im * li) / den, (a_im * lr - xr * li) / den

    def zoh(lr, li, ls):
        step = jnp.exp(ls)
        mag = jnp.exp(lr * step)
        ang = li * step
        return zoh_coeff(mag * jnp.cos(ang), mag * jnp.sin(ang), lr, li) + (step,)

    pc = pcol_ref[...]
    lr, li = pc[:, 0:1], pc[:, 1:2]
    step = jnp.exp(pc[:, 2:3])
    n = lax.broadcasted_iota(jnp.int32, (SSM_STATE, LANES), 1).astype(F32)
    mag = jnp.exp(lr * step * n)
    th = li * step * n
    pw_re, pw_im = mag * jnp.cos(th), mag * jnp.sin(th)
    cfr, cfi = zoh_coeff(pw_re[:, 1:2], pw_im[:, 1:2], lr, li)
    b_re, b_im = b_re_ref[...], b_im_ref[...]
    bb_re = cfr * b_re - cfi * b_im
    bb_im = cfr * b_im + cfi * b_re

    pr = prow_ref[...]
    lr_row, li_row, ls_row = pr[0:1], pr[1:2], pr[2:3]
    cfr_row, cfi_row, step_row = zoh(lr_row, li_row, ls_row)
    bt_re, bt_im = bt_re_ref[...], bt_im_ref[...]
    bbt_re = cfr_row[:, :SSM_STATE] * bt_re - cfi_row[:, :SSM_STATE] * bt_im
    bbt_im = cfr_row[:, :SSM_STATE] * bt_im + cfi_row[:, :SSM_STATE] * bt_re

    sel = sel_ref[...]
    brep_re, brep_im = _select_cols(bb_re, sel), _select_cols(bb_im, sel)
    crep_re, crep_im = _select_cols(ct_re_ref[...], sel), _select_cols(ct_im_ref[...], sel)

    apow = lambda i: (_select_cols(pw_re, reps_ref[i]), _select_cols(pw_im, reps_ref[i]))

    e_re, e_im = apow(0)
    g_re = e_re * crep_re - e_im * crep_im
    g_im = e_re * crep_im + e_im * crep_re
    kt = _dot_exact(bbt_re, g_re) - _dot_exact(bbt_im, g_im)
    klane = lax.broadcasted_iota(jnp.int32, kt.shape, 1)
    m_ref[0:SSM_GROUP, :] = kt.astype(BF16)
    for s in range(1, SSM_CHUNK):
        shifted = jnp.where(klane >= s * SSM_GROUP, pltpu.roll(kt, s * SSM_GROUP, axis=1), 0.0)
        m_ref[s * SSM_GROUP:(s + 1) * SSM_GROUP, :] = shifted.astype(BF16)

    e_re, e_im = apow(1)
    wst_ref[0:SSM_STATE, :] = (e_re * brep_re - e_im * brep_im).astype(BF16)
    wst_ref[SSM_STATE:, :] = (e_re * brep_im + e_im * brep_re).astype(BF16)

    e_re, e_im = apow(2)
    wo_ref[0:SSM_STATE, :] = (e_re * crep_re - e_im * crep_im).astype(BF16)
    wo_ref[SSM_STATE:, :] = (-(e_re * crep_im + e_im * crep_re)).astype(BF16)

    krow = lax.broadcasted_iota(jnp.int32, (8, LANES), 0)
    klan = lax.broadcasted_iota(jnp.int32, (8, LANES), 1)
    nn = (SSM_CHUNK * jnp.left_shift(1, krow)).astype(F32)
    mag = jnp.exp(lr_row * step_row * nn)
    th = li_row * step_row * nn
    p_re, p_im = mag * jnp.cos(th), mag * jnp.sin(th)
    apa_ref[...] = p_re
    apb_ref[...] = jnp.where(klan < SSM_STATE, -p_im, p_im)


def _s5_prep(lam_re, lam_im, log_step, b_re, b_im, c_re, c_im):
    nl, g, p = lam_re.shape
    hh = SSM_GROUP
    ls = jnp.broadcast_to(log_step[..., None], (nl, g, p))
    pcol = jnp.zeros((nl, g, p, 8), F32)
    pcol = pcol.at[..., 0].set(lam_re).at[..., 1].set(lam_im).at[..., 2].set(ls)
    dup = lambda v: jnp.concatenate([v, v], axis=-1)
    prow = jnp.zeros((nl, g, 8, LANES), F32)
    prow = prow.at[:, :, 0].set(dup(lam_re)).at[:, :, 1].set(dup(lam_im)).at[:, :, 2].set(dup(ls))
    sel = jnp.tile(jnp.eye(hh, dtype=BF16), (1, SSM_CHUNK))
    tau = jnp.arange(CHUNK_W) // hh
    nrow = jnp.arange(LANES)[:, None]
    reps = jnp.stack([nrow == tau, nrow == SSM_CHUNK - 1 - tau, nrow == tau + 1]).astype(BF16)
    blk = lambda *s: pl.BlockSpec((None, S5_PREP_GROUPS) + s, lambda l, i: (l, i) + (0,) * len(s))
    return pl.pallas_call(
        _s5_prep_kernel,
        grid=(nl, g // S5_PREP_GROUPS),
        in_specs=[blk(p, 8), blk(8, LANES), blk(p, hh), blk(p, hh), blk(hh, p), blk(hh, p),
                  blk(p, hh), blk(p, hh), _const_spec(hh, CHUNK_W), _const_spec(3, LANES, CHUNK_W)],
        out_specs=[blk(CHUNK_W, CHUNK_W), blk(2 * p, CHUNK_W), blk(2 * p, CHUNK_W),
                   blk(8, LANES), blk(8, LANES)],
        out_shape=[jax.ShapeDtypeStruct((nl, g, CHUNK_W, CHUNK_W), BF16),
                   jax.ShapeDtypeStruct((nl, g, 2 * p, CHUNK_W), BF16),
                   jax.ShapeDtypeStruct((nl, g, 2 * p, CHUNK_W), BF16),
                   jax.ShapeDtypeStruct((nl, g, 8, LANES), F32),
                   jax.ShapeDtypeStruct((nl, g, 8, LANES), F32)],
        compiler_params=_params("parallel", "parallel"),
        name="s5_prep",
    )(pcol, prow, b_re, b_im, jnp.swapaxes(b_re, -1, -2), jnp.swapaxes(b_im, -1, -2),
      jnp.swapaxes(c_re, -1, -2), jnp.swapaxes(c_im, -1, -2), sel, reps)


def _s5_mix_kernel(u_ref, perm_ref, m_ref, wst_ref, wo_ref, apa_ref, apb_ref, dx_ref, y_ref,
                   xs_scr, ys_scr, *, nsteps):
    nb, seq, _ = u_ref.shape
    cps = seq // SSM_CHUNK
    nc = nb * cps
    gpt = GROUPS_PER_TILE
    nslab = SSM_CHUNK // gpt
    perm = perm_ref[...]

    def slab(j):
        return jnp.concatenate([
            jnp.concatenate([u_ref[b, pl.ds(gpt * j + s, cps, stride=SSM_CHUNK), :] for s in range(gpt)], axis=1)
            for b in range(nb)], axis=0)

    z = _dot(jnp.concatenate([slab(j) for j in range(nslab)], axis=0).astype(BF16), perm).astype(BF16)
    for j in range(nslab):
        for g in range(gpt):
            xs_scr[g, :, j * LANES:(j + 1) * LANES] = z[j * nc:(j + 1) * nc, g * LANES:(g + 1) * LANES]

    c = lax.broadcasted_iota(jnp.int32, (nc, 2 * SSM_STATE), 0) & (cps - 1)
    for g in range(gpt):
        x = xs_scr[g]
        y = _dot(x, m_ref[g])
        st = _dot_nt(x, wst_ref[g])
        apa, apb = apa_ref[g], apb_ref[g]
        for k in range(nsteps):
            d = 1 << k
            sh = jnp.where(c >= d, pltpu.roll(st, d, axis=0), 0.0)
            st = st + sh * apa[k:k + 1] + pltpu.roll(sh, SSM_STATE, axis=1) * apb[k:k + 1]
        carried = jnp.where(c >= 1, pltpu.roll(st, 1, axis=0), 0.0)
        y = y + _dot(carried.astype(BF16), wo_ref[g]) + dx_ref[g] * x.astype(F32)
        ys_scr[g] = jax.nn.gelu(y).astype(BF16)

    w = jnp.concatenate([
        jnp.concatenate([ys_scr[g, :, j * LANES:(j + 1) * LANES] for g in range(gpt)], axis=1)
        for j in range(nslab)], axis=0)
    o = _dot(w, perm)
    for j in range(nslab):
        for t in range(gpt):
            for b in range(nb):
                y_ref[b, pl.ds(gpt * j + t, cps, stride=SSM_CHUNK), :] = (
                    o[j * nc + b * cps:j * nc + (b + 1) * cps, t * LANES:(t + 1) * LANES])


def _s5_mix(l, u, prep, dx, perm):
    tiles, bsz, seq, _ = u.shape
    cps = seq // SSM_CHUNK
    assert cps & (cps - 1) == 0 and cps <= 256
    gpt = GROUPS_PER_TILE
    nb = 2 if bsz % 2 == 0 else 1
    m, wst, wo, apa, apb = prep
    tokens = pl.BlockSpec((None, nb, seq, LANES), lambda t, b: (t, b, 0, 0))
    grp = lambda *s: pl.BlockSpec((None, gpt) + s, lambda t, b: (l, t) + (0,) * len(s))
    return pl.pallas_call(
        functools.partial(_s5_mix_kernel, nsteps=cps.bit_length() - 1),
        grid=(tiles, bsz // nb),
        in_specs=[tokens, _const_spec(gpt * LANES, gpt * LANES), grp(CHUNK_W, CHUNK_W),
                  grp(2 * SSM_STATE, CHUNK_W), grp(2 * SSM_STATE, CHUNK_W), grp(8, LANES), grp(8, LANES),
                  grp(1, CHUNK_W)],
        out_specs=tokens,
        out_shape=jax.ShapeDtypeStruct(u.shape, F32),
        scratch_shapes=[pltpu.VMEM((gpt, nb * cps, CHUNK_W), BF16), pltpu.VMEM((gpt, nb * cps, CHUNK_W), BF16)],
        compiler_params=_params("parallel", "parallel"),
        name="s5_mix",
    )(u, perm, m, wst, wo, apa, apb, dx)


FULL, DIAG_HALF, DIAG_FULL = 0, 1, 2


def _flash_kernel(qt_ref, kt_ref, kind_ref, q_ref, k_ref, v_ref, o_ref, m_scr, acc_scr, *, heads):
    p_idx = pl.program_id(2)
    ki = kt_ref[p_idx]
    kind = kind_ref[p_idx]
    tq, tk = q_ref.shape[2], k_ref.shape[2]

    @pl.when(ki == 0)
    def _():
        m_scr[...] = jnp.full(m_scr.shape, -jnp.inf, F32)
        acc_scr[...] = jnp.zeros(acc_scr.shape, F32)

    def attend(j, nk, diag_offset):
        s = _dot_nt(q_ref[0, j], k_ref[0, j, 0:nk, :])
        if diag_offset is not None:
            row = lax.broadcasted_iota(jnp.int32, s.shape, 0)
            col = lax.broadcasted_iota(jnp.int32, s.shape, 1)
            s = jnp.where(col <= row + diag_offset, s, -jnp.inf)
        m_prev = m_scr[j]
        m_new = jnp.maximum(m_prev, jnp.max(s, axis=-1, keepdims=True))
        alpha = jnp.exp2(m_prev - m_new)
        p = jnp.exp2(s - jnp.tile(m_new, (1, nk // LANES)))
        acc_scr[j] = alpha * acc_scr[j] + _dot(p.astype(BF16), v_ref[0, j, 0:nk, :])
        m_scr[j] = m_new

    @pl.when(kind == FULL)
    def _():
        for j in range(heads):
            attend(j, tk, None)

    @pl.when(kind == DIAG_HALF)
    def _():
        for j in range(heads):
            attend(j, tq, 0)

    @pl.when(kind == DIAG_FULL)
    def _():
        for j in range(heads):
            attend(j, tk, tq)

    @pl.when(kind != FULL)
    def _():
        outs = []
        for j in range(heads):
            acc = acc_scr[j]
            outs.append(acc[:, :V_DIM] / acc[:, V_DIM:V_DIM + 1])
        o_ref[0] = jnp.concatenate(outs, axis=-1).astype(BF16)


def _flash(q, k, v, tq, heads_per_step):
    bsz, heads, seq, _ = q.shape
    hp = heads_per_step
    tk = 2 * tq
    assert seq % tk == 0
    steps = []
    for i in range(seq // tq):
        steps += [(i, j, FULL) for j in range(i // 2)] + [(i, i // 2, DIAG_HALF if i % 2 == 0 else DIAG_FULL)]
    qt, kt, kind = (jnp.asarray(col, jnp.int32) for col in zip(*steps))
    qspec = pl.BlockSpec((1, hp, tq, HEAD_PAD), lambda b, h, p, qt, kt, kind: (b, h, qt[p], 0))
    kspec = pl.BlockSpec((1, hp, tk, HEAD_PAD), lambda b, h, p, qt, kt, kind: (b, h, kt[p], 0))
    grid_spec = pltpu.PrefetchScalarGridSpec(
        num_scalar_prefetch=3,
        grid=(bsz, heads // hp, len(steps)),
        in_specs=[qspec, kspec, kspec],
        out_specs=pl.BlockSpec((1, tq, hp * V_DIM), lambda b, h, p, qt, kt, kind: (b, qt[p], h)),
        scratch_shapes=[pltpu.VMEM((hp, tq, LANES), F32), pltpu.VMEM((hp, tq, HEAD_PAD), F32)])
    return pl.pallas_call(
        functools.partial(_flash_kernel, heads=hp),
        grid_spec=grid_spec,
        out_shape=jax.ShapeDtypeStruct((bsz, seq, heads * V_DIM), BF16),
        compiler_params=_params("parallel", "parallel", "arbitrary"),
        name="flash",
    )(qt, kt, kind, q, k, v)


def _head_sums(v2, lane, width):
    out = jnp.zeros_like(v2)
    for hd in range(v2.shape[-1] // width):
        msk = (lane >= hd * width) & (lane < (hd + 1) * width)
        out = out + jnp.where(msk, jnp.sum(jnp.where(msk, v2, 0.0), axis=-1, keepdims=True), 0.0)
    return out


def _mem_prep_kernel(mem_ref, g_ref, w_ref, kg_ref, kt_ref, vm_ref):
    hm = _rms(mem_ref[0], g_ref[0]).astype(BF16)
    kv = _dot(hm, w_ref[0])
    k, v = kv[:, :MEM_WIDTH], kv[:, MEM_WIDTH:]
    lane = lax.broadcasted_iota(jnp.int32, k.shape, 1)
    ss = _head_sums(k * k, lane, MEM_HEAD_DIM)
    kn = k * lax.rsqrt(ss * (1.0 / MEM_HEAD_DIM) + EPS) * kg_ref[0] * (1.0 / math.sqrt(MEM_HEAD_DIM))
    knt = kn.T
    row = lax.broadcasted_iota(jnp.int32, knt.shape, 0)
    for hd in range(MEM_HEADS):
        lo, hi = hd * MEM_HEAD_DIM, (hd + 1) * MEM_HEAD_DIM
        kt_ref[0, 0, hd] = jnp.where((row >= lo) & (row < hi), knt, 0.0).astype(BF16)
        vm_ref[0, 0, hd] = jnp.where((lane >= lo) & (lane < hi), v, 0.0).astype(BF16)


def _mem_prep(mem, g, w_kv, k_gain):
    nl = g.shape[0]
    bsz = mem.shape[0]
    out = jax.ShapeDtypeStruct((nl, bsz, MEM_HEADS, N_MEM, MEM_WIDTH), BF16)
    return pl.pallas_call(
        _mem_prep_kernel,
        grid=(nl, bsz),
        in_specs=[pl.BlockSpec((1, N_MEM, D_MODEL), lambda l, b: (b, 0, 0)),
                  pl.BlockSpec((1, 1, D_MODEL), lambda l, b: (l, 0, 0)),
                  pl.BlockSpec((1, D_MODEL, 2 * MEM_WIDTH), lambda l, b: (l, 0, 0)),
                  pl.BlockSpec((1, 1, MEM_WIDTH), lambda l, b: (l, 0, 0))],
        out_specs=[pl.BlockSpec((1, 1, MEM_HEADS, MEM_WIDTH, N_MEM), lambda l, b: (l, b, 0, 0, 0)),
                   pl.BlockSpec((1, 1, MEM_HEADS, N_MEM, MEM_WIDTH), lambda l, b: (l, b, 0, 0, 0))],
        out_shape=[out, out],
        compiler_params=_params("parallel", "parallel"),
        name="mem_prep",
    )(mem, g, w_kv, k_gain)


def _outproj_kernel(x_ref, ys_ref, om_ref, wglu_ref, bglu_ref, gs_ref, gm_ref, wout_ref,
                    gmq_ref, wmq_ref, qg_ref, kt_ref, vm_ref, wmo_ref, gmlp_ref,
                    x2_ref, h3_ref):
    ys = jnp.concatenate([ys_ref[t] for t in range(LANE_TILES)], axis=1)
    yg = ys * jax.nn.sigmoid(_dot(ys.astype(BF16), wglu_ref[...]) + bglu_ref[...])
    n1 = _rms(yg, gs_ref[...]).astype(BF16)
    n2 = _rms(om_ref[...].astype(F32), gm_ref[...]).astype(BF16)
    x1 = x_ref[...] + _dot(n1, wout_ref[:SSM_WIDTH, :]) + _dot(n2, wout_ref[SSM_WIDTH:, :])

    q = _dot(_rms(x1, gmq_ref[...]).astype(BF16), wmq_ref[...])
    lane = lax.broadcasted_iota(jnp.int32, q.shape, 1)
    ss = _head_sums(q * q, lane, MEM_HEAD_DIM)
    qn = (q * lax.rsqrt(ss * (1.0 / MEM_HEAD_DIM) + EPS) * qg_ref[...]).astype(BF16)
    o = jnp.zeros(q.shape, F32)
    for hd in range(MEM_HEADS):
        s = _dot(qn, kt_ref[hd])
        p = jnp.exp(s - jnp.max(s, axis=-1, keepdims=True))
        inv = 1.0 / jnp.sum(p, axis=-1, keepdims=True)
        o = o + _dot(p.astype(BF16), vm_ref[hd]) * inv
    x2 = x1 + _dot(o.astype(BF16), wmo_ref[...])
    x2_ref[...] = x2
    h3_ref[...] = _rms(x2, gmlp_ref[...]).astype(BF16)


def _outproj(l, x, ys, om, w, kt, vm, tm):
    bsz, seq, _ = x.shape
    tok = lambda width: pl.BlockSpec((None, tm, width), lambda b, i: (b, i, 0))
    memb = pl.BlockSpec((None, None, MEM_HEADS, N_MEM, MEM_WIDTH), lambda b, i: (l, b, 0, 0, 0))
    return pl.pallas_call(
        _outproj_kernel,
        grid=(bsz, seq // tm),
        in_specs=[tok(D_MODEL), pl.BlockSpec((LANE_TILES, None, tm, LANES), lambda b, i: (0, b, i, 0)),
                  tok(MLA_WIDTH),
                  _layer_spec(l, SSM_WIDTH, SSM_WIDTH), _layer_spec(l, 1, SSM_WIDTH),
                  _layer_spec(l, 1, SSM_WIDTH), _layer_spec(l, 1, MLA_WIDTH),
                  _layer_spec(l, D_MODEL, D_MODEL), _layer_spec(l, 1, D_MODEL),
                  _layer_spec(l, D_MODEL, MEM_WIDTH), _layer_spec(l, 1, MEM_WIDTH),
                  memb, memb, _layer_spec(l, MEM_WIDTH, D_MODEL), _layer_spec(l, 1, D_MODEL)],
        out_specs=[tok(D_MODEL), tok(D_MODEL)],
        out_shape=[jax.ShapeDtypeStruct((bsz, seq, D_MODEL), F32),
                   jax.ShapeDtypeStruct((bsz, seq, D_MODEL), BF16)],
        compiler_params=_params("parallel", "parallel"),
        name="outproj",
    )(x, ys, om, w["w_glu"], w["b_glu"], w["g_ssm"], w["g_mla"], w["w_out"],
      w["g_memq"], w["w_mq"], w["mem_q_gain"], kt, vm, w["w_mo"], w["g_mlp"])


def _mlp_kernel(x_ref, h_ref, w1_ref, w2_ref, o_ref, *, ff_tile):
    h = h_ref[...]
    acc = x_ref[...]
    for c in range(D_FF // ff_tile):
        a = jnp.maximum(_dot(h, w1_ref[:, c * ff_tile:(c + 1) * ff_tile]), 0.0)
        acc = acc + _dot((a * a).astype(BF16), w2_ref[c * ff_tile:(c + 1) * ff_tile, :])
    o_ref[...] = acc


def _mlp(l, x, h, w1, w2, tm):
    t = x.shape[0]
    tok = pl.BlockSpec((tm, D_MODEL), lambda i: (i, 0))
    return pl.pallas_call(
        functools.partial(_mlp_kernel, ff_tile=1024),
        grid=(t // tm,),
        in_specs=[tok, tok, _layer_spec(l, D_MODEL, D_FF), _layer_spec(l, D_FF, D_MODEL)],
        out_specs=tok,
        out_shape=jax.ShapeDtypeStruct((t, D_MODEL), F32),
        compiler_params=_params("parallel"),
        name="mlp",
    )(x, h, w1, w2)


def _half_swap(w):
    h = w.shape[-1] // 2
    return jnp.concatenate([-w[..., h:], w[..., :h]], axis=-1)


def _half_swap_unsigned(w):
    h = w.shape[-1] // 2
    return jnp.concatenate([w[..., h:], w[..., :h]], axis=-1)


def _pad_last(w, before, after):
    pads = [(0, 0)] * (w.ndim - 1) + [(before, after)]
    return jnp.pad(w, pads)


def _trunk(tm, tq, x, mem, positions, norm_mix, w_in, ssm_lambda_re, ssm_lambda_im, ssm_log_step, ssm_b_re, ssm_b_im, ssm_c_re, ssm_c_im, ssm_d, ssm_w_glu, ssm_b_glu, mla_q_norm, mla_w_uq, mla_kv_norm, mla_w_ukv, mla_q_gain, mla_k_gain, out_norm_ssm, out_norm_mla, w_out, norm_mem_q, norm_mem_kv, mem_w_q, mem_w_kv, mem_q_gain, mem_k_gain, mem_w_o, norm_mlp, mlp_w1, mlp_w2):
    bsz, seq, _ = x.shape
    depth = norm_mix.shape[0]
    row = lambda a: a[:, None, :]
    tail = HEAD_PAD - QK_DIM

    s3 = SSM_WIDTH + Q_LORA + KV_LORA
    k_rope_w = w_in[..., s3:]
    w_in_x = jnp.concatenate([w_in[..., :s3], _pad_last(k_rope_w, QK_NOPE, tail),
                              _pad_last(_half_swap(k_rope_w), QK_NOPE, tail)], axis=-1).astype(BF16)
    wq = mla_w_uq.reshape(depth, Q_LORA, MLA_HEADS, QK_DIM)
    hw = MLA_HEADS * HEAD_PAD
    w_qa = _pad_last(wq, 0, tail).reshape(depth, Q_LORA, hw).astype(BF16)
    w_qb = _pad_last(_half_swap(wq[..., QK_NOPE:]), QK_NOPE, tail).reshape(depth, Q_LORA, hw).astype(BF16)
    wkv = mla_w_ukv.reshape(depth, KV_LORA, MLA_HEADS, QK_NOPE + V_DIM)
    w_k = _pad_last(wkv[..., :QK_NOPE], 0, HEAD_PAD - QK_NOPE).reshape(depth, KV_LORA, hw).astype(BF16)
    w_v = _pad_last(wkv[..., QK_NOPE:], 0, HEAD_PAD - V_DIM).reshape(depth, KV_LORA, hw).astype(BF16)
    q_rope_g, k_rope_g = mla_q_gain[:, QK_NOPE:], mla_k_gain[:, QK_NOPE:]
    mla_gains = jnp.stack([
        _pad_last(mla_q_gain, 0, tail),
        _pad_last(_half_swap_unsigned(q_rope_g), QK_NOPE, tail),
        _pad_last(mla_k_gain[:, :QK_NOPE], 0, HEAD_PAD - QK_NOPE),
        _pad_last(k_rope_g, QK_NOPE, tail),
        _pad_last(_half_swap_unsigned(k_rope_g), QK_NOPE, tail)], axis=1)
    mla_gains = jnp.pad(mla_gains, ((0, 0), (0, 3), (0, 0)))
    blk = jnp.arange(2 * HEAD_PAD) // HEAD_PAD
    bd = (blk[:, None] == blk[None, :]).astype(BF16)
    wmkv = mem_w_kv.reshape(depth, D_MODEL, MEM_HEADS, 2, MEM_HEAD_DIM)
    w_mkv = wmkv.transpose(0, 1, 3, 2, 4).reshape(depth, D_MODEL, 2 * MEM_WIDTH).astype(BF16)
    idx = jnp.arange(GROUPS_PER_TILE * LANES)
    dest = ((idx // SSM_GROUP) % GROUPS_PER_TILE) * LANES + (idx // LANES) * SSM_GROUP + idx % SSM_GROUP
    perm = (dest[:, None] == idx[None, :]).astype(BF16)

    w = dict(g_mix=row(norm_mix), w_in=w_in_x, g_q=row(mla_q_norm), w_qa=w_qa, w_qb=w_qb,
             g_kv=row(mla_kv_norm), w_k=w_k, w_v=w_v, mla_gains=mla_gains, bd=bd,
             w_glu=ssm_w_glu.astype(BF16), b_glu=row(ssm_b_glu), g_ssm=row(out_norm_ssm),
             g_mla=row(out_norm_mla), w_out=w_out.astype(BF16), g_memq=row(norm_mem_q),
             w_mq=mem_w_q.astype(BF16), mem_q_gain=row(jnp.tile(mem_q_gain, (1, MEM_HEADS))),
             w_mo=mem_w_o.astype(BF16), g_mlp=row(norm_mlp))
    w1_b = mlp_w1.astype(BF16)
    w2_b = mlp_w2.astype(BF16)

    ra, rb = _rope_tables(positions)
    prep = _s5_prep(ssm_lambda_re, ssm_lambda_im, ssm_log_step, ssm_b_re, ssm_b_im, ssm_c_re, ssm_c_im)
    kt_all, vm_all = _mem_prep(mem, row(norm_mem_kv), w_mkv, row(jnp.tile(mem_k_gain, (1, MEM_HEADS))))
    dx = jnp.tile(ssm_d.reshape(depth, SSM_GROUPS, 1, SSM_GROUP), (1, 1, 1, SSM_CHUNK))

    for l in range(depth):
        u, q, k, v = _inproj(l, x, w, ra, rb, min(2 * tm, seq))
        ys = _s5_mix(l, u, prep, dx, perm)
        om = _flash(q, k, v, tq, FLASH_HEADS)
        x2, h3 = _outproj(l, x, ys, om, w, kt_all, vm_all, min(2 * tm, seq))
        x = _mlp(l, x2.reshape(bsz * seq, D_MODEL), h3.reshape(bsz * seq, D_MODEL),
                 w1_b, w2_b, tm).reshape(bsz, seq, D_MODEL)
    return x


def kernel(x, mem, positions, norm_mix, w_in, ssm_lambda_re, ssm_lambda_im, ssm_log_step, ssm_b_re, ssm_b_im, ssm_c_re, ssm_c_im, ssm_d, ssm_w_glu, ssm_b_glu, mla_q_norm, mla_w_uq, mla_kv_norm, mla_w_ukv, mla_q_gain, mla_k_gain, out_norm_ssm, out_norm_mla, w_out, norm_mem_q, norm_mem_kv, mem_w_q, mem_w_kv, mem_q_gain, mem_k_gain, mem_w_o, norm_mlp, mlp_w1, mlp_w2):
    seq = x.shape[1]
    return _trunk(min(512, seq), min(512, seq // 2), x, mem, positions, norm_mix, w_in, ssm_lambda_re, ssm_lambda_im, ssm_log_step, ssm_b_re, ssm_b_im, ssm_c_re, ssm_c_im, ssm_d, ssm_w_glu, ssm_b_glu, mla_q_norm, mla_w_uq, mla_kv_norm, mla_w_ukv, mla_q_gain, mla_k_gain, out_norm_ssm, out_norm_mla, w_out, norm_mem_q, norm_mem_kv, mem_w_q, mem_w_kv, mem_q_gain, mem_k_gain, mem_w_o, norm_mlp, mlp_w1, mlp_w2)
```

```python
import functools
import math

import jax
import jax.numpy as jnp
from jax import lax
from jax.experimental import pallas as pl
from jax.experimental.pallas import tpu as pltpu

D_MODEL = 1024
N_MEM = 256
MEM_HEADS = 4
MEM_HEAD_DIM = 64
MEM_WIDTH = MEM_HEADS * MEM_HEAD_DIM
SSM_WIDTH = 512
MLA_WIDTH = 512
SSM_GROUP = 16
SSM_GROUPS = 32
SSM_STATE = 64
MLA_HEADS = 8
QK_NOPE = 64
QK_ROPE = 32
QK_DIM = QK_NOPE + QK_ROPE
V_DIM = 64
Q_LORA = 256
KV_LORA = 128
ROPE_THETA = 10000.0
D_FF = 4 * D_MODEL
EPS = 1e-6

LANES = 128
HEAD_PAD = 128
SSM_CHUNK = 32
CHUNK_W = SSM_CHUNK * SSM_GROUP
GROUPS_PER_TILE = LANES // SSM_GROUP
LANE_TILES = SSM_WIDTH // LANES
IN_COLS_PAD = SSM_WIDTH + Q_LORA + KV_LORA + 2 * LANES
FLASH_KEY_RATIO = 4
FLASH_HEADS = 8
VMEM_LIMIT = 56 * 1024 * 1024

F32 = jnp.float32
BF16 = jnp.bfloat16
HIGHEST = lax.Precision.HIGHEST


def _dot(a, b):
    return jnp.dot(a, b, preferred_element_type=F32)


def _dot_exact(a, b):
    return jnp.dot(a, b, precision=HIGHEST, preferred_element_type=F32)


def _select_cols(a, onehot):
    hi = a.astype(BF16)
    r1 = a - hi.astype(F32)
    mid = r1.astype(BF16)
    lo = (r1 - mid.astype(F32)).astype(BF16)
    return _dot(hi, onehot) + _dot(mid, onehot) + _dot(lo, onehot)


def _dot_nt(a, b):
    return lax.dot_general(a, b, (((1,), (1,)), ((), ())), preferred_element_type=F32)


def _rms(v, gain):
    return v * lax.rsqrt(jnp.mean(v * v, axis=-1, keepdims=True) + EPS) * gain


def _params(*sem, flags=None):
    return pltpu.CompilerParams(dimension_semantics=sem, vmem_limit_bytes=VMEM_LIMIT, flags=flags)


def _layer_spec(l, *shape):
    return pl.BlockSpec((None,) + shape, lambda *_: (l,) + (0,) * len(shape))


def _const_spec(*shape):
    return pl.BlockSpec(shape, lambda *_: (0,) * len(shape))


def _rope_table_kernel(pos_ref, freq_ref, a_ref, b_ref):
    ang = pos_ref[...].astype(F32) * freq_ref[...]
    lane = lax.broadcasted_iota(jnp.int32, ang.shape, 1)
    in_rope = (lane >= QK_NOPE) & (lane < QK_DIM)
    a_ref[...] = jnp.where(lane < QK_NOPE, 1.0, jnp.where(in_rope, jnp.cos(ang), 0.0))
    b_ref[...] = jnp.where(in_rope, jnp.sin(ang), 0.0)


def _rope_tables(positions):
    bsz, seq = positions.shape
    t = bsz * seq
    tm = min(1024, t)
    half = QK_ROPE // 2
    inv_freq = ROPE_THETA ** (-jnp.arange(half, dtype=F32) / half)
    freq = jnp.pad(jnp.tile(inv_freq, 2), (QK_NOPE, LANES - QK_DIM))[None, :]
    a, b = pl.pallas_call(
        _rope_table_kernel,
        grid=(t // tm,),
        in_specs=[pl.BlockSpec((tm, 1), lambda i: (i, 0)), _const_spec(1, LANES)],
        out_specs=[pl.BlockSpec((tm, LANES), lambda i: (i, 0))] * 2,
        out_shape=[jax.ShapeDtypeStruct((t, LANES), F32)] * 2,
        compiler_params=_params("parallel"),
        name="rope_tables",
    )(positions.reshape(t, 1), freq)
    return a.reshape(bsz, seq, LANES), b.reshape(bsz, seq, LANES)


def _inproj_kernel(x_ref, gmix_ref, win_ref, gq_ref, wqa_ref, wqb_ref, gkv_ref, wk_ref, wv_ref,
                   gains_ref, bd_ref, ra_ref, rb_ref, u_ref, q_ref, k_ref, v_ref):
    h = _rms(x_ref[...], gmix_ref[...]).astype(BF16)
    proj = _dot(h, win_ref[...])
    for t in range(LANE_TILES):
        u_ref[t] = proj[:, t * LANES:(t + 1) * LANES]

    two = lambda a: jnp.concatenate([a, a], axis=1)
    ra, rb = ra_ref[...], rb_ref[...]
    gains = gains_ref[...]
    bd = bd_ref[...]
    scale = math.log2(math.e) / math.sqrt(QK_DIM)
    inv_d = 1.0 / QK_DIM

    o1 = SSM_WIDTH
    o2 = o1 + Q_LORA
    o3 = o2 + KV_LORA
    hq = _rms(proj[:, o1:o2], gq_ref[...]).astype(BF16)
    qa = _dot(hq, wqa_ref[...])
    qb = _dot(hq, wqb_ref[...])
    hkv = _rms(proj[:, o2:o3], gkv_ref[...]).astype(BF16)
    kn = _dot(hkv, wk_ref[...])
    vm = _dot(hkv, wv_ref[...])

    kr = proj[:, o3:o3 + LANES]
    kr_sw = proj[:, o3 + LANES:o3 + 2 * LANES]
    kr_ss = two(_dot((kr * kr).astype(BF16), bd[:LANES, :LANES]))
    krot = two(kr * (gains[3:4] * ra) + kr_sw * (gains[4:5] * rb))
    ga, gb, gk = two(ra * gains[0:1]), two(rb * gains[1:2]), two(gains[2:3])
    lane = lax.broadcasted_iota(jnp.int32, ga.shape, 1)
    ones_col = jnp.where((lane & (LANES - 1)) == V_DIM, 1.0, 0.0)

    for hp in range(MLA_HEADS // 2):
        sl = slice(2 * hp * HEAD_PAD, (2 * hp + 2) * HEAD_PAD)
        q2 = qa[:, sl]
        rq = lax.rsqrt(_dot((q2 * q2).astype(BF16), bd) * inv_d + EPS) * scale
        qo = ((q2 * ga + qb[:, sl] * gb) * rq).astype(BF16)
        k2 = kn[:, sl]
        rk = lax.rsqrt((_dot((k2 * k2).astype(BF16), bd) + kr_ss) * inv_d + EPS)
        ko = ((k2 * gk + krot) * rk).astype(BF16)
        vo = (vm[:, sl] + ones_col).astype(BF16)
        for j in range(2):
            q_ref[2 * hp + j] = qo[:, j * HEAD_PAD:(j + 1) * HEAD_PAD]
            k_ref[2 * hp + j] = ko[:, j * HEAD_PAD:(j + 1) * HEAD_PAD]
            v_ref[2 * hp + j] = vo[:, j * HEAD_PAD:(j + 1) * HEAD_PAD]


def _inproj(l, x, w, ra, rb, tm):
    bsz, seq, _ = x.shape
    tok = lambda width: pl.BlockSpec((None, tm, width), lambda b, i: (b, i, 0))
    head = pl.BlockSpec((None, MLA_HEADS, tm, HEAD_PAD), lambda b, i: (b, 0, i, 0))
    hw = MLA_HEADS * HEAD_PAD
    head_shape = jax.ShapeDtypeStruct((bsz, MLA_HEADS, seq, HEAD_PAD), BF16)
    return pl.pallas_call(
        _inproj_kernel,
        grid=(bsz, seq // tm),
        in_specs=[tok(D_MODEL), _layer_spec(l, 1, D_MODEL), _layer_spec(l, D_MODEL, IN_COLS_PAD),
                  _layer_spec(l, 1, Q_LORA), _layer_spec(l, Q_LORA, hw), _layer_spec(l, Q_LORA, hw),
                  _layer_spec(l, 1, KV_LORA), _layer_spec(l, KV_LORA, hw), _layer_spec(l, KV_LORA, hw),
                  _layer_spec(l, 8, HEAD_PAD), _const_spec(2 * HEAD_PAD, 2 * HEAD_PAD),
                  tok(LANES), tok(LANES)],
        out_specs=[pl.BlockSpec((LANE_TILES, None, tm, LANES), lambda b, i: (0, b, i, 0)), head, head, head],
        out_shape=[jax.ShapeDtypeStruct((LANE_TILES, bsz, seq, LANES), F32), head_shape, head_shape, head_shape],
        compiler_params=_params("parallel", "parallel"),
        name="inproj",
    )(x, w["g_mix"], w["w_in"], w["g_q"], w["w_qa"], w["w_qb"], w["g_kv"], w["w_k"], w["w_v"],
      w["mla_gains"], w["bd"], ra, rb)


S5_PREP_GROUPS = 8


def _s5_prep_kernel(*refs):
    per_group_in, shared, outs = refs[:8], refs[8:10], refs[10:]
    for g in range(S5_PREP_GROUPS):
        _s5_prep_group(*[r.at[g] for r in per_group_in], *shared, *[r.at[g] for r in outs])


def _s5_prep_group(pcol_ref, prow_ref, b_re_ref, b_im_ref, bt_re_ref, bt_im_ref,
                   ct_re_ref, ct_im_ref, sel_ref, reps_ref,
                   m_ref, wst_ref, wo_ref, apa_ref, apb_ref):
    def zoh_coeff(a_re, a_im, lr, li):
        xr = a_re - 1.0
        den = lr * lr + li * li
        return (xr * lr + a_im * li) / den, (a_im * lr - xr * li) / den

    def zoh(lr, li, ls):
        step = jnp.exp(ls)
        mag = jnp.exp(lr * step)
        ang = li * step
        return zoh_coeff(mag * jnp.cos(ang), mag * jnp.sin(ang), lr, li) + (step,)

    pc = pcol_ref[...]
    lr, li = pc[:, 0:1], pc[:, 1:2]
    step = jnp.exp(pc[:, 2:3])
    n = lax.broadcasted_iota(jnp.int32, (SSM_STATE, LANES), 1).astype(F32)
    mag = jnp.exp(lr * step * n)
    th = li * step * n
    pw_re, pw_im = mag * jnp.cos(th), mag * jnp.sin(th)
    cfr, cfi = zoh_coeff(pw_re[:, 1:2], pw_im[:, 1:2], lr, li)
    b_re, b_im = b_re_ref[...], b_im_ref[...]
    bb_re = cfr * b_re - cfi * b_im
    bb_im = cfr * b_im + cfi * b_re

    pr = prow_ref[...]
    lr_row, li_row, ls_row = pr[0:1], pr[1:2], pr[2:3]
    cfr_row, cfi_row, step_row = zoh(lr_row, li_row, ls_row)
    bt_re, bt_im = bt_re_ref[...], bt_im_ref[...]
    bbt_re = cfr_row[:, :SSM_STATE] * bt_re - cfi_row[:, :SSM_STATE] * bt_im
    bbt_im = cfr_row[:, :SSM_STATE] * bt_im + cfi_row[:, :SSM_STATE] * bt_re

    sel = sel_ref[...]
    brep_re, brep_im = _select_cols(bb_re, sel), _select_cols(bb_im, sel)
    crep_re, crep_im = _select_cols(ct_re_ref[...], sel), _select_cols(ct_im_ref[...], sel)

    apow = lambda i: (_select_cols(pw_re, reps_ref[i]), _select_cols(pw_im, reps_ref[i]))

    e_re, e_im = apow(0)
    g_re = e_re * crep_re - e_im * crep_im
    g_im = e_re * crep_im + e_im * crep_re
    kt = _dot_exact(bbt_re, g_re) - _dot_exact(bbt_im, g_im)
    klane = lax.broadcasted_iota(jnp.int32, kt.shape, 1)
    m_ref[0:SSM_GROUP, :] = kt.astype(BF16)
    for s in range(1, SSM_CHUNK):
        shifted = jnp.where(klane >= s * SSM_GROUP, pltpu.roll(kt, s * SSM_GROUP, axis=1), 0.0)
        m_ref[s * SSM_GROUP:(s + 1) * SSM_GROUP, :] = shifted.astype(BF16)

    e_re, e_im = apow(1)
    wst_ref[0:SSM_STATE, :] = (e_re * brep_re - e_im * brep_im).astype(BF16)
    wst_ref[SSM_STATE:, :] = (e_re * brep_im + e_im * brep_re).astype(BF16)

    e_re, e_im = apow(2)
    wo_ref[0:SSM_STATE, :] = (e_re * crep_re - e_im * crep_im).astype(BF16)
    wo_ref[SSM_STATE:, :] = (-(e_re * crep_im + e_im * crep_re)).astype(BF16)

    krow = lax.broadcasted_iota(jnp.int32, (8, LANES), 0)
    klan = lax.broadcasted_iota(jnp.int32, (8, LANES), 1)
    nn = (SSM_CHUNK * jnp.left_shift(1, krow)).astype(F32)
    mag = jnp.exp(lr_row * step_row * nn)
    th = li_row * step_row * nn
    p_re, p_im = mag * jnp.cos(th), mag * jnp.sin(th)
    apa_ref[...] = p_re
    apb_ref[...] = jnp.where(klan < SSM_STATE, -p_im, p_im)


def _s5_prep(lam_re, lam_im, log_step, b_re, b_im, c_re, c_im):
    nl, g, p = lam_re.shape
    hh = SSM_GROUP
    ls = jnp.broadcast_to(log_step[..., None], (nl, g, p))
    params = [lam_re, lam_im, ls] + [jnp.zeros_like(ls)] * 5
    pcol = jnp.stack(params, axis=-1)
    prow = jnp.stack([jnp.concatenate([v, v], axis=-1) for v in params], axis=2)
    sel = jnp.tile(jnp.eye(hh, dtype=BF16), (1, SSM_CHUNK))
    tau = jnp.arange(CHUNK_W) // hh
    nrow = jnp.arange(LANES)[:, None]
    reps = jnp.stack([nrow == tau, nrow == SSM_CHUNK - 1 - tau, nrow == tau + 1]).astype(BF16)
    blk = lambda *s: pl.BlockSpec((None, S5_PREP_GROUPS) + s, lambda l, i: (l, i) + (0,) * len(s))
    return pl.pallas_call(
        _s5_prep_kernel,
        grid=(nl, g // S5_PREP_GROUPS),
        in_specs=[blk(p, 8), blk(8, LANES), blk(p, hh), blk(p, hh), blk(hh, p), blk(hh, p),
                  blk(p, hh), blk(p, hh), _const_spec(hh, CHUNK_W), _const_spec(3, LANES, CHUNK_W)],
        out_specs=[blk(CHUNK_W, CHUNK_W), blk(2 * p, CHUNK_W), blk(2 * p, CHUNK_W),
                   blk(8, LANES), blk(8, LANES)],
        out_shape=[jax.ShapeDtypeStruct((nl, g, CHUNK_W, CHUNK_W), BF16),
                   jax.ShapeDtypeStruct((nl, g, 2 * p, CHUNK_W), BF16),
                   jax.ShapeDtypeStruct((nl, g, 2 * p, CHUNK_W), BF16),
                   jax.ShapeDtypeStruct((nl, g, 8, LANES), F32),
                   jax.ShapeDtypeStruct((nl, g, 8, LANES), F32)],
        compiler_params=_params("parallel", "parallel"),
        name="s5_prep",
    )(pcol, prow, b_re, b_im, jnp.swapaxes(b_re, -1, -2), jnp.swapaxes(b_im, -1, -2),
      jnp.swapaxes(c_re, -1, -2), jnp.swapaxes(c_im, -1, -2), sel, reps)


def _s5_mix_kernel(u_ref, perm_ref, m_ref, wst_ref, wo_ref, apa_ref, apb_ref, dx_ref, y_ref,
                   xs_scr, ys_scr, *, nsteps):
    nb, seq, _ = u_ref.shape
    cps = seq // SSM_CHUNK
    nc = nb * cps
    gpt = GROUPS_PER_TILE
    nslab = SSM_CHUNK // gpt
    perm = perm_ref[...]

    def slab(j):
        return jnp.concatenate([
            jnp.concatenate([u_ref[b, pl.ds(gpt * j + s, cps, stride=SSM_CHUNK), :] for s in range(gpt)], axis=1)
            for b in range(nb)], axis=0)

    z = _dot(jnp.concatenate([slab(j) for j in range(nslab)], axis=0).astype(BF16), perm).astype(BF16)
    for j in range(nslab):
        for g in range(gpt):
            xs_scr[g, :, j * LANES:(j + 1) * LANES] = z[j * nc:(j + 1) * nc, g * LANES:(g + 1) * LANES]

    c = lax.broadcasted_iota(jnp.int32, (nc, 2 * SSM_STATE), 0) & (cps - 1)
    for g in range(gpt):
        x = xs_scr[g]
        y = _dot(x, m_ref[g])
        st = _dot_nt(x, wst_ref[g])
        apa, apb = apa_ref[g], apb_ref[g]
        for k in range(nsteps):
            d = 1 << k
            sh = jnp.where(c >= d, pltpu.roll(st, d, axis=0), 0.0)
            st = st + sh * apa[k:k + 1] + pltpu.roll(sh, SSM_STATE, axis=1) * apb[k:k + 1]
        carried = jnp.where(c >= 1, pltpu.roll(st, 1, axis=0), 0.0)
        y = y + _dot(carried.astype(BF16), wo_ref[g]) + dx_ref[g] * x.astype(F32)
        ys_scr[g] = jax.nn.gelu(y).astype(BF16)

    w = jnp.concatenate([
        jnp.concatenate([ys_scr[g, :, j * LANES:(j + 1) * LANES] for g in range(gpt)], axis=1)
        for j in range(nslab)], axis=0)
    o = _dot(w, perm)
    for j in range(nslab):
        for t in range(gpt):
            for b in range(nb):
                y_ref[b, pl.ds(gpt * j + t, cps, stride=SSM_CHUNK), :] = (
                    o[j * nc + b * cps:j * nc + (b + 1) * cps, t * LANES:(t + 1) * LANES])


def _s5_mix(l, u, prep, dx, perm):
    tiles, bsz, seq, _ = u.shape
    cps = seq // SSM_CHUNK
    assert cps & (cps - 1) == 0 and cps <= 256
    gpt = GROUPS_PER_TILE
    nb = 2 if bsz % 2 == 0 else 1
    m, wst, wo, apa, apb = prep
    tokens = pl.BlockSpec((None, nb, seq, LANES), lambda t, b: (t, b, 0, 0))
    grp = lambda *s: pl.BlockSpec((None, gpt) + s, lambda t, b: (l, t) + (0,) * len(s))
    return pl.pallas_call(
        functools.partial(_s5_mix_kernel, nsteps=cps.bit_length() - 1),
        grid=(tiles, bsz // nb),
        in_specs=[tokens, _const_spec(gpt * LANES, gpt * LANES), grp(CHUNK_W, CHUNK_W),
                  grp(2 * SSM_STATE, CHUNK_W), grp(2 * SSM_STATE, CHUNK_W), grp(8, LANES), grp(8, LANES),
                  grp(1, CHUNK_W)],
        out_specs=tokens,
        out_shape=jax.ShapeDtypeStruct(u.shape, F32),
        scratch_shapes=[pltpu.VMEM((gpt, nb * cps, CHUNK_W), BF16), pltpu.VMEM((gpt, nb * cps, CHUNK_W), BF16)],
        compiler_params=_params("parallel", "parallel"),
        name="s5_mix",
    )(u, perm, m, wst, wo, apa, apb, dx)


FULL = 0


def _flash_kernel(qt_ref, kt_ref, kind_ref, q_ref, k_ref, v_ref, o_ref, m_scr, acc_scr, *, heads):
    p_idx = pl.program_id(2)
    ki = kt_ref[p_idx]
    kind = kind_ref[p_idx]
    tq, tk = q_ref.shape[2], k_ref.shape[2]

    @pl.when(ki == 0)
    def _():
        m_scr[...] = jnp.full(m_scr.shape, -jnp.inf, F32)
        acc_scr[...] = jnp.zeros(acc_scr.shape, F32)

    def attend(j, nk, diag_offset):
        s = _dot_nt(q_ref[0, j], k_ref[0, j, 0:nk, :])
        if diag_offset is not None:
            row = lax.broadcasted_iota(jnp.int32, s.shape, 0)
            col = lax.broadcasted_iota(jnp.int32, s.shape, 1)
            s = jnp.where(col <= row + diag_offset, s, -jnp.inf)
        m_prev = m_scr[j]
        m_new = jnp.maximum(m_prev, jnp.max(s, axis=-1, keepdims=True))
        alpha = jnp.exp2(m_prev - m_new)
        p = jnp.exp2(s - jnp.tile(m_new, (1, nk // LANES)))
        acc_scr[j] = alpha * acc_scr[j] + _dot(p.astype(BF16), v_ref[0, j, 0:nk, :])
        m_scr[j] = m_new

    @pl.when(kind == FULL)
    def _():
        for j in range(heads):
            attend(j, tk, None)

    for r in range(1, tk // tq + 1):
        @pl.when(kind == r)
        def _(r=r):
            for j in range(heads):
                attend(j, r * tq, (r - 1) * tq)

    @pl.when(kind != FULL)
    def _():
        outs = []
        for j in range(heads):
            acc = acc_scr[j]
            outs.append(acc[:, :V_DIM] / acc[:, V_DIM:V_DIM + 1])
        o_ref[0] = jnp.concatenate(outs, axis=-1).astype(BF16)


def _flash(q, k, v, tq, heads_per_step):
    bsz, heads, seq, _ = q.shape
    hp = heads_per_step
    ratio = FLASH_KEY_RATIO
    tk = ratio * tq
    assert seq % tk == 0
    steps = []
    for i in range(seq // tq):
        steps += [(i, j, FULL) for j in range(i // ratio)] + [(i, i // ratio, i % ratio + 1)]
    qt, kt, kind = (jnp.asarray(col, jnp.int32) for col in zip(*steps))
    qspec = pl.BlockSpec((1, hp, tq, HEAD_PAD), lambda b, h, p, qt, kt, kind: (b, h, qt[p], 0))
    kspec = pl.BlockSpec((1, hp, tk, HEAD_PAD), lambda b, h, p, qt, kt, kind: (b, h, kt[p], 0))
    grid_spec = pltpu.PrefetchScalarGridSpec(
        num_scalar_prefetch=3,
        grid=(bsz, heads // hp, len(steps)),
        in_specs=[qspec, kspec, kspec],
        out_specs=pl.BlockSpec((1, tq, hp * V_DIM), lambda b, h, p, qt, kt, kind: (b, qt[p], h)),
        scratch_shapes=[pltpu.VMEM((hp, tq, LANES), F32), pltpu.VMEM((hp, tq, HEAD_PAD), F32)])
    return pl.pallas_call(
        functools.partial(_flash_kernel, heads=hp),
        grid_spec=grid_spec,
        out_shape=jax.ShapeDtypeStruct((bsz, seq, heads * V_DIM), BF16),
        compiler_params=_params("parallel", "parallel", "arbitrary"),
        name="flash",
    )(qt, kt, kind, q, k, v)


def _head_sums(v2, lane, width):
    out = jnp.zeros_like(v2)
    for hd in range(v2.shape[-1] // width):
        msk = (lane >= hd * width) & (lane < (hd + 1) * width)
        out = out + jnp.where(msk, jnp.sum(jnp.where(msk, v2, 0.0), axis=-1, keepdims=True), 0.0)
    return out


def _mem_prep_kernel(mem_ref, g_ref, w_ref, kg_ref, kt_ref, vm_ref):
    hm = _rms(mem_ref[0], g_ref[0]).astype(BF16)
    kv = _dot(hm, w_ref[0])
    k, v = kv[:, :MEM_WIDTH], kv[:, MEM_WIDTH:]
    lane = lax.broadcasted_iota(jnp.int32, k.shape, 1)
    ss = _head_sums(k * k, lane, MEM_HEAD_DIM)
    kn = k * lax.rsqrt(ss * (1.0 / MEM_HEAD_DIM) + EPS) * kg_ref[0] * (1.0 / math.sqrt(MEM_HEAD_DIM))
    knt = kn.T
    row = lax.broadcasted_iota(jnp.int32, knt.shape, 0)
    for hd in range(MEM_HEADS):
        lo, hi = hd * MEM_HEAD_DIM, (hd + 1) * MEM_HEAD_DIM
        kt_ref[0, 0, hd] = jnp.where((row >= lo) & (row < hi), knt, 0.0).astype(BF16)
        vm_ref[0, 0, hd] = jnp.where((lane >= lo) & (lane < hi), v, 0.0).astype(BF16)


def _mem_prep(mem, g, w_kv, k_gain):
    nl = g.shape[0]
    bsz = mem.shape[0]
    out = jax.ShapeDtypeStruct((nl, bsz, MEM_HEADS, N_MEM, MEM_WIDTH), BF16)
    return pl.pallas_call(
        _mem_prep_kernel,
        grid=(nl, bsz),
        in_specs=[pl.BlockSpec((1, N_MEM, D_MODEL), lambda l, b: (b, 0, 0)),
                  pl.BlockSpec((1, 1, D_MODEL), lambda l, b: (l, 0, 0)),
                  pl.BlockSpec((1, D_MODEL, 2 * MEM_WIDTH), lambda l, b: (l, 0, 0)),
                  pl.BlockSpec((1, 1, MEM_WIDTH), lambda l, b: (l, 0, 0))],
        out_specs=[pl.BlockSpec((1, 1, MEM_HEADS, MEM_WIDTH, N_MEM), lambda l, b: (l, b, 0, 0, 0)),
                   pl.BlockSpec((1, 1, MEM_HEADS, N_MEM, MEM_WIDTH), lambda l, b: (l, b, 0, 0, 0))],
        out_shape=[out, out],
        compiler_params=_params("parallel", "parallel"),
        name="mem_prep",
    )(mem, g, w_kv, k_gain)


def _outproj_kernel(x_ref, ys_ref, om_ref, wglu_ref, bglu_ref, gs_ref, gm_ref, wout_ref,
                    gmq_ref, wmq_ref, qg_ref, kt_ref, vm_ref, wmo_ref, gmlp_ref,
                    x2_ref, h3_ref):
    ys = jnp.concatenate([ys_ref[t] for t in range(LANE_TILES)], axis=1)
    yg = ys * jax.nn.sigmoid(_dot(ys.astype(BF16), wglu_ref[...]) + bglu_ref[...])
    n1 = _rms(yg, gs_ref[...]).astype(BF16)
    n2 = _rms(om_ref[...].astype(F32), gm_ref[...]).astype(BF16)
    x1 = x_ref[...] + _dot(n1, wout_ref[:SSM_WIDTH, :]) + _dot(n2, wout_ref[SSM_WIDTH:, :])

    q = _dot(_rms(x1, gmq_ref[...]).astype(BF16), wmq_ref[...])
    lane = lax.broadcasted_iota(jnp.int32, q.shape, 1)
    ss = _head_sums(q * q, lane, MEM_HEAD_DIM)
    qn = (q * lax.rsqrt(ss * (1.0 / MEM_HEAD_DIM) + EPS) * qg_ref[...]).astype(BF16)
    o = jnp.zeros(q.shape, F32)
    for hd in range(MEM_HEADS):
        s = _dot(qn, kt_ref[hd])
        p = jnp.exp(s - jnp.max(s, axis=-1, keepdims=True))
        inv = 1.0 / jnp.sum(p, axis=-1, keepdims=True)
        o = o + _dot(p.astype(BF16), vm_ref[hd]) * inv
    x2 = x1 + _dot(o.astype(BF16), wmo_ref[...])
    x2_ref[...] = x2
    h3_ref[...] = _rms(x2, gmlp_ref[...]).astype(BF16)


def _outproj(l, x, ys, om, w, kt, vm, tm):
    bsz, seq, _ = x.shape
    tok = lambda width: pl.BlockSpec((None, tm, width), lambda b, i: (b, i, 0))
    memb = pl.BlockSpec((None, None, MEM_HEADS, N_MEM, MEM_WIDTH), lambda b, i: (l, b, 0, 0, 0))
    return pl.pallas_call(
        _outproj_kernel,
        grid=(bsz, seq // tm),
        in_specs=[tok(D_MODEL), pl.BlockSpec((LANE_TILES, None, tm, LANES), lambda b, i: (0, b, i, 0)),
                  tok(MLA_WIDTH),
                  _layer_spec(l, SSM_WIDTH, SSM_WIDTH), _layer_spec(l, 1, SSM_WIDTH),
                  _layer_spec(l, 1, SSM_WIDTH), _layer_spec(l, 1, MLA_WIDTH),
                  _layer_spec(l, D_MODEL, D_MODEL), _layer_spec(l, 1, D_MODEL),
                  _layer_spec(l, D_MODEL, MEM_WIDTH), _layer_spec(l, 1, MEM_WIDTH),
                  memb, memb, _layer_spec(l, MEM_WIDTH, D_MODEL), _layer_spec(l, 1, D_MODEL)],
        out_specs=[tok(D_MODEL), tok(D_MODEL)],
        out_shape=[jax.ShapeDtypeStruct((bsz, seq, D_MODEL), F32),
                   jax.ShapeDtypeStruct((bsz, seq, D_MODEL), BF16)],
        compiler_params=_params("parallel", "parallel"),
        name="outproj",
    )(x, ys, om, w["w_glu"], w["b_glu"], w["g_ssm"], w["g_mla"], w["w_out"],
      w["g_memq"], w["w_mq"], w["mem_q_gain"], kt, vm, w["w_mo"], w["g_mlp"])


def _mlp_kernel(x_ref, h_ref, w1_ref, w2_ref, o_ref, *, ff_tile):
    h = h_ref[...]
    acc = x_ref[...]
    for c in range(D_FF // ff_tile):
        a = jnp.maximum(_dot(h, w1_ref[:, c * ff_tile:(c + 1) * ff_tile]), 0.0)
        acc = acc + _dot((a * a).astype(BF16), w2_ref[c * ff_tile:(c + 1) * ff_tile, :])
    o_ref[...] = acc


def _mlp(l, x, h, w1, w2, tm):
    t = x.shape[0]
    tok = pl.BlockSpec((tm, D_MODEL), lambda i: (i, 0))
    return pl.pallas_call(
        functools.partial(_mlp_kernel, ff_tile=1024),
        grid=(t // tm,),
        in_specs=[tok, tok, _layer_spec(l, D_MODEL, D_FF), _layer_spec(l, D_FF, D_MODEL)],
        out_specs=tok,
        out_shape=jax.ShapeDtypeStruct((t, D_MODEL), F32),
        compiler_params=_params("parallel"),
        name="mlp",
    )(x, h, w1, w2)


def _half_swap(w):
    h = w.shape[-1] // 2
    return jnp.concatenate([-w[..., h:], w[..., :h]], axis=-1)


def _half_swap_unsigned(w):
    h = w.shape[-1] // 2
    return jnp.concatenate([w[..., h:], w[..., :h]], axis=-1)


def _pad_last(w, before, after):
    pads = [(0, 0)] * (w.ndim - 1) + [(before, after)]
    return jnp.pad(w, pads)


def _trunk(tm, tq, x, mem, positions, norm_mix, w_in, ssm_lambda_re, ssm_lambda_im, ssm_log_step, ssm_b_re, ssm_b_im, ssm_c_re, ssm_c_im, ssm_d, ssm_w_glu, ssm_b_glu, mla_q_norm, mla_w_uq, mla_kv_norm, mla_w_ukv, mla_q_gain, mla_k_gain, out_norm_ssm, out_norm_mla, w_out, norm_mem_q, norm_mem_kv, mem_w_q, mem_w_kv, mem_q_gain, mem_k_gain, mem_w_o, norm_mlp, mlp_w1, mlp_w2):
    bsz, seq, _ = x.shape
    depth = norm_mix.shape[0]
    row = lambda a: a[:, None, :]
    tail = HEAD_PAD - QK_DIM

    s3 = SSM_WIDTH + Q_LORA + KV_LORA
    k_rope_w = w_in[..., s3:]
    w_in_x = jnp.concatenate([w_in[..., :s3], _pad_last(k_rope_w, QK_NOPE, tail),
                              _pad_last(_half_swap(k_rope_w), QK_NOPE, tail)], axis=-1).astype(BF16)
    wq = mla_w_uq.reshape(depth, Q_LORA, MLA_HEADS, QK_DIM)
    hw = MLA_HEADS * HEAD_PAD
    w_qa = _pad_last(wq, 0, tail).reshape(depth, Q_LORA, hw).astype(BF16)
    w_qb = _pad_last(_half_swap(wq[..., QK_NOPE:]), QK_NOPE, tail).reshape(depth, Q_LORA, hw).astype(BF16)
    wkv = mla_w_ukv.reshape(depth, KV_LORA, MLA_HEADS, QK_NOPE + V_DIM)
    w_k = _pad_last(wkv[..., :QK_NOPE], 0, HEAD_PAD - QK_NOPE).reshape(depth, KV_LORA, hw).astype(BF16)
    w_v = _pad_last(wkv[..., QK_NOPE:], 0, HEAD_PAD - V_DIM).reshape(depth, KV_LORA, hw).astype(BF16)
    q_rope_g, k_rope_g = mla_q_gain[:, QK_NOPE:], mla_k_gain[:, QK_NOPE:]
    mla_gains = jnp.stack([
        _pad_last(mla_q_gain, 0, tail),
        _pad_last(_half_swap_unsigned(q_rope_g), QK_NOPE, tail),
        _pad_last(mla_k_gain[:, :QK_NOPE], 0, HEAD_PAD - QK_NOPE),
        _pad_last(k_rope_g, QK_NOPE, tail),
        _pad_last(_half_swap_unsigned(k_rope_g), QK_NOPE, tail)], axis=1)
    mla_gains = jnp.pad(mla_gains, ((0, 0), (0, 3), (0, 0)))
    blk = jnp.arange(2 * HEAD_PAD) // HEAD_PAD
    bd = (blk[:, None] == blk[None, :]).astype(BF16)
    wmkv = mem_w_kv.reshape(depth, D_MODEL, MEM_HEADS, 2, MEM_HEAD_DIM)
    w_mkv = wmkv.transpose(0, 1, 3, 2, 4).reshape(depth, D_MODEL, 2 * MEM_WIDTH).astype(BF16)
    idx = jnp.arange(GROUPS_PER_TILE * LANES)
    dest = ((idx // SSM_GROUP) % GROUPS_PER_TILE) * LANES + (idx // LANES) * SSM_GROUP + idx % SSM_GROUP
    perm = (dest[:, None] == idx[None, :]).astype(BF16)

    w = dict(g_mix=row(norm_mix), w_in=w_in_x, g_q=row(mla_q_norm), w_qa=w_qa, w_qb=w_qb,
             g_kv=row(mla_kv_norm), w_k=w_k, w_v=w_v, mla_gains=mla_gains, bd=bd,
             w_glu=ssm_w_glu.astype(BF16), b_glu=row(ssm_b_glu), g_ssm=row(out_norm_ssm),
             g_mla=row(out_norm_mla), w_out=w_out.astype(BF16), g_memq=row(norm_mem_q),
             w_mq=mem_w_q.astype(BF16), mem_q_gain=row(jnp.tile(mem_q_gain, (1, MEM_HEADS))),
             w_mo=mem_w_o.astype(BF16), g_mlp=row(norm_mlp))
    w1_b = mlp_w1.astype(BF16)
    w2_b = mlp_w2.astype(BF16)

    ra, rb = _rope_tables(positions)
    prep = _s5_prep(ssm_lambda_re, ssm_lambda_im, ssm_log_step, ssm_b_re, ssm_b_im, ssm_c_re, ssm_c_im)
    kt_all, vm_all = _mem_prep(mem, row(norm_mem_kv), w_mkv, row(jnp.tile(mem_k_gain, (1, MEM_HEADS))))
    dx = jnp.tile(ssm_d.reshape(depth, SSM_GROUPS, 1, SSM_GROUP), (1, 1, 1, SSM_CHUNK))

    for l in range(depth):
        u, q, k, v = _inproj(l, x, w, ra, rb, min(2 * tm, seq))
        ys = _s5_mix(l, u, prep, dx, perm)
        om = _flash(q, k, v, tq, FLASH_HEADS)
        x2, h3 = _outproj(l, x, ys, om, w, kt_all, vm_all, min(2 * tm, seq))
        x = _mlp(l, x2.reshape(bsz * seq, D_MODEL), h3.reshape(bsz * seq, D_MODEL),
                 w1_b, w2_b, tm).reshape(bsz, seq, D_MODEL)
    return x


def kernel(x, mem, positions, norm_mix, w_in, ssm_lambda_re, ssm_lambda_im, ssm_log_step, ssm_b_re, ssm_b_im, ssm_c_re, ssm_c_im, ssm_d, ssm_w_glu, ssm_b_glu, mla_q_norm, mla_w_uq, mla_kv_norm, mla_w_ukv, mla_q_gain, mla_k_gain, out_norm_ssm, out_norm_mla, w_out, norm_mem_q, norm_mem_kv, mem_w_q, mem_w_kv, mem_q_gain, mem_k_gain, mem_w_o, norm_mlp, mlp_w1, mlp_w2):
    seq = x.shape[1]
    return _trunk(min(512, seq), min(512, seq // 2), x, mem, positions, norm_mix, w_in, ssm_lambda_re, ssm_lambda_im, ssm_log_step, ssm_b_re, ssm_b_im, ssm_c_re, ssm_c_im, ssm_d, ssm_w_glu, ssm_b_glu, mla_q_norm, mla_w_uq, mla_kv_norm, mla_w_ukv, mla_q_gain, mla_k_gain, out_norm_ssm, out_norm_mla, w_out, norm_mem_q, norm_mem_kv, mem_w_q, mem_w_kv, mem_q_gain, mem_k_gain, mem_w_o, norm_mlp, mlp_w1, mlp_w2)
```

```python
import functools
import math

import jax
import jax.numpy as jnp
from jax import lax
from jax.experimental import pallas as pl
from jax.experimental.pallas import tpu as pltpu

D_MODEL = 1024
N_MEM = 256
MEM_HEADS = 4
MEM_HEAD_DIM = 64
MEM_WIDTH = MEM_HEADS * MEM_HEAD_DIM
SSM_WIDTH = 512
MLA_WIDTH = 512
SSM_GROUP = 16
SSM_GROUPS = 32
SSM_STATE = 64
MLA_HEADS = 8
QK_NOPE = 64
QK_ROPE = 32
QK_DIM = QK_NOPE + QK_ROPE
V_DIM = 64
Q_LORA = 256
KV_LORA = 128
ROPE_THETA = 10000.0
D_FF = 4 * D_MODEL
EPS = 1e-6

LANES = 128
HEAD_PAD = 128
SSM_CHUNK = 32
CHUNK_W = SSM_CHUNK * SSM_GROUP
CHUNK_PITCH = SSM_CHUNK + 4
GROUPS_PER_TILE = LANES // SSM_GROUP
LANE_TILES = SSM_WIDTH // LANES
IN_COLS_PAD = SSM_WIDTH + Q_LORA + KV_LORA + 2 * LANES
FLASH_KEY_RATIO = 2
FLASH_HEADS = 8
VMEM_LIMIT = 56 * 1024 * 1024

F32 = jnp.float32
BF16 = jnp.bfloat16
HIGHEST = lax.Precision.HIGHEST


def _dot(a, b):
    return jnp.dot(a, b, preferred_element_type=F32)


def _dot_exact(a, b):
    return jnp.dot(a, b, precision=HIGHEST, preferred_element_type=F32)


def _select_cols(a, onehot):
    hi = a.astype(BF16)
    r1 = a - hi.astype(F32)
    mid = r1.astype(BF16)
    lo = (r1 - mid.astype(F32)).astype(BF16)
    return _dot(hi, onehot) + _dot(mid, onehot) + _dot(lo, onehot)


def _dot_nt(a, b):
    return lax.dot_general(a, b, (((1,), (1,)), ((), ())), preferred_element_type=F32)


def _rms(v, gain):
    return v * lax.rsqrt(jnp.mean(v * v, axis=-1, keepdims=True) + EPS) * gain


def _params(*sem, flags=None):
    return pltpu.CompilerParams(dimension_semantics=sem, vmem_limit_bytes=VMEM_LIMIT, flags=flags)


def _layer_spec(l, *shape):
    return pl.BlockSpec((None,) + shape, lambda *_: (l,) + (0,) * len(shape))


def _pitched(tokens):
    return tokens // SSM_CHUNK * CHUNK_PITCH


def _const_spec(*shape):
    return pl.BlockSpec(shape, lambda *_: (0,) * len(shape))


def _rope_table_kernel(pos_ref, freq_ref, a_ref, b_ref):
    ang = pos_ref[...].astype(F32) * freq_ref[...]
    lane = lax.broadcasted_iota(jnp.int32, ang.shape, 1)
    in_rope = (lane >= QK_NOPE) & (lane < QK_DIM)
    a_ref[...] = jnp.where(lane < QK_NOPE, 1.0, jnp.where(in_rope, jnp.cos(ang), 0.0))
    b_ref[...] = jnp.where(in_rope, jnp.sin(ang), 0.0)


def _rope_tables(positions):
    bsz, seq = positions.shape
    t = bsz * seq
    tm = min(1024, t)
    half = QK_ROPE // 2
    inv_freq = ROPE_THETA ** (-jnp.arange(half, dtype=F32) / half)
    freq = jnp.pad(jnp.tile(inv_freq, 2), (QK_NOPE, LANES - QK_DIM))[None, :]
    a, b = pl.pallas_call(
        _rope_table_kernel,
        grid=(t // tm,),
        in_specs=[pl.BlockSpec((tm, 1), lambda i: (i, 0)), _const_spec(1, LANES)],
        out_specs=[pl.BlockSpec((tm, LANES), lambda i: (i, 0))] * 2,
        out_shape=[jax.ShapeDtypeStruct((t, LANES), F32)] * 2,
        compiler_params=_params("parallel"),
        name="rope_tables",
    )(positions.reshape(t, 1), freq)
    return a.reshape(bsz, seq, LANES), b.reshape(bsz, seq, LANES)


def _inproj_kernel(x_ref, gmix_ref, win_ref, gq_ref, wqa_ref, wqb_ref, gkv_ref, wk_ref, wv_ref,
                   gains_ref, bd_ref, ra_ref, rb_ref, u_ref, q_ref, k_ref, v_ref):
    h = _rms(x_ref[...], gmix_ref[...]).astype(BF16)
    proj = _dot(h, win_ref[...])
    pad_rows = jnp.zeros((CHUNK_PITCH - SSM_CHUNK, LANES), F32)
    for t in range(LANE_TILES):
        for c in range(proj.shape[0] // SSM_CHUNK):
            u_ref[t, c * CHUNK_PITCH:c * CHUNK_PITCH + SSM_CHUNK, :] = (
                proj[c * SSM_CHUNK:(c + 1) * SSM_CHUNK, t * LANES:(t + 1) * LANES])
            u_ref[t, c * CHUNK_PITCH + SSM_CHUNK:(c + 1) * CHUNK_PITCH, :] = pad_rows

    two = lambda a: jnp.concatenate([a, a], axis=1)
    ra, rb = ra_ref[...], rb_ref[...]
    gains = gains_ref[...]
    bd = bd_ref[...]
    scale = math.log2(math.e) / math.sqrt(QK_DIM)
    inv_d = 1.0 / QK_DIM

    o1 = SSM_WIDTH
    o2 = o1 + Q_LORA
    o3 = o2 + KV_LORA
    hq = _rms(proj[:, o1:o2], gq_ref[...]).astype(BF16)
    qa = _dot(hq, wqa_ref[...])
    qb = _dot(hq, wqb_ref[...])
    hkv = _rms(proj[:, o2:o3], gkv_ref[...]).astype(BF16)
    kn = _dot(hkv, wk_ref[...])
    vm = _dot(hkv, wv_ref[...])

    kr = proj[:, o3:o3 + LANES]
    kr_sw = proj[:, o3 + LANES:o3 + 2 * LANES]
    kr_ss = two(_dot((kr * kr).astype(BF16), bd[:LANES, :LANES]))
    krot = two(kr * (gains[3:4] * ra) + kr_sw * (gains[4:5] * rb))
    ga, gb, gk = two(ra * gains[0:1]), two(rb * gains[1:2]), two(gains[2:3])
    lane = lax.broadcasted_iota(jnp.int32, ga.shape, 1)
    ones_col = jnp.where((lane & (LANES - 1)) == V_DIM, 1.0, 0.0)

    for hp in range(MLA_HEADS // 2):
        sl = slice(2 * hp * HEAD_PAD, (2 * hp + 2) * HEAD_PAD)
        q2 = qa[:, sl]
        rq = lax.rsqrt(_dot((q2 * q2).astype(BF16), bd) * inv_d + EPS) * scale
        qo = ((q2 * ga + qb[:, sl] * gb) * rq).astype(BF16)
        k2 = kn[:, sl]
        rk = lax.rsqrt((_dot((k2 * k2).astype(BF16), bd) + kr_ss) * inv_d + EPS)
        ko = ((k2 * gk + krot) * rk).astype(BF16)
        vo = (vm[:, sl] + ones_col).astype(BF16)
        for j in range(2):
            q_ref[2 * hp + j] = qo[:, j * HEAD_PAD:(j + 1) * HEAD_PAD]
            k_ref[2 * hp + j] = ko[:, j * HEAD_PAD:(j + 1) * HEAD_PAD]
            v_ref[2 * hp + j] = vo[:, j * HEAD_PAD:(j + 1) * HEAD_PAD]


def _inproj(l, x, w, ra, rb, tm):
    bsz, seq, _ = x.shape
    tok = lambda width: pl.BlockSpec((None, tm, width), lambda b, i: (b, i, 0))
    head = pl.BlockSpec((None, MLA_HEADS, tm, HEAD_PAD), lambda b, i: (b, 0, i, 0))
    hw = MLA_HEADS * HEAD_PAD
    head_shape = jax.ShapeDtypeStruct((bsz, MLA_HEADS, seq, HEAD_PAD), BF16)
    return pl.pallas_call(
        _inproj_kernel,
        grid=(bsz, seq // tm),
        in_specs=[tok(D_MODEL), _layer_spec(l, 1, D_MODEL), _layer_spec(l, D_MODEL, IN_COLS_PAD),
                  _layer_spec(l, 1, Q_LORA), _layer_spec(l, Q_LORA, hw), _layer_spec(l, Q_LORA, hw),
                  _layer_spec(l, 1, KV_LORA), _layer_spec(l, KV_LORA, hw), _layer_spec(l, KV_LORA, hw),
                  _layer_spec(l, 8, HEAD_PAD), _const_spec(2 * HEAD_PAD, 2 * HEAD_PAD),
                  tok(LANES), tok(LANES)],
        out_specs=[pl.BlockSpec((LANE_TILES, None, _pitched(tm), LANES), lambda b, i: (0, b, i, 0)),
                   head, head, head],
        out_shape=[jax.ShapeDtypeStruct((LANE_TILES, bsz, _pitched(seq), LANES), F32),
                   head_shape, head_shape, head_shape],
        compiler_params=_params("parallel", "parallel"),
        name="inproj",
    )(x, w["g_mix"], w["w_in"], w["g_q"], w["w_qa"], w["w_qb"], w["g_kv"], w["w_k"], w["w_v"],
      w["mla_gains"], w["bd"], ra, rb)


S5_PREP_GROUPS = 8


def _s5_prep_kernel(*refs):
    per_group_in, shared, outs = refs[:8], refs[8:10], refs[10:]
    for g in range(S5_PREP_GROUPS):
        _s5_prep_group(*[r.at[g] for r in per_group_in], *shared, *[r.at[g] for r in outs])


def _s5_prep_group(pcol_ref, prow_ref, b_re_ref, b_im_ref, bt_re_ref, bt_im_ref,
                   ct_re_ref, ct_im_ref, sel_ref, reps_ref,
                   m_ref, wst_ref, wo_ref, apa_ref, apb_ref):
    def zoh_coeff(a_re, a_im, lr, li):
        xr = a_re - 1.0
        den = lr * lr + li * li
        return (xr * lr + a_im * li) / den, (a_im * lr - xr * li) / den

    def zoh(lr, li, ls):
        step = jnp.exp(ls)
        mag = jnp.exp(lr * step)
        ang = li * step
        return zoh_coeff(mag * jnp.cos(ang), mag * jnp.sin(ang), lr, li) + (step,)

    pc = pcol_ref[...]
    lr, li = pc[:, 0:1], pc[:, 1:2]
    step = jnp.exp(pc[:, 2:3])
    n = lax.broadcasted_iota(jnp.int32, (SSM_STATE, LANES), 1).astype(F32)
    mag = jnp.exp(lr * step * n)
    th = li * step * n
    pw_re, pw_im = mag * jnp.cos(th), mag * jnp.sin(th)
    cfr, cfi = zoh_coeff(pw_re[:, 1:2], pw_im[:, 1:2], lr, li)
    b_re, b_im = b_re_ref[...], b_im_ref[...]
    bb_re = cfr * b_re - cfi * b_im
    bb_im = cfr * b_im + cfi * b_re

    pr = prow_ref[...]
    lr_row, li_row, ls_row = pr[0:1], pr[1:2], pr[2:3]
    cfr_row, cfi_row, step_row = zoh(lr_row, li_row, ls_row)
    bt_re, bt_im = bt_re_ref[...], bt_im_ref[...]
    bbt_re = cfr_row[:, :SSM_STATE] * bt_re - cfi_row[:, :SSM_STATE] * bt_im
    bbt_im = cfr_row[:, :SSM_STATE] * bt_im + cfi_row[:, :SSM_STATE] * bt_re

    sel = sel_ref[...]
    brep_re, brep_im = _select_cols(bb_re, sel), _select_cols(bb_im, sel)
    crep_re, crep_im = _select_cols(ct_re_ref[...], sel), _select_cols(ct_im_ref[...], sel)

    apow = lambda i: (_select_cols(pw_re, reps_ref[i]), _select_cols(pw_im, reps_ref[i]))

    e_re, e_im = apow(0)
    g_re = e_re * crep_re - e_im * crep_im
    g_im = e_re * crep_im + e_im * crep_re
    kt = _dot_exact(bbt_re, g_re) - _dot_exact(bbt_im, g_im)
    klane = lax.broadcasted_iota(jnp.int32, kt.shape, 1)
    m_ref[0:SSM_GROUP, :] = kt.astype(BF16)
    for s in range(1, SSM_CHUNK):
        shifted = jnp.where(klane >= s * SSM_GROUP, pltpu.roll(kt, s * SSM_GROUP, axis=1), 0.0)
        m_ref[s * SSM_GROUP:(s + 1) * SSM_GROUP, :] = shifted.astype(BF16)

    e_re, e_im = apow(1)
    wst_ref[0:SSM_STATE, :] = (e_re * brep_re - e_im * brep_im).astype(BF16)
    wst_ref[SSM_STATE:, :] = (e_re * brep_im + e_im * brep_re).astype(BF16)

    e_re, e_im = apow(2)
    wo_ref[0:SSM_STATE, :] = (e_re * crep_re - e_im * crep_im).astype(BF16)
    wo_ref[SSM_STATE:, :] = (-(e_re * crep_im + e_im * crep_re)).astype(BF16)

    krow = lax.broadcasted_iota(jnp.int32, (8, LANES), 0)
    klan = lax.broadcasted_iota(jnp.int32, (8, LANES), 1)
    nn = (SSM_CHUNK * jnp.left_shift(1, krow)).astype(F32)
    mag = jnp.exp(lr_row * step_row * nn)
    th = li_row * step_row * nn
    p_re, p_im = mag * jnp.cos(th), mag * jnp.sin(th)
    apa_ref[...] = p_re
    apb_ref[...] = jnp.where(klan < SSM_STATE, -p_im, p_im)


def _s5_prep(lam_re, lam_im, log_step, b_re, b_im, c_re, c_im):
    nl, g, p = lam_re.shape
    hh = SSM_GROUP
    ls = jnp.broadcast_to(log_step[..., None], (nl, g, p))
    params = [lam_re, lam_im, ls] + [jnp.zeros_like(ls)] * 5
    pcol = jnp.stack(params, axis=-1)
    prow = jnp.stack([jnp.concatenate([v, v], axis=-1) for v in params], axis=2)
    sel = jnp.tile(jnp.eye(hh, dtype=BF16), (1, SSM_CHUNK))
    tau = jnp.arange(CHUNK_W) // hh
    nrow = jnp.arange(LANES)[:, None]
    reps = jnp.stack([nrow == tau, nrow == SSM_CHUNK - 1 - tau, nrow == tau + 1]).astype(BF16)
    blk = lambda *s: pl.BlockSpec((None, S5_PREP_GROUPS) + s, lambda l, i: (l, i) + (0,) * len(s))
    return pl.pallas_call(
        _s5_prep_kernel,
        grid=(nl, g // S5_PREP_GROUPS),
        in_specs=[blk(p, 8), blk(8, LANES), blk(p, hh), blk(p, hh), blk(hh, p), blk(hh, p),
                  blk(p, hh), blk(p, hh), _const_spec(hh, CHUNK_W), _const_spec(3, LANES, CHUNK_W)],
        out_specs=[blk(CHUNK_W, CHUNK_W), blk(2 * p, CHUNK_W), blk(2 * p, CHUNK_W),
                   blk(8, LANES), blk(8, LANES)],
        out_shape=[jax.ShapeDtypeStruct((nl, g, CHUNK_W, CHUNK_W), BF16),
                   jax.ShapeDtypeStruct((nl, g, 2 * p, CHUNK_W), BF16),
                   jax.ShapeDtypeStruct((nl, g, 2 * p, CHUNK_W), BF16),
                   jax.ShapeDtypeStruct((nl, g, 8, LANES), F32),
                   jax.ShapeDtypeStruct((nl, g, 8, LANES), F32)],
        compiler_params=_params("parallel", "parallel"),
        name="s5_prep",
    )(pcol, prow, b_re, b_im, jnp.swapaxes(b_re, -1, -2), jnp.swapaxes(b_im, -1, -2),
      jnp.swapaxes(c_re, -1, -2), jnp.swapaxes(c_im, -1, -2), sel, reps)


def _s5_mix_kernel(u_ref, perm_ref, m_ref, wst_ref, wo_ref, apa_ref, apb_ref, dx_ref, y_ref,
                   *, nsteps):
    nb, rows, _ = u_ref.shape
    cps = rows // CHUNK_PITCH
    nc = nb * cps
    gpt = GROUPS_PER_TILE
    nslab = SSM_CHUNK // gpt
    perm = perm_ref[...]

    def slab(j):
        return jnp.concatenate([
            jnp.concatenate([u_ref[b, pl.ds(gpt * j + s, cps, stride=CHUNK_PITCH), :] for s in range(gpt)], axis=1)
            for b in range(nb)], axis=0)

    z = _dot(jnp.concatenate([slab(j) for j in range(nslab)], axis=0).astype(BF16), perm).astype(BF16)

    c = lax.broadcasted_iota(jnp.int32, (nc, 2 * SSM_STATE), 0) & (cps - 1)
    ys = []
    for g in range(gpt):
        x = jnp.concatenate([z[j * nc:(j + 1) * nc, g * LANES:(g + 1) * LANES] for j in range(nslab)],
                            axis=1)
        y = _dot(x, m_ref[g])
        st = _dot_nt(x, wst_ref[g])
        apa, apb = apa_ref[g], apb_ref[g]
        for k in range(nsteps):
            d = 1 << k
            sh = jnp.where(c >= d, pltpu.roll(st, d, axis=0), 0.0)
            st = st + sh * apa[k:k + 1] + pltpu.roll(sh, SSM_STATE, axis=1) * apb[k:k + 1]
        carried = jnp.where(c >= 1, pltpu.roll(st, 1, axis=0), 0.0)
        y = y + _dot(carried.astype(BF16), wo_ref[g]) + dx_ref[g] * x.astype(F32)
        ys.append(jax.nn.gelu(y).astype(BF16))

    w = jnp.concatenate([
        jnp.concatenate([ys[g][:, j * LANES:(j + 1) * LANES] for g in range(gpt)], axis=1)
        for j in range(nslab)], axis=0)
    o = _dot(w, perm)
    for b in range(nb):
        for j in range(nslab):
            for t in range(gpt):
                y_ref[b, pl.ds(gpt * j + t, cps, stride=CHUNK_PITCH), :] = (
                    o[j * nc + b * cps:j * nc + (b + 1) * cps, t * LANES:(t + 1) * LANES])
        for r in range(SSM_CHUNK, CHUNK_PITCH):
            y_ref[b, pl.ds(r, cps, stride=CHUNK_PITCH), :] = jnp.zeros((cps, LANES), F32)


def _s5_mix(l, u, prep, dx, perm):
    tiles, bsz, rows, _ = u.shape
    cps = rows // CHUNK_PITCH
    assert cps & (cps - 1) == 0 and cps <= 256
    gpt = GROUPS_PER_TILE
    nb = 2 if bsz % 2 == 0 else 1
    m, wst, wo, apa, apb = prep
    tokens = pl.BlockSpec((None, nb, rows, LANES), lambda t, b: (t, b, 0, 0))
    grp = lambda *s: pl.BlockSpec((None, gpt) + s, lambda t, b: (l, t) + (0,) * len(s))
    return pl.pallas_call(
        functools.partial(_s5_mix_kernel, nsteps=cps.bit_length() - 1),
        grid=(tiles, bsz // nb),
        in_specs=[tokens, _const_spec(gpt * LANES, gpt * LANES), grp(CHUNK_W, CHUNK_W),
                  grp(2 * SSM_STATE, CHUNK_W), grp(2 * SSM_STATE, CHUNK_W), grp(8, LANES), grp(8, LANES),
                  grp(1, CHUNK_W)],
        out_specs=tokens,
        out_shape=jax.ShapeDtypeStruct(u.shape, F32),
        compiler_params=_params("parallel", "parallel"),
        name="s5_mix",
    )(u, perm, m, wst, wo, apa, apb, dx)


FULL = 0


def _flash_kernel(qt_ref, kt_ref, kind_ref, q_ref, k_ref, v_ref, o_ref, m_scr, acc_scr, *, heads):
    p_idx = pl.program_id(2)
    ki = kt_ref[p_idx]
    kind = kind_ref[p_idx]
    tq, tk = q_ref.shape[2], k_ref.shape[2]

    @pl.when(ki == 0)
    def _():
        m_scr[...] = jnp.full(m_scr.shape, -jnp.inf, F32)
        acc_scr[...] = jnp.zeros(acc_scr.shape, F32)

    def attend(j, nk, diag_offset):
        s = _dot_nt(q_ref[0, j], k_ref[0, j, 0:nk, :])
        if diag_offset is not None:
            row = lax.broadcasted_iota(jnp.int32, s.shape, 0)
            col = lax.broadcasted_iota(jnp.int32, s.shape, 1)
            s = jnp.where(col <= row + diag_offset, s, -jnp.inf)
        m_prev = m_scr[j]
        m_new = jnp.maximum(m_prev, jnp.max(s, axis=-1, keepdims=True))
        alpha = jnp.exp2(m_prev - m_new)
        p = jnp.exp2(s - jnp.tile(m_new, (1, nk // LANES)))
        acc_scr[j] = alpha * acc_scr[j] + _dot(p.astype(BF16), v_ref[0, j, 0:nk, :])
        m_scr[j] = m_new

    @pl.when(kind == FULL)
    def _():
        for j in range(heads):
            attend(j, tk, None)

    for r in range(1, tk // tq + 1):
        @pl.when(kind == r)
        def _(r=r):
            for j in range(heads):
                attend(j, r * tq, (r - 1) * tq)

    @pl.when(kind != FULL)
    def _():
        outs = []
        for j in range(heads):
            acc = acc_scr[j]
            outs.append(acc[:, :V_DIM] / acc[:, V_DIM:V_DIM + 1])
        o_ref[0] = jnp.concatenate(outs, axis=-1).astype(BF16)


def _flash(q, k, v, tq, heads_per_step):
    bsz, heads, seq, _ = q.shape
    hp = heads_per_step
    ratio = FLASH_KEY_RATIO
    tk = ratio * tq
    assert seq % tk == 0
    steps = []
    for i in range(seq // tq):
        steps += [(i, j, FULL) for j in range(i // ratio)] + [(i, i // ratio, i % ratio + 1)]
    qt, kt, kind = (jnp.asarray(col, jnp.int32) for col in zip(*steps))
    qspec = pl.BlockSpec((1, hp, tq, HEAD_PAD), lambda b, h, p, qt, kt, kind: (b, h, qt[p], 0))
    kspec = pl.BlockSpec((1, hp, tk, HEAD_PAD), lambda b, h, p, qt, kt, kind: (b, h, kt[p], 0))
    grid_spec = pltpu.PrefetchScalarGridSpec(
        num_scalar_prefetch=3,
        grid=(bsz, heads // hp, len(steps)),
        in_specs=[qspec, kspec, kspec],
        out_specs=pl.BlockSpec((1, tq, hp * V_DIM), lambda b, h, p, qt, kt, kind: (b, qt[p], h)),
        scratch_shapes=[pltpu.VMEM((hp, tq, LANES), F32), pltpu.VMEM((hp, tq, HEAD_PAD), F32)])
    return pl.pallas_call(
        functools.partial(_flash_kernel, heads=hp),
        grid_spec=grid_spec,
        out_shape=jax.ShapeDtypeStruct((bsz, seq, heads * V_DIM), BF16),
        compiler_params=_params("parallel", "parallel", "arbitrary"),
        name="flash",
    )(qt, kt, kind, q, k, v)


def _head_sums(v2, lane, width):
    out = jnp.zeros_like(v2)
    for hd in range(v2.shape[-1] // width):
        msk = (lane >= hd * width) & (lane < (hd + 1) * width)
        out = out + jnp.where(msk, jnp.sum(jnp.where(msk, v2, 0.0), axis=-1, keepdims=True), 0.0)
    return out


def _mem_prep_kernel(mem_ref, g_ref, w_ref, kg_ref, kt_ref, vm_ref):
    hm = _rms(mem_ref[0], g_ref[0]).astype(BF16)
    kv = _dot(hm, w_ref[0])
    k, v = kv[:, :MEM_WIDTH], kv[:, MEM_WIDTH:]
    lane = lax.broadcasted_iota(jnp.int32, k.shape, 1)
    ss = _head_sums(k * k, lane, MEM_HEAD_DIM)
    kn = k * lax.rsqrt(ss * (1.0 / MEM_HEAD_DIM) + EPS) * kg_ref[0] * (1.0 / math.sqrt(MEM_HEAD_DIM))
    knt = kn.T
    row = lax.broadcasted_iota(jnp.int32, knt.shape, 0)
    for hd in range(MEM_HEADS):
        lo, hi = hd * MEM_HEAD_DIM, (hd + 1) * MEM_HEAD_DIM
        kt_ref[0, 0, hd] = jnp.where((row >= lo) & (row < hi), knt, 0.0).astype(BF16)
        vm_ref[0, 0, hd] = jnp.where((lane >= lo) & (lane < hi), v, 0.0).astype(BF16)


def _mem_prep(mem, g, w_kv, k_gain):
    nl = g.shape[0]
    bsz = mem.shape[0]
    out = jax.ShapeDtypeStruct((nl, bsz, MEM_HEADS, N_MEM, MEM_WIDTH), BF16)
    return pl.pallas_call(
        _mem_prep_kernel,
        grid=(nl, bsz),
        in_specs=[pl.BlockSpec((1, N_MEM, D_MODEL), lambda l, b: (b, 0, 0)),
                  pl.BlockSpec((1, 1, D_MODEL), lambda l, b: (l, 0, 0)),
                  pl.BlockSpec((1, D_MODEL, 2 * MEM_WIDTH), lambda l, b: (l, 0, 0)),
                  pl.BlockSpec((1, 1, MEM_WIDTH), lambda l, b: (l, 0, 0))],
        out_specs=[pl.BlockSpec((1, 1, MEM_HEADS, MEM_WIDTH, N_MEM), lambda l, b: (l, b, 0, 0, 0)),
                   pl.BlockSpec((1, 1, MEM_HEADS, N_MEM, MEM_WIDTH), lambda l, b: (l, b, 0, 0, 0))],
        out_shape=[out, out],
        compiler_params=_params("parallel", "parallel"),
        name="mem_prep",
    )(mem, g, w_kv, k_gain)


def _outproj_kernel(x_ref, ys_ref, om_ref, wglu_ref, bglu_ref, gs_ref, gm_ref, wout_ref,
                    gmq_ref, wmq_ref, qg_ref, kt_ref, vm_ref, wmo_ref, gmlp_ref,
                    x2_ref, h3_ref):
    nchunk = x_ref.shape[0] // SSM_CHUNK
    ys = jnp.concatenate([
        jnp.concatenate([ys_ref[t, c * CHUNK_PITCH:c * CHUNK_PITCH + SSM_CHUNK, :] for c in range(nchunk)], axis=0)
        for t in range(LANE_TILES)], axis=1)
    yg = ys * jax.nn.sigmoid(_dot(ys.astype(BF16), wglu_ref[...]) + bglu_ref[...])
    n1 = _rms(yg, gs_ref[...]).astype(BF16)
    n2 = _rms(om_ref[...].astype(F32), gm_ref[...]).astype(BF16)
    x1 = x_ref[...] + _dot(n1, wout_ref[:SSM_WIDTH, :]) + _dot(n2, wout_ref[SSM_WIDTH:, :])

    q = _dot(_rms(x1, gmq_ref[...]).astype(BF16), wmq_ref[...])
    lane = lax.broadcasted_iota(jnp.int32, q.shape, 1)
    ss = _head_sums(q * q, lane, MEM_HEAD_DIM)
    qn = (q * lax.rsqrt(ss * (1.0 / MEM_HEAD_DIM) + EPS) * qg_ref[...]).astype(BF16)
    o = jnp.zeros(q.shape, F32)
    for hd in range(MEM_HEADS):
        s = _dot(qn, kt_ref[hd])
        p = jnp.exp(s - jnp.max(s, axis=-1, keepdims=True))
        inv = 1.0 / jnp.sum(p, axis=-1, keepdims=True)
        o = o + _dot(p.astype(BF16), vm_ref[hd]) * inv
    x2 = x1 + _dot(o.astype(BF16), wmo_ref[...])
    x2_ref[...] = x2
    h3_ref[...] = _rms(x2, gmlp_ref[...]).astype(BF16)


def _outproj(l, x, ys, om, w, kt, vm, tm):
    bsz, seq, _ = x.shape
    tok = lambda width: pl.BlockSpec((None, tm, width), lambda b, i: (b, i, 0))
    memb = pl.BlockSpec((None, None, MEM_HEADS, N_MEM, MEM_WIDTH), lambda b, i: (l, b, 0, 0, 0))
    return pl.pallas_call(
        _outproj_kernel,
        grid=(bsz, seq // tm),
        in_specs=[tok(D_MODEL), pl.BlockSpec((LANE_TILES, None, _pitched(tm), LANES), lambda b, i: (0, b, i, 0)),
                  tok(MLA_WIDTH),
                  _layer_spec(l, SSM_WIDTH, SSM_WIDTH), _layer_spec(l, 1, SSM_WIDTH),
                  _layer_spec(l, 1, SSM_WIDTH), _layer_spec(l, 1, MLA_WIDTH),
                  _layer_spec(l, D_MODEL, D_MODEL), _layer_spec(l, 1, D_MODEL),
                  _layer_spec(l, D_MODEL, MEM_WIDTH), _layer_spec(l, 1, MEM_WIDTH),
                  memb, memb, _layer_spec(l, MEM_WIDTH, D_MODEL), _layer_spec(l, 1, D_MODEL)],
        out_specs=[tok(D_MODEL), tok(D_MODEL)],
        out_shape=[jax.ShapeDtypeStruct((bsz, seq, D_MODEL), F32),
                   jax.ShapeDtypeStruct((bsz, seq, D_MODEL), BF16)],
        compiler_params=_params("parallel", "parallel"),
        name="outproj",
    )(x, ys, om, w["w_glu"], w["b_glu"], w["g_ssm"], w["g_mla"], w["w_out"],
      w["g_memq"], w["w_mq"], w["mem_q_gain"], kt, vm, w["w_mo"], w["g_mlp"])


def _mlp_kernel(x_ref, h_ref, w1_ref, w2_ref, o_ref, *, ff_tile):
    h = h_ref[...]
    acc = x_ref[...]
    for c in range(D_FF // ff_tile):
        a = jnp.maximum(_dot(h, w1_ref[:, c * ff_tile:(c + 1) * ff_tile]), 0.0)
        acc = acc + _dot((a * a).astype(BF16), w2_ref[c * ff_tile:(c + 1) * ff_tile, :])
    o_ref[...] = acc


def _mlp(l, x, h, w1, w2, tm):
    t = x.shape[0]
    tok = pl.BlockSpec((tm, D_MODEL), lambda i: (i, 0))
    return pl.pallas_call(
        functools.partial(_mlp_kernel, ff_tile=1024),
        grid=(t // tm,),
        in_specs=[tok, tok, _layer_spec(l, D_MODEL, D_FF), _layer_spec(l, D_FF, D_MODEL)],
        out_specs=tok,
        out_shape=jax.ShapeDtypeStruct((t, D_MODEL), F32),
        compiler_params=_params("parallel"),
        name="mlp",
    )(x, h, w1, w2)


def _half_swap(w):
    h = w.shape[-1] // 2
    return jnp.concatenate([-w[..., h:], w[..., :h]], axis=-1)


def _half_swap_unsigned(w):
    h = w.shape[-1] // 2
    return jnp.concatenate([w[..., h:], w[..., :h]], axis=-1)


def _pad_last(w, before, after):
    pads = [(0, 0)] * (w.ndim - 1) + [(before, after)]
    return jnp.pad(w, pads)


def _trunk(tm, tq, x, mem, positions, norm_mix, w_in, ssm_lambda_re, ssm_lambda_im, ssm_log_step, ssm_b_re, ssm_b_im, ssm_c_re, ssm_c_im, ssm_d, ssm_w_glu, ssm_b_glu, mla_q_norm, mla_w_uq, mla_kv_norm, mla_w_ukv, mla_q_gain, mla_k_gain, out_norm_ssm, out_norm_mla, w_out, norm_mem_q, norm_mem_kv, mem_w_q, mem_w_kv, mem_q_gain, mem_k_gain, mem_w_o, norm_mlp, mlp_w1, mlp_w2):
    bsz, seq, _ = x.shape
    depth = norm_mix.shape[0]
    row = lambda a: a[:, None, :]
    tail = HEAD_PAD - QK_DIM

    s3 = SSM_WIDTH + Q_LORA + KV_LORA
    k_rope_w = w_in[..., s3:]
    w_in_x = jnp.concatenate([w_in[..., :s3], _pad_last(k_rope_w, QK_NOPE, tail),
                              _pad_last(_half_swap(k_rope_w), QK_NOPE, tail)], axis=-1).astype(BF16)
    wq = mla_w_uq.reshape(depth, Q_LORA, MLA_HEADS, QK_DIM)
    hw = MLA_HEADS * HEAD_PAD
    w_qa = _pad_last(wq, 0, tail).reshape(depth, Q_LORA, hw).astype(BF16)
    w_qb = _pad_last(_half_swap(wq[..., QK_NOPE:]), QK_NOPE, tail).reshape(depth, Q_LORA, hw).astype(BF16)
    wkv = mla_w_ukv.reshape(depth, KV_LORA, MLA_HEADS, QK_NOPE + V_DIM)
    w_k = _pad_last(wkv[..., :QK_NOPE], 0, HEAD_PAD - QK_NOPE).reshape(depth, KV_LORA, hw).astype(BF16)
    w_v = _pad_last(wkv[..., QK_NOPE:], 0, HEAD_PAD - V_DIM).reshape(depth, KV_LORA, hw).astype(BF16)
    q_rope_g, k_rope_g = mla_q_gain[:, QK_NOPE:], mla_k_gain[:, QK_NOPE:]
    mla_gains = jnp.stack([
        _pad_last(mla_q_gain, 0, tail),
        _pad_last(_half_swap_unsigned(q_rope_g), QK_NOPE, tail),
        _pad_last(mla_k_gain[:, :QK_NOPE], 0, HEAD_PAD - QK_NOPE),
        _pad_last(k_rope_g, QK_NOPE, tail),
        _pad_last(_half_swap_unsigned(k_rope_g), QK_NOPE, tail)], axis=1)
    mla_gains = jnp.pad(mla_gains, ((0, 0), (0, 3), (0, 0)))
    blk = jnp.arange(2 * HEAD_PAD) // HEAD_PAD
    bd = (blk[:, None] == blk[None, :]).astype(BF16)
    wmkv = mem_w_kv.reshape(depth, D_MODEL, MEM_HEADS, 2, MEM_HEAD_DIM)
    w_mkv = wmkv.transpose(0, 1, 3, 2, 4).reshape(depth, D_MODEL, 2 * MEM_WIDTH).astype(BF16)
    idx = jnp.arange(GROUPS_PER_TILE * LANES)
    dest = ((idx // SSM_GROUP) % GROUPS_PER_TILE) * LANES + (idx // LANES) * SSM_GROUP + idx % SSM_GROUP
    perm = (dest[:, None] == idx[None, :]).astype(BF16)

    w = dict(g_mix=row(norm_mix), w_in=w_in_x, g_q=row(mla_q_norm), w_qa=w_qa, w_qb=w_qb,
             g_kv=row(mla_kv_norm), w_k=w_k, w_v=w_v, mla_gains=mla_gains, bd=bd,
             w_glu=ssm_w_glu.astype(BF16), b_glu=row(ssm_b_glu), g_ssm=row(out_norm_ssm),
             g_mla=row(out_norm_mla), w_out=w_out.astype(BF16), g_memq=row(norm_mem_q),
             w_mq=mem_w_q.astype(BF16), mem_q_gain=row(jnp.tile(mem_q_gain, (1, MEM_HEADS))),
             w_mo=mem_w_o.astype(BF16), g_mlp=row(norm_mlp))
    w1_b = mlp_w1.astype(BF16)
    w2_b = mlp_w2.astype(BF16)

    ra, rb = _rope_tables(positions)
    prep = _s5_prep(ssm_lambda_re, ssm_lambda_im, ssm_log_step, ssm_b_re, ssm_b_im, ssm_c_re, ssm_c_im)
    kt_all, vm_all = _mem_prep(mem, row(norm_mem_kv), w_mkv, row(jnp.tile(mem_k_gain, (1, MEM_HEADS))))
    dx = jnp.tile(ssm_d.reshape(depth, SSM_GROUPS, 1, SSM_GROUP), (1, 1, 1, SSM_CHUNK))

    for l in range(depth):
        u, q, k, v = _inproj(l, x, w, ra, rb, min(2 * tm, seq))
        ys = _s5_mix(l, u, prep, dx, perm)
        om = _flash(q, k, v, tq, FLASH_HEADS)
        x2, h3 = _outproj(l, x, ys, om, w, kt_all, vm_all, min(2 * tm, seq))
        x = _mlp(l, x2.reshape(bsz * seq, D_MODEL), h3.reshape(bsz * seq, D_MODEL),
                 w1_b, w2_b, tm).reshape(bsz, seq, D_MODEL)
    return x


def kernel(x, mem, positions, norm_mix, w_in, ssm_lambda_re, ssm_lambda_im, ssm_log_step, ssm_b_re, ssm_b_im, ssm_c_re, ssm_c_im, ssm_d, ssm_w_glu, ssm_b_glu, mla_q_norm, mla_w_uq, mla_kv_norm, mla_w_ukv, mla_q_gain, mla_k_gain, out_norm_ssm, out_norm_mla, w_out, norm_mem_q, norm_mem_kv, mem_w_q, mem_w_kv, mem_q_gain, mem_k_gain, mem_w_o, norm_mlp, mlp_w1, mlp_w2):
    seq = x.shape[1]
    return _trunk(min(512, seq), min(512, seq // 2), x, mem, positions, norm_mix, w_in, ssm_lambda_re, ssm_lambda_im, ssm_log_step, ssm_b_re, ssm_b_im, ssm_c_re, ssm_c_im, ssm_d, ssm_w_glu, ssm_b_glu, mla_q_norm, mla_w_uq, mla_kv_norm, mla_w_ukv, mla_q_gain, mla_k_gain, out_norm_ssm, out_norm_mla, w_out, norm_mem_q, norm_mem_kv, mem_w_q, mem_w_kv, mem_q_gain, mem_k_gain, mem_w_o, norm_mlp, mlp_w1, mlp_w2)
```

```python
import functools
import math

import jax
import jax.numpy as jnp
from jax import lax
from jax.experimental import pallas as pl
from jax.experimental.pallas import tpu as pltpu

D_MODEL = 1024
N_MEM = 256
MEM_HEADS = 4
MEM_HEAD_DIM = 64
MEM_WIDTH = MEM_HEADS * MEM_HEAD_DIM
SSM_WIDTH = 512
MLA_WIDTH = 512
SSM_GROUP = 16
SSM_GROUPS = 32
SSM_STATE = 64
MLA_HEADS = 8
QK_NOPE = 64
QK_ROPE = 32
QK_DIM = QK_NOPE + QK_ROPE
V_DIM = 64
Q_LORA = 256
KV_LORA = 128
ROPE_THETA = 10000.0
D_FF = 4 * D_MODEL
EPS = 1e-6

LANES = 128
HEAD_PAD = 128
SSM_CHUNK = 32
CHUNK_W = SSM_CHUNK * SSM_GROUP
CHUNK_PITCH = SSM_CHUNK + 4
GROUPS_PER_TILE = LANES // SSM_GROUP
LANE_TILES = SSM_WIDTH // LANES
IN_COLS_PAD = SSM_WIDTH + Q_LORA + KV_LORA + 2 * LANES
FLASH_UNROLL = 4
FLASH_KEY_RATIO = 4
FLASH_HEADS = 8
VMEM_LIMIT = 56 * 1024 * 1024

F32 = jnp.float32
BF16 = jnp.bfloat16
HIGHEST = lax.Precision.HIGHEST


def _dot(a, b):
    return jnp.dot(a, b, preferred_element_type=F32)


def _dot_exact(a, b):
    return jnp.dot(a, b, precision=HIGHEST, preferred_element_type=F32)


def _select_cols(a, onehot):
    hi = a.astype(BF16)
    r1 = a - hi.astype(F32)
    mid = r1.astype(BF16)
    lo = (r1 - mid.astype(F32)).astype(BF16)
    return _dot(hi, onehot) + _dot(mid, onehot) + _dot(lo, onehot)


def _dot_nt(a, b):
    return lax.dot_general(a, b, (((1,), (1,)), ((), ())), preferred_element_type=F32)


def _rms(v, gain):
    return v * lax.rsqrt(jnp.mean(v * v, axis=-1, keepdims=True) + EPS) * gain


def _params(*sem, flags=None):
    return pltpu.CompilerParams(dimension_semantics=sem, vmem_limit_bytes=VMEM_LIMIT, flags=flags)


def _layer_spec(l, *shape):
    return pl.BlockSpec((None,) + shape, lambda *_: (l,) + (0,) * len(shape))


def _pitched(tokens):
    return tokens // SSM_CHUNK * CHUNK_PITCH


def _const_spec(*shape):
    return pl.BlockSpec(shape, lambda *_: (0,) * len(shape))


def _rope_table_kernel(pos_ref, freq_ref, a_ref, b_ref):
    ang = pos_ref[...].astype(F32) * freq_ref[...]
    lane = lax.broadcasted_iota(jnp.int32, ang.shape, 1)
    in_rope = (lane >= QK_NOPE) & (lane < QK_DIM)
    a_ref[...] = jnp.where(lane < QK_NOPE, 1.0, jnp.where(in_rope, jnp.cos(ang), 0.0))
    b_ref[...] = jnp.where(in_rope, jnp.sin(ang), 0.0)


def _rope_tables(positions):
    bsz, seq = positions.shape
    t = bsz * seq
    tm = min(1024, t)
    half = QK_ROPE // 2
    inv_freq = ROPE_THETA ** (-jnp.arange(half, dtype=F32) / half)
    freq = jnp.pad(jnp.tile(inv_freq, 2), (QK_NOPE, LANES - QK_DIM))[None, :]
    a, b = pl.pallas_call(
        _rope_table_kernel,
        grid=(t // tm,),
        in_specs=[pl.BlockSpec((tm, 1), lambda i: (i, 0)), _const_spec(1, LANES)],
        out_specs=[pl.BlockSpec((tm, LANES), lambda i: (i, 0))] * 2,
        out_shape=[jax.ShapeDtypeStruct((t, LANES), F32)] * 2,
        compiler_params=_params("parallel"),
        name="rope_tables",
    )(positions.reshape(t, 1), freq)
    return a.reshape(bsz, seq, LANES), b.reshape(bsz, seq, LANES)


def _inproj_kernel(x_ref, gmix_ref, win_ref, gq_ref, wqa_ref, wqb_ref, gkv_ref, wk_ref, wv_ref,
                   gains_ref, bd_ref, ra_ref, rb_ref, u_ref, q_ref, k_ref, v_ref):
    h = _rms(x_ref[...], gmix_ref[...]).astype(BF16)
    proj = _dot(h, win_ref[...])
    pad_rows = jnp.zeros((CHUNK_PITCH - SSM_CHUNK, LANES), F32)
    for t in range(LANE_TILES):
        for c in range(proj.shape[0] // SSM_CHUNK):
            u_ref[t, c * CHUNK_PITCH:c * CHUNK_PITCH + SSM_CHUNK, :] = (
                proj[c * SSM_CHUNK:(c + 1) * SSM_CHUNK, t * LANES:(t + 1) * LANES])
            u_ref[t, c * CHUNK_PITCH + SSM_CHUNK:(c + 1) * CHUNK_PITCH, :] = pad_rows

    two = lambda a: jnp.concatenate([a, a], axis=1)
    ra, rb = ra_ref[...], rb_ref[...]
    gains = gains_ref[...]
    bd = bd_ref[...]
    scale = math.log2(math.e) / math.sqrt(QK_DIM)
    inv_d = 1.0 / QK_DIM

    o1 = SSM_WIDTH
    o2 = o1 + Q_LORA
    o3 = o2 + KV_LORA
    hq = _rms(proj[:, o1:o2], gq_ref[...]).astype(BF16)
    qa = _dot(hq, wqa_ref[...])
    qb = _dot(hq, wqb_ref[...])
    hkv = _rms(proj[:, o2:o3], gkv_ref[...]).astype(BF16)
    kn = _dot(hkv, wk_ref[...])
    vm = _dot(hkv, wv_ref[...])

    kr = proj[:, o3:o3 + LANES]
    kr_sw = proj[:, o3 + LANES:o3 + 2 * LANES]
    kr_ss = two(_dot((kr * kr).astype(BF16), bd[:LANES, :LANES]))
    krot = two(kr * (gains[3:4] * ra) + kr_sw * (gains[4:5] * rb))
    ga, gb, gk = two(ra * gains[0:1]), two(rb * gains[1:2]), two(gains[2:3])
    lane = lax.broadcasted_iota(jnp.int32, ga.shape, 1)
    ones_col = jnp.where((lane & (LANES - 1)) == V_DIM, 1.0, 0.0)

    for hp in range(MLA_HEADS // 2):
        sl = slice(2 * hp * HEAD_PAD, (2 * hp + 2) * HEAD_PAD)
        q2 = qa[:, sl]
        rq = lax.rsqrt(_dot((q2 * q2).astype(BF16), bd) * inv_d + EPS) * scale
        qo = ((q2 * ga + qb[:, sl] * gb) * rq).astype(BF16)
        k2 = kn[:, sl]
        rk = lax.rsqrt((_dot((k2 * k2).astype(BF16), bd) + kr_ss) * inv_d + EPS)
        ko = ((k2 * gk + krot) * rk).astype(BF16)
        vo = (vm[:, sl] + ones_col).astype(BF16)
        for j in range(2):
            q_ref[2 * hp + j] = qo[:, j * HEAD_PAD:(j + 1) * HEAD_PAD]
            k_ref[2 * hp + j] = ko[:, j * HEAD_PAD:(j + 1) * HEAD_PAD]
            v_ref[2 * hp + j] = vo[:, j * HEAD_PAD:(j + 1) * HEAD_PAD]


def _inproj(l, x, w, ra, rb, tm):
    bsz, seq, _ = x.shape
    tok = lambda width: pl.BlockSpec((None, tm, width), lambda b, i: (b, i, 0))
    head = pl.BlockSpec((None, MLA_HEADS, tm, HEAD_PAD), lambda b, i: (b, 0, i, 0))
    hw = MLA_HEADS * HEAD_PAD
    head_shape = jax.ShapeDtypeStruct((bsz, MLA_HEADS, seq, HEAD_PAD), BF16)
    return pl.pallas_call(
        _inproj_kernel,
        grid=(bsz, seq // tm),
        in_specs=[tok(D_MODEL), _layer_spec(l, 1, D_MODEL), _layer_spec(l, D_MODEL, IN_COLS_PAD),
                  _layer_spec(l, 1, Q_LORA), _layer_spec(l, Q_LORA, hw), _layer_spec(l, Q_LORA, hw),
                  _layer_spec(l, 1, KV_LORA), _layer_spec(l, KV_LORA, hw), _layer_spec(l, KV_LORA, hw),
                  _layer_spec(l, 8, HEAD_PAD), _const_spec(2 * HEAD_PAD, 2 * HEAD_PAD),
                  tok(LANES), tok(LANES)],
        out_specs=[pl.BlockSpec((LANE_TILES, None, _pitched(tm), LANES), lambda b, i: (0, b, i, 0)),
                   head, head, head],
        out_shape=[jax.ShapeDtypeStruct((LANE_TILES, bsz, _pitched(seq), LANES), F32),
                   head_shape, head_shape, head_shape],
        compiler_params=_params("parallel", "parallel"),
        name="inproj",
    )(x, w["g_mix"], w["w_in"], w["g_q"], w["w_qa"], w["w_qb"], w["g_kv"], w["w_k"], w["w_v"],
      w["mla_gains"], w["bd"], ra, rb)


S5_PREP_GROUPS = 8


def _s5_prep_kernel(*refs):
    per_group_in, shared, outs = refs[:8], refs[8:10], refs[10:]
    for g in range(S5_PREP_GROUPS):
        _s5_prep_group(*[r.at[g] for r in per_group_in], *shared, *[r.at[g] for r in outs])


def _s5_prep_group(pcol_ref, prow_ref, b_re_ref, b_im_ref, bt_re_ref, bt_im_ref,
                   ct_re_ref, ct_im_ref, sel_ref, reps_ref,
                   m_ref, wst_ref, wo_ref, apa_ref, apb_ref):
    def zoh_coeff(a_re, a_im, lr, li):
        xr = a_re - 1.0
        den = lr * lr + li * li
        return (xr * lr + a_im * li) / den, (a_im * lr - xr * li) / den

    def zoh(lr, li, ls):
        step = jnp.exp(ls)
        mag = jnp.exp(lr * step)
        ang = li * step
        return zoh_coeff(mag * jnp.cos(ang), mag * jnp.sin(ang), lr, li) + (step,)

    pc = pcol_ref[...]
    lr, li = pc[:, 0:1], pc[:, 1:2]
    step = jnp.exp(pc[:, 2:3])
    n = lax.broadcasted_iota(jnp.int32, (SSM_STATE, LANES), 1).astype(F32)
    mag = jnp.exp(lr * step * n)
    th = li * step * n
    pw_re, pw_im = mag * jnp.cos(th), mag * jnp.sin(th)
    cfr, cfi = zoh_coeff(pw_re[:, 1:2], pw_im[:, 1:2], lr, li)
    b_re, b_im = b_re_ref[...], b_im_ref[...]
    bb_re = cfr * b_re - cfi * b_im
    bb_im = cfr * b_im + cfi * b_re

    pr = prow_ref[...]
    lr_row, li_row, ls_row = pr[0:1], pr[1:2], pr[2:3]
    cfr_row, cfi_row, step_row = zoh(lr_row, li_row, ls_row)
    bt_re, bt_im = bt_re_ref[...], bt_im_ref[...]
    bbt_re = cfr_row[:, :SSM_STATE] * bt_re - cfi_row[:, :SSM_STATE] * bt_im
    bbt_im = cfr_row[:, :SSM_STATE] * bt_im + cfi_row[:, :SSM_STATE] * bt_re

    sel = sel_ref[...]
    brep_re, brep_im = _select_cols(bb_re, sel), _select_cols(bb_im, sel)
    crep_re, crep_im = _select_cols(ct_re_ref[...], sel), _select_cols(ct_im_ref[...], sel)

    apow = lambda i: (_select_cols(pw_re, reps_ref[i]), _select_cols(pw_im, reps_ref[i]))

    e_re, e_im = apow(0)
    g_re = e_re * crep_re - e_im * crep_im
    g_im = e_re * crep_im + e_im * crep_re
    kt = _dot_exact(bbt_re, g_re) - _dot_exact(bbt_im, g_im)
    klane = lax.broadcasted_iota(jnp.int32, kt.shape, 1)
    m_ref[0:SSM_GROUP, :] = kt.astype(BF16)
    for s in range(1, SSM_CHUNK):
        shifted = jnp.where(klane >= s * SSM_GROUP, pltpu.roll(kt, s * SSM_GROUP, axis=1), 0.0)
        m_ref[s * SSM_GROUP:(s + 1) * SSM_GROUP, :] = shifted.astype(BF16)

    e_re, e_im = apow(1)
    wst_ref[0:SSM_STATE, :] = (e_re * brep_re - e_im * brep_im).astype(BF16)
    wst_ref[SSM_STATE:, :] = (e_re * brep_im + e_im * brep_re).astype(BF16)

    e_re, e_im = apow(2)
    wo_ref[0:SSM_STATE, :] = (e_re * crep_re - e_im * crep_im).astype(BF16)
    wo_ref[SSM_STATE:, :] = (-(e_re * crep_im + e_im * crep_re)).astype(BF16)

    krow = lax.broadcasted_iota(jnp.int32, (8, LANES), 0)
    klan = lax.broadcasted_iota(jnp.int32, (8, LANES), 1)
    nn = (SSM_CHUNK * jnp.left_shift(1, krow)).astype(F32)
    mag = jnp.exp(lr_row * step_row * nn)
    th = li_row * step_row * nn
    p_re, p_im = mag * jnp.cos(th), mag * jnp.sin(th)
    apa_ref[...] = p_re
    apb_ref[...] = jnp.where(klan < SSM_STATE, -p_im, p_im)


def _s5_prep(lam_re, lam_im, log_step, b_re, b_im, c_re, c_im):
    nl, g, p = lam_re.shape
    hh = SSM_GROUP
    ls = jnp.broadcast_to(log_step[..., None], (nl, g, p))
    params = [lam_re, lam_im, ls] + [jnp.zeros_like(ls)] * 5
    pcol = jnp.stack(params, axis=-1)
    prow = jnp.stack([jnp.concatenate([v, v], axis=-1) for v in params], axis=2)
    sel = jnp.tile(jnp.eye(hh, dtype=BF16), (1, SSM_CHUNK))
    tau = jnp.arange(CHUNK_W) // hh
    nrow = jnp.arange(LANES)[:, None]
    reps = jnp.stack([nrow == tau, nrow == SSM_CHUNK - 1 - tau, nrow == tau + 1]).astype(BF16)
    blk = lambda *s: pl.BlockSpec((None, S5_PREP_GROUPS) + s, lambda l, i: (l, i) + (0,) * len(s))
    return pl.pallas_call(
        _s5_prep_kernel,
        grid=(nl, g // S5_PREP_GROUPS),
        in_specs=[blk(p, 8), blk(8, LANES), blk(p, hh), blk(p, hh), blk(hh, p), blk(hh, p),
                  blk(p, hh), blk(p, hh), _const_spec(hh, CHUNK_W), _const_spec(3, LANES, CHUNK_W)],
        out_specs=[blk(CHUNK_W, CHUNK_W), blk(2 * p, CHUNK_W), blk(2 * p, CHUNK_W),
                   blk(8, LANES), blk(8, LANES)],
        out_shape=[jax.ShapeDtypeStruct((nl, g, CHUNK_W, CHUNK_W), BF16),
                   jax.ShapeDtypeStruct((nl, g, 2 * p, CHUNK_W), BF16),
                   jax.ShapeDtypeStruct((nl, g, 2 * p, CHUNK_W), BF16),
                   jax.ShapeDtypeStruct((nl, g, 8, LANES), F32),
                   jax.ShapeDtypeStruct((nl, g, 8, LANES), F32)],
        compiler_params=_params("parallel", "parallel"),
        name="s5_prep",
    )(pcol, prow, b_re, b_im, jnp.swapaxes(b_re, -1, -2), jnp.swapaxes(b_im, -1, -2),
      jnp.swapaxes(c_re, -1, -2), jnp.swapaxes(c_im, -1, -2), sel, reps)


def _s5_mix_kernel(u_ref, perm_ref, m_ref, wst_ref, wo_ref, apa_ref, apb_ref, dx_ref, y_ref,
                   *, nsteps):
    nb, rows, _ = u_ref.shape
    cps = rows // CHUNK_PITCH
    nc = nb * cps
    gpt = GROUPS_PER_TILE
    nslab = SSM_CHUNK // gpt
    perm = perm_ref[...]

    def slab(j):
        return jnp.concatenate([
            jnp.concatenate([u_ref[b, pl.ds(gpt * j + s, cps, stride=CHUNK_PITCH), :] for s in range(gpt)], axis=1)
            for b in range(nb)], axis=0)

    z = _dot(jnp.concatenate([slab(j) for j in range(nslab)], axis=0).astype(BF16), perm).astype(BF16)

    c = lax.broadcasted_iota(jnp.int32, (nc, 2 * SSM_STATE), 0) & (cps - 1)
    ys = []
    for g in range(gpt):
        x = jnp.concatenate([z[j * nc:(j + 1) * nc, g * LANES:(g + 1) * LANES] for j in range(nslab)],
                            axis=1)
        y = _dot(x, m_ref[g])
        st = _dot_nt(x, wst_ref[g])
        apa, apb = apa_ref[g], apb_ref[g]
        for k in range(nsteps):
            d = 1 << k
            sh = jnp.where(c >= d, pltpu.roll(st, d, axis=0), 0.0)
            st = st + sh * apa[k:k + 1] + pltpu.roll(sh, SSM_STATE, axis=1) * apb[k:k + 1]
        carried = jnp.where(c >= 1, pltpu.roll(st, 1, axis=0), 0.0)
        y = y + _dot(carried.astype(BF16), wo_ref[g]) + dx_ref[g] * x.astype(F32)
        ys.append(jax.nn.gelu(y).astype(BF16))

    w = jnp.concatenate([
        jnp.concatenate([ys[g][:, j * LANES:(j + 1) * LANES] for g in range(gpt)], axis=1)
        for j in range(nslab)], axis=0)
    o = _dot(w, perm)
    for b in range(nb):
        for j in range(nslab):
            for t in range(gpt):
                y_ref[b, pl.ds(gpt * j + t, cps, stride=CHUNK_PITCH), :] = (
                    o[j * nc + b * cps:j * nc + (b + 1) * cps, t * LANES:(t + 1) * LANES])
        for r in range(SSM_CHUNK, CHUNK_PITCH):
            y_ref[b, pl.ds(r, cps, stride=CHUNK_PITCH), :] = jnp.zeros((cps, LANES), F32)


def _s5_mix(l, u, prep, dx, perm):
    tiles, bsz, rows, _ = u.shape
    cps = rows // CHUNK_PITCH
    assert cps & (cps - 1) == 0 and cps <= 256
    gpt = GROUPS_PER_TILE
    nb = 2 if bsz % 2 == 0 else 1
    m, wst, wo, apa, apb = prep
    tokens = pl.BlockSpec((None, nb, rows, LANES), lambda t, b: (t, b, 0, 0))
    grp = lambda *s: pl.BlockSpec((None, gpt) + s, lambda t, b: (l, t) + (0,) * len(s))
    return pl.pallas_call(
        functools.partial(_s5_mix_kernel, nsteps=cps.bit_length() - 1),
        grid=(tiles, bsz // nb),
        in_specs=[tokens, _const_spec(gpt * LANES, gpt * LANES), grp(CHUNK_W, CHUNK_W),
                  grp(2 * SSM_STATE, CHUNK_W), grp(2 * SSM_STATE, CHUNK_W), grp(8, LANES), grp(8, LANES),
                  grp(1, CHUNK_W)],
        out_specs=tokens,
        out_shape=jax.ShapeDtypeStruct(u.shape, F32),
        compiler_params=_params("parallel", "parallel"),
        name="s5_mix",
    )(u, perm, m, wst, wo, apa, apb, dx)


FULL = 0


def _flash_kernel(qt_ref, kt_ref, kind_ref, q_ref, k_ref, v_ref, o_ref, m_scr, acc_scr, *, heads):
    p_idx = pl.program_id(2)
    ki = kt_ref[p_idx]
    kind = kind_ref[p_idx]
    tq, tk = q_ref.shape[2], k_ref.shape[2]

    @pl.when(ki == 0)
    def _():
        m_scr[...] = jnp.full(m_scr.shape, -jnp.inf, F32)
        acc_scr[...] = jnp.zeros(acc_scr.shape, F32)

    def attend(j, nk, diag_offset):
        s = _dot_nt(q_ref[0, j], k_ref[0, j, 0:nk, :])
        if diag_offset is not None:
            row = lax.broadcasted_iota(jnp.int32, s.shape, 0)
            col = lax.broadcasted_iota(jnp.int32, s.shape, 1)
            s = jnp.where(col <= row + diag_offset, s, -jnp.inf)
        m_prev = m_scr[j]
        m_new = jnp.maximum(m_prev, jnp.max(s, axis=-1, keepdims=True))
        alpha = jnp.exp2(m_prev - m_new)
        p = jnp.exp2(s - jnp.tile(m_new, (1, nk // LANES)))
        acc_scr[j] = alpha * acc_scr[j] + _dot(p.astype(BF16), v_ref[0, j, 0:nk, :])
        m_scr[j] = m_new

    def all_heads(nk, diag_offset):
        def group(g, carry):
            for j in range(FLASH_UNROLL):
                attend(g * FLASH_UNROLL + j, nk, diag_offset)
            return carry
        lax.fori_loop(0, heads // FLASH_UNROLL, group, 0)

    @pl.when(kind == FULL)
    def _():
        all_heads(tk, None)

    for r in range(1, tk // tq + 1):
        @pl.when(kind == r)
        def _(r=r):
            all_heads(r * tq, (r - 1) * tq)

    @pl.when(kind != FULL)
    def _():
        outs = []
        for j in range(heads):
            acc = acc_scr[j]
            outs.append(acc[:, :V_DIM] / acc[:, V_DIM:V_DIM + 1])
        o_ref[0] = jnp.concatenate(outs, axis=-1).astype(BF16)


def _flash(q, k, v, tq, heads_per_step):
    bsz, heads, seq, _ = q.shape
    hp = heads_per_step
    ratio = FLASH_KEY_RATIO
    tk = ratio * tq
    assert seq % tk == 0
    steps = []
    for i in range(seq // tq):
        steps += [(i, j, FULL) for j in range(i // ratio)] + [(i, i // ratio, i % ratio + 1)]
    qt, kt, kind = (jnp.asarray(col, jnp.int32) for col in zip(*steps))
    qspec = pl.BlockSpec((1, hp, tq, HEAD_PAD), lambda b, h, p, qt, kt, kind: (b, h, qt[p], 0))
    kspec = pl.BlockSpec((1, hp, tk, HEAD_PAD), lambda b, h, p, qt, kt, kind: (b, h, kt[p], 0))
    grid_spec = pltpu.PrefetchScalarGridSpec(
        num_scalar_prefetch=3,
        grid=(bsz, heads // hp, len(steps)),
        in_specs=[qspec, kspec, kspec],
        out_specs=pl.BlockSpec((1, tq, hp * V_DIM), lambda b, h, p, qt, kt, kind: (b, qt[p], h)),
        scratch_shapes=[pltpu.VMEM((hp, tq, LANES), F32), pltpu.VMEM((hp, tq, HEAD_PAD), F32)])
    return pl.pallas_call(
        functools.partial(_flash_kernel, heads=hp),
        grid_spec=grid_spec,
        out_shape=jax.ShapeDtypeStruct((bsz, seq, heads * V_DIM), BF16),
        compiler_params=_params("parallel", "parallel", "arbitrary"),
        name="flash",
    )(qt, kt, kind, q, k, v)


def _head_sums(v2, lane, width):
    out = jnp.zeros_like(v2)
    for hd in range(v2.shape[-1] // width):
        msk = (lane >= hd * width) & (lane < (hd + 1) * width)
        out = out + jnp.where(msk, jnp.sum(jnp.where(msk, v2, 0.0), axis=-1, keepdims=True), 0.0)
    return out


def _mem_prep_kernel(mem_ref, g_ref, w_ref, kg_ref, kt_ref, vm_ref):
    hm = _rms(mem_ref[0], g_ref[0]).astype(BF16)
    kv = _dot(hm, w_ref[0])
    k, v = kv[:, :MEM_WIDTH], kv[:, MEM_WIDTH:]
    lane = lax.broadcasted_iota(jnp.int32, k.shape, 1)
    ss = _head_sums(k * k, lane, MEM_HEAD_DIM)
    kn = k * lax.rsqrt(ss * (1.0 / MEM_HEAD_DIM) + EPS) * kg_ref[0] * (1.0 / math.sqrt(MEM_HEAD_DIM))
    knt = kn.T
    row = lax.broadcasted_iota(jnp.int32, knt.shape, 0)
    for hd in range(MEM_HEADS):
        lo, hi = hd * MEM_HEAD_DIM, (hd + 1) * MEM_HEAD_DIM
        kt_ref[0, 0, hd] = jnp.where((row >= lo) & (row < hi), knt, 0.0).astype(BF16)
        vm_ref[0, 0, hd] = jnp.where((lane >= lo) & (lane < hi), v, 0.0).astype(BF16)


def _mem_prep(mem, g, w_kv, k_gain):
    nl = g.shape[0]
    bsz = mem.shape[0]
    out = jax.ShapeDtypeStruct((nl, bsz, MEM_HEADS, N_MEM, MEM_WIDTH), BF16)
    return pl.pallas_call(
        _mem_prep_kernel,
        grid=(nl, bsz),
        in_specs=[pl.BlockSpec((1, N_MEM, D_MODEL), lambda l, b: (b, 0, 0)),
                  pl.BlockSpec((1, 1, D_MODEL), lambda l, b: (l, 0, 0)),
                  pl.BlockSpec((1, D_MODEL, 2 * MEM_WIDTH), lambda l, b: (l, 0, 0)),
                  pl.BlockSpec((1, 1, MEM_WIDTH), lambda l, b: (l, 0, 0))],
        out_specs=[pl.BlockSpec((1, 1, MEM_HEADS, MEM_WIDTH, N_MEM), lambda l, b: (l, b, 0, 0, 0)),
                   pl.BlockSpec((1, 1, MEM_HEADS, N_MEM, MEM_WIDTH), lambda l, b: (l, b, 0, 0, 0))],
        out_shape=[out, out],
        compiler_params=_params("parallel", "parallel"),
        name="mem_prep",
    )(mem, g, w_kv, k_gain)


def _outproj_kernel(x_ref, ys_ref, om_ref, wglu_ref, bglu_ref, gs_ref, gm_ref, wout_ref,
                    gmq_ref, wmq_ref, qg_ref, kt_ref, vm_ref, wmo_ref, gmlp_ref,
                    x2_ref, h3_ref):
    nchunk = x_ref.shape[0] // SSM_CHUNK
    ys = jnp.concatenate([
        jnp.concatenate([ys_ref[t, c * CHUNK_PITCH:c * CHUNK_PITCH + SSM_CHUNK, :] for c in range(nchunk)], axis=0)
        for t in range(LANE_TILES)], axis=1)
    yg = ys * jax.nn.sigmoid(_dot(ys.astype(BF16), wglu_ref[...]) + bglu_ref[...])
    n1 = _rms(yg, gs_ref[...]).astype(BF16)
    n2 = _rms(om_ref[...].astype(F32), gm_ref[...]).astype(BF16)
    x1 = x_ref[...] + _dot(n1, wout_ref[:SSM_WIDTH, :]) + _dot(n2, wout_ref[SSM_WIDTH:, :])

    q = _dot(_rms(x1, gmq_ref[...]).astype(BF16), wmq_ref[...])
    lane = lax.broadcasted_iota(jnp.int32, q.shape, 1)
    ss = _head_sums(q * q, lane, MEM_HEAD_DIM)
    qn = (q * lax.rsqrt(ss * (1.0 / MEM_HEAD_DIM) + EPS) * qg_ref[...]).astype(BF16)
    o = jnp.zeros(q.shape, F32)
    for hd in range(MEM_HEADS):
        s = _dot(qn, kt_ref[hd])
        p = jnp.exp(s - jnp.max(s, axis=-1, keepdims=True))
        inv = 1.0 / jnp.sum(p, axis=-1, keepdims=True)
        o = o + _dot(p.astype(BF16), vm_ref[hd]) * inv
    x2 = x1 + _dot(o.astype(BF16), wmo_ref[...])
    x2_ref[...] = x2
    h3_ref[...] = _rms(x2, gmlp_ref[...]).astype(BF16)


def _outproj(l, x, ys, om, w, kt, vm, tm):
    bsz, seq, _ = x.shape
    tok = lambda width: pl.BlockSpec((None, tm, width), lambda b, i: (b, i, 0))
    memb = pl.BlockSpec((None, None, MEM_HEADS, N_MEM, MEM_WIDTH), lambda b, i: (l, b, 0, 0, 0))
    return pl.pallas_call(
        _outproj_kernel,
        grid=(bsz, seq // tm),
        in_specs=[tok(D_MODEL), pl.BlockSpec((LANE_TILES, None, _pitched(tm), LANES), lambda b, i: (0, b, i, 0)),
                  tok(MLA_WIDTH),
                  _layer_spec(l, SSM_WIDTH, SSM_WIDTH), _layer_spec(l, 1, SSM_WIDTH),
                  _layer_spec(l, 1, SSM_WIDTH), _layer_spec(l, 1, MLA_WIDTH),
                  _layer_spec(l, D_MODEL, D_MODEL), _layer_spec(l, 1, D_MODEL),
                  _layer_spec(l, D_MODEL, MEM_WIDTH), _layer_spec(l, 1, MEM_WIDTH),
                  memb, memb, _layer_spec(l, MEM_WIDTH, D_MODEL), _layer_spec(l, 1, D_MODEL)],
        out_specs=[tok(D_MODEL), tok(D_MODEL)],
        out_shape=[jax.ShapeDtypeStruct((bsz, seq, D_MODEL), F32),
                   jax.ShapeDtypeStruct((bsz, seq, D_MODEL), BF16)],
        compiler_params=_params("parallel", "parallel"),
        name="outproj",
    )(x, ys, om, w["w_glu"], w["b_glu"], w["g_ssm"], w["g_mla"], w["w_out"],
      w["g_memq"], w["w_mq"], w["mem_q_gain"], kt, vm, w["w_mo"], w["g_mlp"])


def _mlp_kernel(x_ref, h_ref, w1_ref, w2_ref, o_ref, *, ff_tile):
    h = h_ref[...]
    acc = x_ref[...]
    for c in range(D_FF // ff_tile):
        a = jnp.maximum(_dot(h, w1_ref[:, c * ff_tile:(c + 1) * ff_tile]), 0.0)
        acc = acc + _dot((a * a).astype(BF16), w2_ref[c * ff_tile:(c + 1) * ff_tile, :])
    o_ref[...] = acc


def _mlp(l, x, h, w1, w2, tm):
    t = x.shape[0]
    tok = pl.BlockSpec((tm, D_MODEL), lambda i: (i, 0))
    return pl.pallas_call(
        functools.partial(_mlp_kernel, ff_tile=1024),
        grid=(t // tm,),
        in_specs=[tok, tok, _layer_spec(l, D_MODEL, D_FF), _layer_spec(l, D_FF, D_MODEL)],
        out_specs=tok,
        out_shape=jax.ShapeDtypeStruct((t, D_MODEL), F32),
        compiler_params=_params("parallel"),
        name="mlp",
    )(x, h, w1, w2)


def _half_swap(w):
    h = w.shape[-1] // 2
    return jnp.concatenate([-w[..., h:], w[..., :h]], axis=-1)


def _half_swap_unsigned(w):
    h = w.shape[-1] // 2
    return jnp.concatenate([w[..., h:], w[..., :h]], axis=-1)


def _pad_last(w, before, after):
    pads = [(0, 0)] * (w.ndim - 1) + [(before, after)]
    return jnp.pad(w, pads)


def _trunk(tm, tq, x, mem, positions, norm_mix, w_in, ssm_lambda_re, ssm_lambda_im, ssm_log_step, ssm_b_re, ssm_b_im, ssm_c_re, ssm_c_im, ssm_d, ssm_w_glu, ssm_b_glu, mla_q_norm, mla_w_uq, mla_kv_norm, mla_w_ukv, mla_q_gain, mla_k_gain, out_norm_ssm, out_norm_mla, w_out, norm_mem_q, norm_mem_kv, mem_w_q, mem_w_kv, mem_q_gain, mem_k_gain, mem_w_o, norm_mlp, mlp_w1, mlp_w2):
    bsz, seq, _ = x.shape
    depth = norm_mix.shape[0]
    row = lambda a: a[:, None, :]
    tail = HEAD_PAD - QK_DIM

    s3 = SSM_WIDTH + Q_LORA + KV_LORA
    k_rope_w = w_in[..., s3:]
    w_in_x = jnp.concatenate([w_in[..., :s3], _pad_last(k_rope_w, QK_NOPE, tail),
                              _pad_last(_half_swap(k_rope_w), QK_NOPE, tail)], axis=-1).astype(BF16)
    wq = mla_w_uq.reshape(depth, Q_LORA, MLA_HEADS, QK_DIM)
    hw = MLA_HEADS * HEAD_PAD
    w_qa = _pad_last(wq, 0, tail).reshape(depth, Q_LORA, hw).astype(BF16)
    w_qb = _pad_last(_half_swap(wq[..., QK_NOPE:]), QK_NOPE, tail).reshape(depth, Q_LORA, hw).astype(BF16)
    wkv = mla_w_ukv.reshape(depth, KV_LORA, MLA_HEADS, QK_NOPE + V_DIM)
    w_k = _pad_last(wkv[..., :QK_NOPE], 0, HEAD_PAD - QK_NOPE).reshape(depth, KV_LORA, hw).astype(BF16)
    w_v = _pad_last(wkv[..., QK_NOPE:], 0, HEAD_PAD - V_DIM).reshape(depth, KV_LORA, hw).astype(BF16)
    q_rope_g, k_rope_g = mla_q_gain[:, QK_NOPE:], mla_k_gain[:, QK_NOPE:]
    mla_gains = jnp.stack([
        _pad_last(mla_q_gain, 0, tail),
        _pad_last(_half_swap_unsigned(q_rope_g), QK_NOPE, tail),
        _pad_last(mla_k_gain[:, :QK_NOPE], 0, HEAD_PAD - QK_NOPE),
        _pad_last(k_rope_g, QK_NOPE, tail),
        _pad_last(_half_swap_unsigned(k_rope_g), QK_NOPE, tail)], axis=1)
    mla_gains = jnp.pad(mla_gains, ((0, 0), (0, 3), (0, 0)))
    blk = jnp.arange(2 * HEAD_PAD) // HEAD_PAD
    bd = (blk[:, None] == blk[None, :]).astype(BF16)
    wmkv = mem_w_kv.reshape(depth, D_MODEL, MEM_HEADS, 2, MEM_HEAD_DIM)
    w_mkv = wmkv.transpose(0, 1, 3, 2, 4).reshape(depth, D_MODEL, 2 * MEM_WIDTH).astype(BF16)
    idx = jnp.arange(GROUPS_PER_TILE * LANES)
    dest = ((idx // SSM_GROUP) % GROUPS_PER_TILE) * LANES + (idx // LANES) * SSM_GROUP + idx % SSM_GROUP
    perm = (dest[:, None] == idx[None, :]).astype(BF16)

    w = dict(g_mix=row(norm_mix), w_in=w_in_x, g_q=row(mla_q_norm), w_qa=w_qa, w_qb=w_qb,
             g_kv=row(mla_kv_norm), w_k=w_k, w_v=w_v, mla_gains=mla_gains, bd=bd,
             w_glu=ssm_w_glu.astype(BF16), b_glu=row(ssm_b_glu), g_ssm=row(out_norm_ssm),
             g_mla=row(out_norm_mla), w_out=w_out.astype(BF16), g_memq=row(norm_mem_q),
             w_mq=mem_w_q.astype(BF16), mem_q_gain=row(jnp.tile(mem_q_gain, (1, MEM_HEADS))),
             w_mo=mem_w_o.astype(BF16), g_mlp=row(norm_mlp))
    w1_b = mlp_w1.astype(BF16)
    w2_b = mlp_w2.astype(BF16)

    ra, rb = _rope_tables(positions)
    prep = _s5_prep(ssm_lambda_re, ssm_lambda_im, ssm_log_step, ssm_b_re, ssm_b_im, ssm_c_re, ssm_c_im)
    kt_all, vm_all = _mem_prep(mem, row(norm_mem_kv), w_mkv, row(jnp.tile(mem_k_gain, (1, MEM_HEADS))))
    dx = jnp.tile(ssm_d.reshape(depth, SSM_GROUPS, 1, SSM_GROUP), (1, 1, 1, SSM_CHUNK))

    for l in range(depth):
        u, q, k, v = _inproj(l, x, w, ra, rb, min(2 * tm, seq))
        ys = _s5_mix(l, u, prep, dx, perm)
        om = _flash(q, k, v, tq, FLASH_HEADS)
        x2, h3 = _outproj(l, x, ys, om, w, kt_all, vm_all, min(2 * tm, seq))
        x = _mlp(l, x2.reshape(bsz * seq, D_MODEL), h3.reshape(bsz * seq, D_MODEL),
                 w1_b, w2_b, tm).reshape(bsz, seq, D_MODEL)
    return x


def kernel(x, mem, positions, norm_mix, w_in, ssm_lambda_re, ssm_lambda_im, ssm_log_step, ssm_b_re, ssm_b_im, ssm_c_re, ssm_c_im, ssm_d, ssm_w_glu, ssm_b_glu, mla_q_norm, mla_w_uq, mla_kv_norm, mla_w_ukv, mla_q_gain, mla_k_gain, out_norm_ssm, out_norm_mla, w_out, norm_mem_q, norm_mem_kv, mem_w_q, mem_w_kv, mem_q_gain, mem_k_gain, mem_w_o, norm_mlp, mlp_w1, mlp_w2):
    seq = x.shape[1]
    return _trunk(min(512, seq), min(512, seq // 2), x, mem, positions, norm_mix, w_in, ssm_lambda_re, ssm_lambda_im, ssm_log_step, ssm_b_re, ssm_b_im, ssm_c_re, ssm_c_im, ssm_d, ssm_w_glu, ssm_b_glu, mla_q_norm, mla_w_uq, mla_kv_norm, mla_w_ukv, mla_q_gain, mla_k_gain, out_norm_ssm, out_norm_mla, w_out, norm_mem_q, norm_mem_kv, mem_w_q, mem_w_kv, mem_q_gain, mem_k_gain, mem_w_o, norm_mlp, mlp_w1, mlp_w2)
```

```python
import functools
import math

import jax
import jax.numpy as jnp
from jax import lax
from jax.experimental import pallas as pl
from jax.experimental.pallas import tpu as pltpu

D_MODEL = 1024
N_MEM = 256
MEM_HEADS = 4
MEM_HEAD_DIM = 64
MEM_WIDTH = MEM_HEADS * MEM_HEAD_DIM
SSM_WIDTH = 512
MLA_WIDTH = 512
SSM_GROUP = 16
SSM_GROUPS = 32
SSM_STATE = 64
MLA_HEADS = 8
QK_NOPE = 64
QK_ROPE = 32
QK_DIM = QK_NOPE + QK_ROPE
V_DIM = 64
Q_LORA = 256
KV_LORA = 128
ROPE_THETA = 10000.0
D_FF = 4 * D_MODEL
EPS = 1e-6

LANES = 128
HEAD_PAD = 128
SSM_CHUNK = 32
CHUNK_W = SSM_CHUNK * SSM_GROUP
CHUNK_PITCH = SSM_CHUNK + 4
GROUPS_PER_TILE = LANES // SSM_GROUP
LANE_TILES = SSM_WIDTH // LANES
IN_COLS_PAD = SSM_WIDTH + Q_LORA + KV_LORA + 2 * LANES
FLASH_UNROLL = 4
FLASH_KEY_RATIO = 4
FLASH_HEADS = 8
VMEM_LIMIT = 56 * 1024 * 1024

F32 = jnp.float32
BF16 = jnp.bfloat16
HIGHEST = lax.Precision.HIGHEST


def _dot(a, b):
    return jnp.dot(a, b, preferred_element_type=F32)


def _dot_exact(a, b):
    return jnp.dot(a, b, precision=HIGHEST, preferred_element_type=F32)


def _select_cols(a, onehot):
    hi = a.astype(BF16)
    r1 = a - hi.astype(F32)
    mid = r1.astype(BF16)
    lo = (r1 - mid.astype(F32)).astype(BF16)
    return _dot(hi, onehot) + _dot(mid, onehot) + _dot(lo, onehot)


def _dot_nt(a, b):
    return lax.dot_general(a, b, (((1,), (1,)), ((), ())), preferred_element_type=F32)


def _rms(v, gain=None):
    y = v * lax.rsqrt(jnp.mean(v * v, axis=-1, keepdims=True) + EPS)
    return y if gain is None else y * gain


def _params(*sem, flags=None):
    return pltpu.CompilerParams(dimension_semantics=sem, vmem_limit_bytes=VMEM_LIMIT, flags=flags)


def _layer_spec(l, *shape):
    return pl.BlockSpec((None,) + shape, lambda *_: (l,) + (0,) * len(shape))


def _pitched(tokens):
    return tokens // SSM_CHUNK * CHUNK_PITCH


def _const_spec(*shape):
    return pl.BlockSpec(shape, lambda *_: (0,) * len(shape))


def _rope_table_kernel(pos_ref, freq_ref, a_ref, b_ref):
    ang = pos_ref[...].astype(F32) * freq_ref[...]
    lane = lax.broadcasted_iota(jnp.int32, ang.shape, 1)
    in_rope = (lane >= QK_NOPE) & (lane < QK_DIM)
    a_ref[...] = jnp.where(lane < QK_NOPE, 1.0, jnp.where(in_rope, jnp.cos(ang), 0.0))
    b_ref[...] = jnp.where(in_rope, jnp.sin(ang), 0.0)


def _rope_tables(positions):
    bsz, seq = positions.shape
    t = bsz * seq
    tm = min(1024, t)
    half = QK_ROPE // 2
    inv_freq = ROPE_THETA ** (-jnp.arange(half, dtype=F32) / half)
    freq = jnp.pad(jnp.tile(inv_freq, 2), (QK_NOPE, LANES - QK_DIM))[None, :]
    a, b = pl.pallas_call(
        _rope_table_kernel,
        grid=(t // tm,),
        in_specs=[pl.BlockSpec((tm, 1), lambda i: (i, 0)), _const_spec(1, LANES)],
        out_specs=[pl.BlockSpec((tm, LANES), lambda i: (i, 0))] * 2,
        out_shape=[jax.ShapeDtypeStruct((t, LANES), F32)] * 2,
        compiler_params=_params("parallel"),
        name="rope_tables",
    )(positions.reshape(t, 1), freq)
    return a.reshape(bsz, seq, LANES), b.reshape(bsz, seq, LANES)


def _inproj_kernel(x_ref, win_ref, wqa_ref, wqb_ref, wk_ref, wv_ref,
                   gains_ref, bd_ref, ra_ref, rb_ref, u_ref, q_ref, k_ref, v_ref):
    h = _rms(x_ref[...]).astype(BF16)
    proj = _dot(h, win_ref[...])
    pad_rows = jnp.zeros((CHUNK_PITCH - SSM_CHUNK, LANES), F32)
    for t in range(LANE_TILES):
        for c in range(proj.shape[0] // SSM_CHUNK):
            u_ref[t, c * CHUNK_PITCH:c * CHUNK_PITCH + SSM_CHUNK, :] = (
                proj[c * SSM_CHUNK:(c + 1) * SSM_CHUNK, t * LANES:(t + 1) * LANES])
            u_ref[t, c * CHUNK_PITCH + SSM_CHUNK:(c + 1) * CHUNK_PITCH, :] = pad_rows

    two = lambda a: jnp.concatenate([a, a], axis=1)
    ra, rb = ra_ref[...], rb_ref[...]
    gains = gains_ref[...]
    bd = bd_ref[...]
    scale = math.log2(math.e) / math.sqrt(QK_DIM)
    inv_d = 1.0 / QK_DIM

    o1 = SSM_WIDTH
    o2 = o1 + Q_LORA
    o3 = o2 + KV_LORA
    hq = _rms(proj[:, o1:o2]).astype(BF16)
    qa = _dot(hq, wqa_ref[...])
    qb = _dot(hq, wqb_ref[...])
    hkv = _rms(proj[:, o2:o3]).astype(BF16)
    kn = _dot(hkv, wk_ref[...])
    vm = _dot(hkv, wv_ref[...])

    kr = proj[:, o3:o3 + LANES]
    kr_sw = proj[:, o3 + LANES:o3 + 2 * LANES]
    kr_ss = two(_dot((kr * kr).astype(BF16), bd[:LANES, :LANES]))
    krot = two(kr * (gains[3:4] * ra) + kr_sw * (gains[4:5] * rb))
    ga, gb, gk = two(ra * gains[0:1]), two(rb * gains[1:2]), two(gains[2:3])
    lane = lax.broadcasted_iota(jnp.int32, ga.shape, 1)
    ones_col = jnp.where((lane & (LANES - 1)) == V_DIM, 1.0, 0.0)

    for hp in range(MLA_HEADS // 2):
        sl = slice(2 * hp * HEAD_PAD, (2 * hp + 2) * HEAD_PAD)
        q2 = qa[:, sl]
        rq = lax.rsqrt(_dot((q2 * q2).astype(BF16), bd) * inv_d + EPS) * scale
        qo = ((q2 * ga + qb[:, sl] * gb) * rq).astype(BF16)
        k2 = kn[:, sl]
        rk = lax.rsqrt((_dot((k2 * k2).astype(BF16), bd) + kr_ss) * inv_d + EPS)
        ko = ((k2 * gk + krot) * rk).astype(BF16)
        vo = (vm[:, sl] + ones_col).astype(BF16)
        for j in range(2):
            q_ref[2 * hp + j] = qo[:, j * HEAD_PAD:(j + 1) * HEAD_PAD]
            k_ref[2 * hp + j] = ko[:, j * HEAD_PAD:(j + 1) * HEAD_PAD]
            v_ref[2 * hp + j] = vo[:, j * HEAD_PAD:(j + 1) * HEAD_PAD]


def _inproj(l, x, w, ra, rb, tm):
    bsz, seq, _ = x.shape
    tok = lambda width: pl.BlockSpec((None, tm, width), lambda b, i: (b, i, 0))
    head = pl.BlockSpec((None, MLA_HEADS, tm, HEAD_PAD), lambda b, i: (b, 0, i, 0))
    hw = MLA_HEADS * HEAD_PAD
    head_shape = jax.ShapeDtypeStruct((bsz, MLA_HEADS, seq, HEAD_PAD), BF16)
    return pl.pallas_call(
        _inproj_kernel,
        grid=(bsz, seq // tm),
        in_specs=[tok(D_MODEL), _layer_spec(l, D_MODEL, IN_COLS_PAD),
                  _layer_spec(l, Q_LORA, hw), _layer_spec(l, Q_LORA, hw),
                  _layer_spec(l, KV_LORA, hw), _layer_spec(l, KV_LORA, hw),
                  _layer_spec(l, 8, HEAD_PAD), _const_spec(2 * HEAD_PAD, 2 * HEAD_PAD),
                  tok(LANES), tok(LANES)],
        out_specs=[pl.BlockSpec((LANE_TILES, None, _pitched(tm), LANES), lambda b, i: (0, b, i, 0)),
                   head, head, head],
        out_shape=[jax.ShapeDtypeStruct((LANE_TILES, bsz, _pitched(seq), LANES), F32),
                   head_shape, head_shape, head_shape],
        compiler_params=_params("parallel", "parallel"),
        name="inproj",
    )(x, w["w_in"], w["w_qa"], w["w_qb"], w["w_k"], w["w_v"], w["mla_gains"], w["bd"], ra, rb)


S5_PREP_GROUPS = 8


def _s5_prep_kernel(*refs):
    per_group_in, shared, outs = refs[:8], refs[8:10], refs[10:]
    for g in range(S5_PREP_GROUPS):
        _s5_prep_group(*[r.at[g] for r in per_group_in], *shared, *[r.at[g] for r in outs])


def _s5_prep_group(pcol_ref, prow_ref, b_re_ref, b_im_ref, bt_re_ref, bt_im_ref,
                   ct_re_ref, ct_im_ref, sel_ref, reps_ref,
                   m_ref, wst_ref, wo_ref, apa_ref, apb_ref):
    def zoh_coeff(a_re, a_im, lr, li):
        xr = a_re - 1.0
        den = lr * lr + li * li
        return (xr * lr + a_im * li) / den, (a_im * lr - xr * li) / den

    def zoh(lr, li, ls):
        step = jnp.exp(ls)
        mag = jnp.exp(lr * step)
        ang = li * step
        return zoh_coeff(mag * jnp.cos(ang), mag * jnp.sin(ang), lr, li) + (step,)

    pc = pcol_ref[...]
    lr, li = pc[:, 0:1], pc[:, 1:2]
    step = jnp.exp(pc[:, 2:3])
    n = lax.broadcasted_iota(jnp.int32, (SSM_STATE, LANES), 1).astype(F32)
    mag = jnp.exp(lr * step * n)
    th = li * step * n
    pw_re, pw_im = mag * jnp.cos(th), mag * jnp.sin(th)
    cfr, cfi = zoh_coeff(pw_re[:, 1:2], pw_im[:, 1:2], lr, li)
    b_re, b_im = b_re_ref[...], b_im_ref[...]
    bb_re = cfr * b_re - cfi * b_im
    bb_im = cfr * b_im + cfi * b_re

    pr = prow_ref[...]
    lr_row, li_row, ls_row = pr[0:1], pr[1:2], pr[2:3]
    cfr_row, cfi_row, step_row = zoh(lr_row, li_row, ls_row)
    bt_re, bt_im = bt_re_ref[...], bt_im_ref[...]
    bbt_re = cfr_row[:, :SSM_STATE] * bt_re - cfi_row[:, :SSM_STATE] * bt_im
    bbt_im = cfr_row[:, :SSM_STATE] * bt_im + cfi_row[:, :SSM_STATE] * bt_re

    sel = sel_ref[...]
    brep_re, brep_im = _select_cols(bb_re, sel), _select_cols(bb_im, sel)
    crep_re, crep_im = _select_cols(ct_re_ref[...], sel), _select_cols(ct_im_ref[...], sel)

    apow = lambda i: (_select_cols(pw_re, reps_ref[i]), _select_cols(pw_im, reps_ref[i]))

    e_re, e_im = apow(0)
    g_re = e_re * crep_re - e_im * crep_im
    g_im = e_re * crep_im + e_im * crep_re
    kt = _dot_exact(bbt_re, g_re) - _dot_exact(bbt_im, g_im)
    klane = lax.broadcasted_iota(jnp.int32, kt.shape, 1)
    m_ref[0:SSM_GROUP, :] = kt.astype(BF16)
    for s in range(1, SSM_CHUNK):
        shifted = jnp.where(klane >= s * SSM_GROUP, pltpu.roll(kt, s * SSM_GROUP, axis=1), 0.0)
        m_ref[s * SSM_GROUP:(s + 1) * SSM_GROUP, :] = shifted.astype(BF16)

    e_re, e_im = apow(1)
    wst_ref[0:SSM_STATE, :] = (e_re * brep_re - e_im * brep_im).astype(BF16)
    wst_ref[SSM_STATE:, :] = (e_re * brep_im + e_im * brep_re).astype(BF16)

    e_re, e_im = apow(2)
    wo_ref[0:SSM_STATE, :] = (e_re * crep_re - e_im * crep_im).astype(BF16)
    wo_ref[SSM_STATE:, :] = (-(e_re * crep_im + e_im * crep_re)).astype(BF16)

    krow = lax.broadcasted_iota(jnp.int32, (8, LANES), 0)
    klan = lax.broadcasted_iota(jnp.int32, (8, LANES), 1)
    nn = (SSM_CHUNK * jnp.left_shift(1, krow)).astype(F32)
    mag = jnp.exp(lr_row * step_row * nn)
    th = li_row * step_row * nn
    p_re, p_im = mag * jnp.cos(th), mag * jnp.sin(th)
    apa_ref[...] = p_re
    apb_ref[...] = jnp.where(klan < SSM_STATE, -p_im, p_im)


def _s5_prep(lam_re, lam_im, log_step, b_re, b_im, c_re, c_im):
    nl, g, p = lam_re.shape
    hh = SSM_GROUP
    ls = jnp.broadcast_to(log_step[..., None], (nl, g, p))
    params = [lam_re, lam_im, ls] + [jnp.zeros_like(ls)] * 5
    pcol = jnp.stack(params, axis=-1)
    prow = jnp.stack([jnp.concatenate([v, v], axis=-1) for v in params], axis=2)
    sel = jnp.tile(jnp.eye(hh, dtype=BF16), (1, SSM_CHUNK))
    tau = jnp.arange(CHUNK_W) // hh
    nrow = jnp.arange(LANES)[:, None]
    reps = jnp.stack([nrow == tau, nrow == SSM_CHUNK - 1 - tau, nrow == tau + 1]).astype(BF16)
    blk = lambda *s: pl.BlockSpec((None, S5_PREP_GROUPS) + s, lambda l, i: (l, i) + (0,) * len(s))
    return pl.pallas_call(
        _s5_prep_kernel,
        grid=(nl, g // S5_PREP_GROUPS),
        in_specs=[blk(p, 8), blk(8, LANES), blk(p, hh), blk(p, hh), blk(hh, p), blk(hh, p),
                  blk(p, hh), blk(p, hh), _const_spec(hh, CHUNK_W), _const_spec(3, LANES, CHUNK_W)],
        out_specs=[blk(CHUNK_W, CHUNK_W), blk(2 * p, CHUNK_W), blk(2 * p, CHUNK_W),
                   blk(8, LANES), blk(8, LANES)],
        out_shape=[jax.ShapeDtypeStruct((nl, g, CHUNK_W, CHUNK_W), BF16),
                   jax.ShapeDtypeStruct((nl, g, 2 * p, CHUNK_W), BF16),
                   jax.ShapeDtypeStruct((nl, g, 2 * p, CHUNK_W), BF16),
                   jax.ShapeDtypeStruct((nl, g, 8, LANES), F32),
                   jax.ShapeDtypeStruct((nl, g, 8, LANES), F32)],
        compiler_params=_params("parallel", "parallel"),
        name="s5_prep",
    )(pcol, prow, b_re, b_im, jnp.swapaxes(b_re, -1, -2), jnp.swapaxes(b_im, -1, -2),
      jnp.swapaxes(c_re, -1, -2), jnp.swapaxes(c_im, -1, -2), sel, reps)


def _s5_mix_kernel(u_ref, perm_ref, m_ref, wst_ref, wo_ref, apa_ref, apb_ref, dx_ref, y_ref,
                   *, nsteps):
    nb, rows, _ = u_ref.shape
    cps = rows // CHUNK_PITCH
    nc = nb * cps
    gpt = GROUPS_PER_TILE
    nslab = SSM_CHUNK // gpt
    perm = perm_ref[...]

    def slab(j):
        return jnp.concatenate([
            jnp.concatenate([u_ref[b, pl.ds(gpt * j + s, cps, stride=CHUNK_PITCH), :] for s in range(gpt)], axis=1)
            for b in range(nb)], axis=0)

    z = _dot(jnp.concatenate([slab(j) for j in range(nslab)], axis=0).astype(BF16), perm).astype(BF16)

    c = lax.broadcasted_iota(jnp.int32, (nc, 2 * SSM_STATE), 0) & (cps - 1)
    ys = []
    for g in range(gpt):
        x = jnp.concatenate([z[j * nc:(j + 1) * nc, g * LANES:(g + 1) * LANES] for j in range(nslab)],
                            axis=1)
        y = _dot(x, m_ref[g])
        st = _dot_nt(x, wst_ref[g])
        apa, apb = apa_ref[g], apb_ref[g]
        for k in range(nsteps):
            d = 1 << k
            sh = jnp.where(c >= d, pltpu.roll(st, d, axis=0), 0.0)
            st = st + sh * apa[k:k + 1] + pltpu.roll(sh, SSM_STATE, axis=1) * apb[k:k + 1]
        carried = jnp.where(c >= 1, pltpu.roll(st, 1, axis=0), 0.0)
        y = y + _dot(carried.astype(BF16), wo_ref[g]) + dx_ref[g] * x.astype(F32)
        ys.append(jax.nn.gelu(y).astype(BF16))

    w = jnp.concatenate([
        jnp.concatenate([ys[g][:, j * LANES:(j + 1) * LANES] for g in range(gpt)], axis=1)
        for j in range(nslab)], axis=0)
    o = _dot(w, perm)
    for b in range(nb):
        for j in range(nslab):
            for t in range(gpt):
                y_ref[b, pl.ds(gpt * j + t, cps, stride=CHUNK_PITCH), :] = (
                    o[j * nc + b * cps:j * nc + (b + 1) * cps, t * LANES:(t + 1) * LANES])
        for r in range(SSM_CHUNK, CHUNK_PITCH):
            y_ref[b, pl.ds(r, cps, stride=CHUNK_PITCH), :] = jnp.zeros((cps, LANES), F32)


def _s5_mix(l, u, prep, dx, perm):
    tiles, bsz, rows, _ = u.shape
    cps = rows // CHUNK_PITCH
    assert cps & (cps - 1) == 0 and cps <= 256
    gpt = GROUPS_PER_TILE
    nb = 2 if bsz % 2 == 0 else 1
    m, wst, wo, apa, apb = prep
    tokens = pl.BlockSpec((None, nb, rows, LANES), lambda t, b: (t, b, 0, 0))
    grp = lambda *s: pl.BlockSpec((None, gpt) + s, lambda t, b: (l, t) + (0,) * len(s))
    return pl.pallas_call(
        functools.partial(_s5_mix_kernel, nsteps=cps.bit_length() - 1),
        grid=(tiles, bsz // nb),
        in_specs=[tokens, _const_spec(gpt * LANES, gpt * LANES), grp(CHUNK_W, CHUNK_W),
                  grp(2 * SSM_STATE, CHUNK_W), grp(2 * SSM_STATE, CHUNK_W), grp(8, LANES), grp(8, LANES),
                  grp(1, CHUNK_W)],
        out_specs=tokens,
        out_shape=jax.ShapeDtypeStruct(u.shape, F32),
        compiler_params=_params("parallel", "parallel"),
        name="s5_mix",
    )(u, perm, m, wst, wo, apa, apb, dx)


FULL = 0


def _flash_kernel(qt_ref, kt_ref, kind_ref, q_ref, k_ref, v_ref, o_ref, m_scr, acc_scr, *, heads):
    p_idx = pl.program_id(2)
    ki = kt_ref[p_idx]
    kind = kind_ref[p_idx]
    tq, tk = q_ref.shape[2], k_ref.shape[2]

    @pl.when(ki == 0)
    def _():
        m_scr[...] = jnp.full(m_scr.shape, -jnp.inf, F32)
        acc_scr[...] = jnp.zeros(acc_scr.shape, F32)

    def attend(j, nk, diag_offset):
        s = _dot_nt(q_ref[0, j], k_ref[0, j, 0:nk, :])
        if diag_offset is not None:
            row = lax.broadcasted_iota(jnp.int32, s.shape, 0)
            col = lax.broadcasted_iota(jnp.int32, s.shape, 1)
            s = jnp.where(col <= row + diag_offset, s, -jnp.inf)
        m_prev = m_scr[j]
        m_new = jnp.maximum(m_prev, jnp.max(s, axis=-1, keepdims=True))
        alpha = jnp.exp2(m_prev - m_new)
        p = jnp.exp2(s - jnp.tile(m_new, (1, nk // LANES)))
        acc_scr[j] = alpha * acc_scr[j] + _dot(p.astype(BF16), v_ref[0, j, 0:nk, :])
        m_scr[j] = m_new

    def all_heads(nk, diag_offset):
        def group(g, carry):
            for j in range(FLASH_UNROLL):
                attend(g * FLASH_UNROLL + j, nk, diag_offset)
            return carry
        lax.fori_loop(0, heads // FLASH_UNROLL, group, 0)

    @pl.when(kind == FULL)
    def _():
        all_heads(tk, None)

    for r in range(1, tk // tq + 1):
        @pl.when(kind == r)
        def _(r=r):
            all_heads(r * tq, (r - 1) * tq)

    @pl.when(kind != FULL)
    def _():
        outs = []
        for j in range(heads):
            acc = acc_scr[j]
            outs.append(acc[:, :V_DIM] / acc[:, V_DIM:V_DIM + 1])
        o_ref[0] = jnp.concatenate(outs, axis=-1).astype(BF16)


def _flash(q, k, v, tq, heads_per_step):
    bsz, heads, seq, _ = q.shape
    hp = heads_per_step
    ratio = FLASH_KEY_RATIO
    tk = ratio * tq
    assert seq % tk == 0
    steps = []
    for i in range(seq // tq):
        steps += [(i, j, FULL) for j in range(i // ratio)] + [(i, i // ratio, i % ratio + 1)]
    qt, kt, kind = (jnp.asarray(col, jnp.int32) for col in zip(*steps))
    qspec = pl.BlockSpec((1, hp, tq, HEAD_PAD), lambda b, h, p, qt, kt, kind: (b, h, qt[p], 0))
    kspec = pl.BlockSpec((1, hp, tk, HEAD_PAD), lambda b, h, p, qt, kt, kind: (b, h, kt[p], 0))
    grid_spec = pltpu.PrefetchScalarGridSpec(
        num_scalar_prefetch=3,
        grid=(bsz, heads // hp, len(steps)),
        in_specs=[qspec, kspec, kspec],
        out_specs=pl.BlockSpec((1, tq, hp * V_DIM), lambda b, h, p, qt, kt, kind: (b, qt[p], h)),
        scratch_shapes=[pltpu.VMEM((hp, tq, LANES), F32), pltpu.VMEM((hp, tq, HEAD_PAD), F32)])
    return pl.pallas_call(
        functools.partial(_flash_kernel, heads=hp),
        grid_spec=grid_spec,
        out_shape=jax.ShapeDtypeStruct((bsz, seq, heads * V_DIM), BF16),
        compiler_params=_params("parallel", "parallel", "arbitrary"),
        name="flash",
    )(qt, kt, kind, q, k, v)


def _head_sums(v2, lane, width):
    out = jnp.zeros_like(v2)
    for hd in range(v2.shape[-1] // width):
        msk = (lane >= hd * width) & (lane < (hd + 1) * width)
        out = out + jnp.where(msk, jnp.sum(jnp.where(msk, v2, 0.0), axis=-1, keepdims=True), 0.0)
    return out


def _mem_prep_kernel(mem_ref, g_ref, w_ref, kg_ref, kt_ref, vm_ref):
    hm = _rms(mem_ref[0], g_ref[0]).astype(BF16)
    kv = _dot(hm, w_ref[0])
    k, v = kv[:, :MEM_WIDTH], kv[:, MEM_WIDTH:]
    lane = lax.broadcasted_iota(jnp.int32, k.shape, 1)
    ss = _head_sums(k * k, lane, MEM_HEAD_DIM)
    kn = k * lax.rsqrt(ss * (1.0 / MEM_HEAD_DIM) + EPS) * kg_ref[0] * (math.log2(math.e) / math.sqrt(MEM_HEAD_DIM))
    knt = kn.T
    row = lax.broadcasted_iota(jnp.int32, knt.shape, 0)
    for hd in range(MEM_HEADS):
        lo, hi = hd * MEM_HEAD_DIM, (hd + 1) * MEM_HEAD_DIM
        kt_ref[0, 0, hd] = jnp.where((row >= lo) & (row < hi), knt, 0.0).astype(BF16)
        vm_ref[0, 0, hd] = jnp.where((lane >= lo) & (lane < hi), v, 0.0).astype(BF16)


def _mem_prep(mem, g, w_kv, k_gain):
    nl = g.shape[0]
    bsz = mem.shape[0]
    out = jax.ShapeDtypeStruct((nl, bsz, MEM_HEADS, N_MEM, MEM_WIDTH), BF16)
    return pl.pallas_call(
        _mem_prep_kernel,
        grid=(nl, bsz),
        in_specs=[pl.BlockSpec((1, N_MEM, D_MODEL), lambda l, b: (b, 0, 0)),
                  pl.BlockSpec((1, 1, D_MODEL), lambda l, b: (l, 0, 0)),
                  pl.BlockSpec((1, D_MODEL, 2 * MEM_WIDTH), lambda l, b: (l, 0, 0)),
                  pl.BlockSpec((1, 1, MEM_WIDTH), lambda l, b: (l, 0, 0))],
        out_specs=[pl.BlockSpec((1, 1, MEM_HEADS, MEM_WIDTH, N_MEM), lambda l, b: (l, b, 0, 0, 0)),
                   pl.BlockSpec((1, 1, MEM_HEADS, N_MEM, MEM_WIDTH), lambda l, b: (l, b, 0, 0, 0))],
        out_shape=[out, out],
        compiler_params=_params("parallel", "parallel"),
        name="mem_prep",
    )(mem, g, w_kv, k_gain)


def _outproj_kernel(x_ref, ys_ref, om_ref, wglu_ref, bglu_ref, wout_ref,
                    wmq_ref, qg_ref, bdm_ref, kt_ref, vm_ref, wmo_ref,
                    x2_ref, h3_ref):
    nchunk = x_ref.shape[0] // SSM_CHUNK
    ys = jnp.concatenate([
        jnp.concatenate([ys_ref[t, c * CHUNK_PITCH:c * CHUNK_PITCH + SSM_CHUNK, :] for c in range(nchunk)], axis=0)
        for t in range(LANE_TILES)], axis=1)
    yg = ys * jax.nn.sigmoid(_dot(ys.astype(BF16), wglu_ref[...]) + bglu_ref[...])
    n1 = _rms(yg).astype(BF16)
    n2 = _rms(om_ref[...].astype(F32)).astype(BF16)
    x1 = x_ref[...] + _dot(n1, wout_ref[:SSM_WIDTH, :]) + _dot(n2, wout_ref[SSM_WIDTH:, :])

    r1 = lax.rsqrt(jnp.mean(x1 * x1, axis=-1, keepdims=True) + EPS)
    q = _dot(x1.astype(BF16), wmq_ref[...]) * r1
    ss = _dot((q * q).astype(BF16), bdm_ref[...])
    qn = (q * lax.rsqrt(ss * (1.0 / MEM_HEAD_DIM) + EPS) * qg_ref[...]).astype(BF16)
    o = jnp.zeros(q.shape, F32)
    for hd in range(MEM_HEADS):
        s = _dot(qn, kt_ref[hd])
        p = jnp.exp2(s - jnp.max(s, axis=-1, keepdims=True))
        inv = 1.0 / jnp.sum(p, axis=-1, keepdims=True)
        o = o + _dot(p.astype(BF16), vm_ref[hd]) * inv
    x2 = x1 + _dot(o.astype(BF16), wmo_ref[...])
    x2_ref[...] = x2
    h3_ref[...] = _rms(x2).astype(BF16)


def _outproj(l, x, ys, om, w, kt, vm, tm):
    bsz, seq, _ = x.shape
    tok = lambda width: pl.BlockSpec((None, tm, width), lambda b, i: (b, i, 0))
    memb = pl.BlockSpec((None, None, MEM_HEADS, N_MEM, MEM_WIDTH), lambda b, i: (l, b, 0, 0, 0))
    return pl.pallas_call(
        _outproj_kernel,
        grid=(bsz, seq // tm),
        in_specs=[tok(D_MODEL), pl.BlockSpec((LANE_TILES, None, _pitched(tm), LANES), lambda b, i: (0, b, i, 0)),
                  tok(MLA_WIDTH),
                  _layer_spec(l, SSM_WIDTH, SSM_WIDTH), _layer_spec(l, 1, SSM_WIDTH),
                  _layer_spec(l, D_MODEL, D_MODEL),
                  _layer_spec(l, D_MODEL, MEM_WIDTH), _layer_spec(l, 1, MEM_WIDTH),
                  _const_spec(MEM_WIDTH, MEM_WIDTH),
                  memb, memb, _layer_spec(l, MEM_WIDTH, D_MODEL)],
        out_specs=[tok(D_MODEL), tok(D_MODEL)],
        out_shape=[jax.ShapeDtypeStruct((bsz, seq, D_MODEL), F32),
                   jax.ShapeDtypeStruct((bsz, seq, D_MODEL), BF16)],
        compiler_params=_params("parallel", "parallel"),
        name="outproj",
    )(x, ys, om, w["w_glu"], w["b_glu"], w["w_out"],
      w["w_mq"], w["mem_q_gain"], w["bd_mem"], kt, vm, w["w_mo"])


def _mlp_kernel(x_ref, h_ref, w1_ref, w2_ref, o_ref, *, ff_tile):
    h = h_ref[...]
    acc = x_ref[...]
    for c in range(D_FF // ff_tile):
        a = jnp.maximum(_dot(h, w1_ref[:, c * ff_tile:(c + 1) * ff_tile]), 0.0)
        acc = acc + _dot((a * a).astype(BF16), w2_ref[c * ff_tile:(c + 1) * ff_tile, :])
    o_ref[...] = acc


def _mlp(l, x, h, w1, w2, tm):
    t = x.shape[0]
    tok = pl.BlockSpec((tm, D_MODEL), lambda i: (i, 0))
    return pl.pallas_call(
        functools.partial(_mlp_kernel, ff_tile=1024),
        grid=(t // tm,),
        in_specs=[tok, tok,
                  pl.BlockSpec((None, D_MODEL, D_FF), lambda i: (l, 0, 0), pipeline_mode=pl.Buffered(1)),
                  pl.BlockSpec((None, D_FF, D_MODEL), lambda i: (l, 0, 0), pipeline_mode=pl.Buffered(1))],
        out_specs=tok,
        out_shape=jax.ShapeDtypeStruct((t, D_MODEL), F32),
        compiler_params=_params("parallel"),
        name="mlp",
    )(x, h, w1, w2)


def _half_swap(w):
    h = w.shape[-1] // 2
    return jnp.concatenate([-w[..., h:], w[..., :h]], axis=-1)


def _half_swap_unsigned(w):
    h = w.shape[-1] // 2
    return jnp.concatenate([w[..., h:], w[..., :h]], axis=-1)


def _pad_last(w, before, after):
    pads = [(0, 0)] * (w.ndim - 1) + [(before, after)]
    return jnp.pad(w, pads)


def _trunk(tm, tq, x, mem, positions, norm_mix, w_in, ssm_lambda_re, ssm_lambda_im, ssm_log_step, ssm_b_re, ssm_b_im, ssm_c_re, ssm_c_im, ssm_d, ssm_w_glu, ssm_b_glu, mla_q_norm, mla_w_uq, mla_kv_norm, mla_w_ukv, mla_q_gain, mla_k_gain, out_norm_ssm, out_norm_mla, w_out, norm_mem_q, norm_mem_kv, mem_w_q, mem_w_kv, mem_q_gain, mem_k_gain, mem_w_o, norm_mlp, mlp_w1, mlp_w2):
    bsz, seq, _ = x.shape
    depth = norm_mix.shape[0]
    row = lambda a: a[:, None, :]
    tail = HEAD_PAD - QK_DIM

    s3 = SSM_WIDTH + Q_LORA + KV_LORA
    k_rope_w = w_in[..., s3:]
    rows = lambda g: g[:, :, None]
    w_in_x = jnp.concatenate([w_in[..., :s3], _pad_last(k_rope_w, QK_NOPE, tail),
                              _pad_last(_half_swap(k_rope_w), QK_NOPE, tail)], axis=-1)
    w_in_x = (rows(norm_mix) * w_in_x).astype(BF16)
    wq = mla_w_uq.reshape(depth, Q_LORA, MLA_HEADS, QK_DIM)
    hw = MLA_HEADS * HEAD_PAD
    gq = rows(mla_q_norm)
    w_qa = (gq * _pad_last(wq, 0, tail).reshape(depth, Q_LORA, hw)).astype(BF16)
    w_qb = (gq * _pad_last(_half_swap(wq[..., QK_NOPE:]), QK_NOPE, tail).reshape(depth, Q_LORA, hw)).astype(BF16)
    wkv = mla_w_ukv.reshape(depth, KV_LORA, MLA_HEADS, QK_NOPE + V_DIM)
    gkv = rows(mla_kv_norm)
    w_k = (gkv * _pad_last(wkv[..., :QK_NOPE], 0, HEAD_PAD - QK_NOPE).reshape(depth, KV_LORA, hw)).astype(BF16)
    w_v = (gkv * _pad_last(wkv[..., QK_NOPE:], 0, HEAD_PAD - V_DIM).reshape(depth, KV_LORA, hw)).astype(BF16)
    q_rope_g, k_rope_g = mla_q_gain[:, QK_NOPE:], mla_k_gain[:, QK_NOPE:]
    mla_gains = jnp.stack([
        _pad_last(mla_q_gain, 0, tail),
        _pad_last(_half_swap_unsigned(q_rope_g), QK_NOPE, tail),
        _pad_last(mla_k_gain[:, :QK_NOPE], 0, HEAD_PAD - QK_NOPE),
        _pad_last(k_rope_g, QK_NOPE, tail),
        _pad_last(_half_swap_unsigned(k_rope_g), QK_NOPE, tail)], axis=1)
    mla_gains = jnp.pad(mla_gains, ((0, 0), (0, 3), (0, 0)))
    blk = jnp.arange(2 * HEAD_PAD) // HEAD_PAD
    bd = (blk[:, None] == blk[None, :]).astype(BF16)
    mblk = jnp.arange(MEM_WIDTH) // MEM_HEAD_DIM
    bd_mem = (mblk[:, None] == mblk[None, :]).astype(BF16)
    wmkv = mem_w_kv.reshape(depth, D_MODEL, MEM_HEADS, 2, MEM_HEAD_DIM)
    w_mkv = wmkv.transpose(0, 1, 3, 2, 4).reshape(depth, D_MODEL, 2 * MEM_WIDTH).astype(BF16)
    idx = jnp.arange(GROUPS_PER_TILE * LANES)
    dest = ((idx // SSM_GROUP) % GROUPS_PER_TILE) * LANES + (idx // LANES) * SSM_GROUP + idx % SSM_GROUP
    perm = (dest[:, None] == idx[None, :]).astype(BF16)

    g_out = jnp.concatenate([out_norm_ssm, out_norm_mla], axis=-1)
    w = dict(w_in=w_in_x, w_qa=w_qa, w_qb=w_qb, w_k=w_k, w_v=w_v, mla_gains=mla_gains, bd=bd, bd_mem=bd_mem,
             w_glu=ssm_w_glu.astype(BF16), b_glu=row(ssm_b_glu),
             w_out=(rows(g_out) * w_out).astype(BF16),
             w_mq=(rows(norm_mem_q) * mem_w_q).astype(BF16),
             mem_q_gain=row(jnp.tile(mem_q_gain, (1, MEM_HEADS))),
             w_mo=mem_w_o.astype(BF16))
    w1_b = (rows(norm_mlp) * mlp_w1).astype(BF16)
    w2_b = mlp_w2.astype(BF16)

    ra, rb = _rope_tables(positions)
    prep = _s5_prep(ssm_lambda_re, ssm_lambda_im, ssm_log_step, ssm_b_re, ssm_b_im, ssm_c_re, ssm_c_im)
    kt_all, vm_all = _mem_prep(mem, row(norm_mem_kv), w_mkv, row(jnp.tile(mem_k_gain, (1, MEM_HEADS))))
    dx = jnp.tile(ssm_d.reshape(depth, SSM_GROUPS, 1, SSM_GROUP), (1, 1, 1, SSM_CHUNK))

    for l in range(depth):
        u, q, k, v = _inproj(l, x, w, ra, rb, min(2 * tm, seq))
        ys = _s5_mix(l, u, prep, dx, perm)
        om = _flash(q, k, v, tq, FLASH_HEADS)
        x2, h3 = _outproj(l, x, ys, om, w, kt_all, vm_all, min(2 * tm, seq))
        x = _mlp(l, x2.reshape(bsz * seq, D_MODEL), h3.reshape(bsz * seq, D_MODEL),
                 w1_b, w2_b, min(2 * tm, seq)).reshape(bsz, seq, D_MODEL)
    return x


def kernel(x, mem, positions, norm_mix, w_in, ssm_lambda_re, ssm_lambda_im, ssm_log_step, ssm_b_re, ssm_b_im, ssm_c_re, ssm_c_im, ssm_d, ssm_w_glu, ssm_b_glu, mla_q_norm, mla_w_uq, mla_kv_norm, mla_w_ukv, mla_q_gain, mla_k_gain, out_norm_ssm, out_norm_mla, w_out, norm_mem_q, norm_mem_kv, mem_w_q, mem_w_kv, mem_q_gain, mem_k_gain, mem_w_o, norm_mlp, mlp_w1, mlp_w2):
    seq = x.shape[1]
    return _trunk(min(512, seq), min(512, seq // 2), x, mem, positions, norm_mix, w_in, ssm_lambda_re, ssm_lambda_im, ssm_log_step, ssm_b_re, ssm_b_im, ssm_c_re, ssm_c_im, ssm_d, ssm_w_glu, ssm_b_glu, mla_q_norm, mla_w_uq, mla_kv_norm, mla_w_ukv, mla_q_gain, mla_k_gain, out_norm_ssm, out_norm_mla, w_out, norm_mem_q, norm_mem_kv, mem_w_q, mem_w_kv, mem_q_gain, mem_k_gain, mem_w_o, norm_mlp, mlp_w1, mlp_w2)
```

```python
import functools
import math

import jax
import jax.numpy as jnp
from jax import lax
from jax.experimental import pallas as pl
from jax.experimental.pallas import tpu as pltpu

D_MODEL = 1024
N_MEM = 256
MEM_HEADS = 4
MEM_HEAD_DIM = 64
MEM_WIDTH = MEM_HEADS * MEM_HEAD_DIM
SSM_WIDTH = 512
MLA_WIDTH = 512
SSM_GROUP = 16
SSM_GROUPS = 32
SSM_STATE = 64
MLA_HEADS = 8
QK_NOPE = 64
QK_ROPE = 32
QK_DIM = QK_NOPE + QK_ROPE
V_DIM = 64
Q_LORA = 256
KV_LORA = 128
ROPE_THETA = 10000.0
D_FF = 4 * D_MODEL
EPS = 1e-6

LANES = 128
HEAD_PAD = 128
SSM_CHUNK = 32
CHUNK_W = SSM_CHUNK * SSM_GROUP
CHUNK_PITCH = SSM_CHUNK + 4
GROUPS_PER_TILE = LANES // SSM_GROUP
LANE_TILES = SSM_WIDTH // LANES
IN_COLS_PAD = SSM_WIDTH + Q_LORA + KV_LORA + 2 * LANES
FLASH_UNROLL = 4
FLASH_KEY_RATIO = 4
FLASH_HEADS = 8
VMEM_LIMIT = 56 * 1024 * 1024

F32 = jnp.float32
BF16 = jnp.bfloat16
HIGHEST = lax.Precision.HIGHEST


def _dot(a, b):
    return jnp.dot(a, b, preferred_element_type=F32)


def _dot_exact(a, b):
    return jnp.dot(a, b, precision=HIGHEST, preferred_element_type=F32)


def _select_cols(a, onehot):
    hi = a.astype(BF16)
    lo = (a - hi.astype(F32)).astype(BF16)
    return _dot(hi, onehot) + _dot(lo, onehot)


def _dot_nt(a, b):
    return lax.dot_general(a, b, (((1,), (1,)), ((), ())), preferred_element_type=F32)


def _rms(v, gain=None):
    y = v * lax.rsqrt(jnp.mean(v * v, axis=-1, keepdims=True) + EPS)
    return y if gain is None else y * gain


def _params(*sem, flags=None):
    return pltpu.CompilerParams(dimension_semantics=sem, vmem_limit_bytes=VMEM_LIMIT, flags=flags)


def _layer_spec(l, *shape):
    return pl.BlockSpec((None,) + shape, lambda *_: (l,) + (0,) * len(shape))


def _pitched(tokens):
    return tokens // SSM_CHUNK * CHUNK_PITCH


def _const_spec(*shape):
    return pl.BlockSpec(shape, lambda *_: (0,) * len(shape))


def _rope_table_kernel(pos_ref, freq_ref, a_ref, b_ref):
    ang = pos_ref[...].astype(F32) * freq_ref[...]
    lane = lax.broadcasted_iota(jnp.int32, ang.shape, 1)
    in_rope = (lane >= QK_NOPE) & (lane < QK_DIM)
    a_ref[...] = jnp.where(lane < QK_NOPE, 1.0, jnp.where(in_rope, jnp.cos(ang), 0.0))
    b_ref[...] = jnp.where(in_rope, jnp.sin(ang), 0.0)


def _rope_tables(positions):
    bsz, seq = positions.shape
    t = bsz * seq
    tm = min(1024, t)
    half = QK_ROPE // 2
    inv_freq = ROPE_THETA ** (-jnp.arange(half, dtype=F32) / half)
    freq = jnp.pad(jnp.tile(inv_freq, 2), (QK_NOPE, LANES - QK_DIM))[None, :]
    a, b = pl.pallas_call(
        _rope_table_kernel,
        grid=(t // tm,),
        in_specs=[pl.BlockSpec((tm, 1), lambda i: (i, 0)), _const_spec(1, LANES)],
        out_specs=[pl.BlockSpec((tm, LANES), lambda i: (i, 0))] * 2,
        out_shape=[jax.ShapeDtypeStruct((t, LANES), F32)] * 2,
        compiler_params=_params("parallel"),
        name="rope_tables",
    )(positions.reshape(t, 1), freq)
    return a.reshape(bsz, seq, LANES), b.reshape(bsz, seq, LANES)


def _inproj_kernel(x_ref, win_ref, wqa_ref, wqb_ref, wk_ref, wv_ref,
                   gains_ref, bd_ref, ra_ref, rb_ref, u_ref, q_ref, k_ref, v_ref):
    h = _rms(x_ref[...]).astype(BF16)
    proj = _dot(h, win_ref[...])
    pad_rows = jnp.zeros((CHUNK_PITCH - SSM_CHUNK, LANES), F32)
    for t in range(LANE_TILES):
        for c in range(proj.shape[0] // SSM_CHUNK):
            u_ref[t, c * CHUNK_PITCH:c * CHUNK_PITCH + SSM_CHUNK, :] = (
                proj[c * SSM_CHUNK:(c + 1) * SSM_CHUNK, t * LANES:(t + 1) * LANES])
            u_ref[t, c * CHUNK_PITCH + SSM_CHUNK:(c + 1) * CHUNK_PITCH, :] = pad_rows

    two = lambda a: jnp.concatenate([a, a], axis=1)
    ra, rb = ra_ref[...], rb_ref[...]
    gains = gains_ref[...]
    bd = bd_ref[...]
    scale = math.log2(math.e) / math.sqrt(QK_DIM)
    inv_d = 1.0 / QK_DIM

    o1 = SSM_WIDTH
    o2 = o1 + Q_LORA
    o3 = o2 + KV_LORA
    hq = _rms(proj[:, o1:o2]).astype(BF16)
    qa = _dot(hq, wqa_ref[...])
    qb = _dot(hq, wqb_ref[...])
    hkv = _rms(proj[:, o2:o3]).astype(BF16)
    kn = _dot(hkv, wk_ref[...])
    vm = _dot(hkv, wv_ref[...])

    kr = proj[:, o3:o3 + LANES]
    kr_sw = proj[:, o3 + LANES:o3 + 2 * LANES]
    kr_ss = two(_dot((kr * kr).astype(BF16), bd[:LANES, :LANES]))
    krot = two(kr * (gains[3:4] * ra) + kr_sw * (gains[4:5] * rb))
    ga, gb, gk = two(ra * gains[0:1]), two(rb * gains[1:2]), two(gains[2:3])
    lane = lax.broadcasted_iota(jnp.int32, ga.shape, 1)
    ones_col = jnp.where((lane & (LANES - 1)) == V_DIM, 1.0, 0.0)

    for hp in range(MLA_HEADS // 2):
        sl = slice(2 * hp * HEAD_PAD, (2 * hp + 2) * HEAD_PAD)
        q2 = qa[:, sl]
        rq = lax.rsqrt(_dot((q2 * q2).astype(BF16), bd) * inv_d + EPS) * scale
        qo = ((q2 * ga + qb[:, sl] * gb) * rq).astype(BF16)
        k2 = kn[:, sl]
        rk = lax.rsqrt((_dot((k2 * k2).astype(BF16), bd) + kr_ss) * inv_d + EPS)
        ko = ((k2 * gk + krot) * rk).astype(BF16)
        vo = (vm[:, sl] + ones_col).astype(BF16)
        for j in range(2):
            q_ref[2 * hp + j] = qo[:, j * HEAD_PAD:(j + 1) * HEAD_PAD]
            k_ref[2 * hp + j] = ko[:, j * HEAD_PAD:(j + 1) * HEAD_PAD]
            v_ref[2 * hp + j] = vo[:, j * HEAD_PAD:(j + 1) * HEAD_PAD]


def _inproj(l, x, w, ra, rb, tm):
    bsz, seq, _ = x.shape
    tok = lambda width: pl.BlockSpec((None, tm, width), lambda b, i: (b, i, 0))
    head = pl.BlockSpec((None, MLA_HEADS, tm, HEAD_PAD), lambda b, i: (b, 0, i, 0))
    hw = MLA_HEADS * HEAD_PAD
    head_shape = jax.ShapeDtypeStruct((bsz, MLA_HEADS, seq, HEAD_PAD), BF16)
    return pl.pallas_call(
        _inproj_kernel,
        grid=(bsz, seq // tm),
        in_specs=[tok(D_MODEL), _layer_spec(l, D_MODEL, IN_COLS_PAD),
                  _layer_spec(l, Q_LORA, hw), _layer_spec(l, Q_LORA, hw),
                  _layer_spec(l, KV_LORA, hw), _layer_spec(l, KV_LORA, hw),
                  _layer_spec(l, 8, HEAD_PAD), _const_spec(2 * HEAD_PAD, 2 * HEAD_PAD),
                  tok(LANES), tok(LANES)],
        out_specs=[pl.BlockSpec((LANE_TILES, None, _pitched(tm), LANES), lambda b, i: (0, b, i, 0)),
                   head, head, head],
        out_shape=[jax.ShapeDtypeStruct((LANE_TILES, bsz, _pitched(seq), LANES), F32),
                   head_shape, head_shape, head_shape],
        compiler_params=_params("parallel", "parallel"),
        name="inproj",
    )(x, w["w_in"], w["w_qa"], w["w_qb"], w["w_k"], w["w_v"], w["mla_gains"], w["bd"], ra, rb)


S5_PREP_GROUPS = 8


def _s5_prep_kernel(*refs):
    per_group_in, shared, outs = refs[:8], refs[8:10], refs[10:]
    for g in range(S5_PREP_GROUPS):
        _s5_prep_group(*[r.at[g] for r in per_group_in], *shared, *[r.at[g] for r in outs])


def _s5_prep_group(pcol_ref, prow_ref, b_re_ref, b_im_ref, bt_re_ref, bt_im_ref,
                   ct_re_ref, ct_im_ref, sel_ref, reps_ref,
                   m_ref, wst_ref, wo_ref, apa_ref, apb_ref):
    def zoh_coeff(a_re, a_im, lr, li):
        xr = a_re - 1.0
        den = lr * lr + li * li
        return (xr * lr + a_im * li) / den, (a_im * lr - xr * li) / den

    def zoh(lr, li, ls):
        step = jnp.exp(ls)
        mag = jnp.exp(lr * step)
        ang = li * step
        return zoh_coeff(mag * jnp.cos(ang), mag * jnp.sin(ang), lr, li) + (step,)

    pc = pcol_ref[...]
    lr, li = pc[:, 0:1], pc[:, 1:2]
    step = jnp.exp(pc[:, 2:3])
    n = lax.broadcasted_iota(jnp.int32, (SSM_STATE, LANES), 1).astype(F32)
    mag = jnp.exp(lr * step * n)
    th = li * step * n
    pw_re, pw_im = mag * jnp.cos(th), mag * jnp.sin(th)
    cfr, cfi = zoh_coeff(pw_re[:, 1:2], pw_im[:, 1:2], lr, li)
    b_re, b_im = b_re_ref[...], b_im_ref[...]
    bb_re = cfr * b_re - cfi * b_im
    bb_im = cfr * b_im + cfi * b_re

    pr = prow_ref[...]
    lr_row, li_row, ls_row = pr[0:1], pr[1:2], pr[2:3]
    cfr_row, cfi_row, step_row = zoh(lr_row, li_row, ls_row)
    bt_re, bt_im = bt_re_ref[...], bt_im_ref[...]
    bbt_re = cfr_row[:, :SSM_STATE] * bt_re - cfi_row[:, :SSM_STATE] * bt_im
    bbt_im = cfr_row[:, :SSM_STATE] * bt_im + cfi_row[:, :SSM_STATE] * bt_re

    sel = sel_ref[...]
    brep_re, brep_im = _select_cols(bb_re, sel), _select_cols(bb_im, sel)
    crep_re, crep_im = _select_cols(ct_re_ref[...], sel), _select_cols(ct_im_ref[...], sel)

    apow = lambda i: (_select_cols(pw_re, reps_ref[i]), _select_cols(pw_im, reps_ref[i]))

    e_re, e_im = apow(0)
    g_re = e_re * crep_re - e_im * crep_im
    g_im = e_re * crep_im + e_im * crep_re
    kt = _dot_exact(bbt_re, g_re) - _dot_exact(bbt_im, g_im)
    klane = lax.broadcasted_iota(jnp.int32, kt.shape, 1)
    m_ref[0:SSM_GROUP, :] = kt.astype(BF16)
    for s in range(1, SSM_CHUNK):
        shifted = jnp.where(klane >= s * SSM_GROUP, pltpu.roll(kt, s * SSM_GROUP, axis=1), 0.0)
        m_ref[s * SSM_GROUP:(s + 1) * SSM_GROUP, :] = shifted.astype(BF16)

    e_re, e_im = apow(1)
    wst_ref[0:SSM_STATE, :] = (e_re * brep_re - e_im * brep_im).astype(BF16)
    wst_ref[SSM_STATE:, :] = (e_re * brep_im + e_im * brep_re).astype(BF16)

    e_re, e_im = apow(2)
    wo_ref[0:SSM_STATE, :] = (e_re * crep_re - e_im * crep_im).astype(BF16)
    wo_ref[SSM_STATE:, :] = (-(e_re * crep_im + e_im * crep_re)).astype(BF16)

    krow = lax.broadcasted_iota(jnp.int32, (8, LANES), 0)
    klan = lax.broadcasted_iota(jnp.int32, (8, LANES), 1)
    nn = (SSM_CHUNK * jnp.left_shift(1, krow)).astype(F32)
    mag = jnp.exp(lr_row * step_row * nn)
    th = li_row * step_row * nn
    p_re, p_im = mag * jnp.cos(th), mag * jnp.sin(th)
    apa_ref[...] = p_re
    apb_ref[...] = jnp.where(klan < SSM_STATE, -p_im, p_im)


def _s5_prep(lam_re, lam_im, log_step, b_re, b_im, c_re, c_im):
    nl, g, p = lam_re.shape
    hh = SSM_GROUP
    ls = jnp.broadcast_to(log_step[..., None], (nl, g, p))
    params = jnp.stack([lam_re, lam_im, ls], axis=-1)
    pcol = jnp.pad(params, ((0, 0), (0, 0), (0, 0), (0, 5)))
    prow = jnp.swapaxes(jnp.concatenate([pcol, pcol], axis=2), 2, 3)
    sel = jnp.tile(jnp.eye(hh, dtype=BF16), (1, SSM_CHUNK))
    tau = jnp.arange(CHUNK_W) // hh
    nrow = jnp.arange(LANES)[:, None]
    reps = jnp.stack([nrow == tau, nrow == SSM_CHUNK - 1 - tau, nrow == tau + 1]).astype(BF16)
    blk = lambda *s: pl.BlockSpec((None, S5_PREP_GROUPS) + s, lambda l, i: (l, i) + (0,) * len(s))
    return pl.pallas_call(
        _s5_prep_kernel,
        grid=(nl, g // S5_PREP_GROUPS),
        in_specs=[blk(p, 8), blk(8, LANES), blk(p, hh), blk(p, hh), blk(hh, p), blk(hh, p),
                  blk(p, hh), blk(p, hh), _const_spec(hh, CHUNK_W), _const_spec(3, LANES, CHUNK_W)],
        out_specs=[blk(CHUNK_W, CHUNK_W), blk(2 * p, CHUNK_W), blk(2 * p, CHUNK_W),
                   blk(8, LANES), blk(8, LANES)],
        out_shape=[jax.ShapeDtypeStruct((nl, g, CHUNK_W, CHUNK_W), BF16),
                   jax.ShapeDtypeStruct((nl, g, 2 * p, CHUNK_W), BF16),
                   jax.ShapeDtypeStruct((nl, g, 2 * p, CHUNK_W), BF16),
                   jax.ShapeDtypeStruct((nl, g, 8, LANES), F32),
                   jax.ShapeDtypeStruct((nl, g, 8, LANES), F32)],
        compiler_params=_params("parallel", "parallel"),
        name="s5_prep",
    )(pcol, prow, b_re, b_im, jnp.swapaxes(b_re, -1, -2), jnp.swapaxes(b_im, -1, -2),
      jnp.swapaxes(c_re, -1, -2), jnp.swapaxes(c_im, -1, -2), sel, reps)


def _s5_mix_kernel(u_ref, perm_ref, m_ref, wst_ref, wo_ref, apa_ref, apb_ref, dx_ref, y_ref,
                   *, nsteps):
    nb, rows, _ = u_ref.shape
    cps = rows // CHUNK_PITCH
    nc = nb * cps
    gpt = GROUPS_PER_TILE
    nslab = SSM_CHUNK // gpt
    perm = perm_ref[...]

    def slab(j):
        return jnp.concatenate([
            jnp.concatenate([u_ref[b, pl.ds(gpt * j + s, cps, stride=CHUNK_PITCH), :] for s in range(gpt)], axis=1)
            for b in range(nb)], axis=0)

    z = _dot(jnp.concatenate([slab(j) for j in range(nslab)], axis=0).astype(BF16), perm).astype(BF16)

    c = lax.broadcasted_iota(jnp.int32, (nc, 2 * SSM_STATE), 0) & (cps - 1)
    ys = []
    for g in range(gpt):
        x = jnp.concatenate([z[j * nc:(j + 1) * nc, g * LANES:(g + 1) * LANES] for j in range(nslab)],
                            axis=1)
        y = _dot(x, m_ref[g])
        st = _dot_nt(x, wst_ref[g])
        apa, apb = apa_ref[g], apb_ref[g]
        for k in range(nsteps):
            d = 1 << k
            sh = jnp.where(c >= d, pltpu.roll(st, d, axis=0), 0.0)
            st = st + sh * apa[k:k + 1] + pltpu.roll(sh, SSM_STATE, axis=1) * apb[k:k + 1]
        carried = jnp.where(c >= 1, pltpu.roll(st, 1, axis=0), 0.0)
        y = y + _dot(carried.astype(BF16), wo_ref[g]) + dx_ref[g] * x.astype(F32)
        ys.append(jax.nn.gelu(y).astype(BF16))

    w = jnp.concatenate([
        jnp.concatenate([ys[g][:, j * LANES:(j + 1) * LANES] for g in range(gpt)], axis=1)
        for j in range(nslab)], axis=0)
    o = _dot(w, perm)
    for b in range(nb):
        for j in range(nslab):
            for t in range(gpt):
                y_ref[b, pl.ds(gpt * j + t, cps, stride=CHUNK_PITCH), :] = (
                    o[j * nc + b * cps:j * nc + (b + 1) * cps, t * LANES:(t + 1) * LANES])
        for r in range(SSM_CHUNK, CHUNK_PITCH):
            y_ref[b, pl.ds(r, cps, stride=CHUNK_PITCH), :] = jnp.zeros((cps, LANES), F32)


def _s5_mix(l, u, prep, dx, perm):
    tiles, bsz, rows, _ = u.shape
    cps = rows // CHUNK_PITCH
    assert cps & (cps - 1) == 0 and cps <= 256
    gpt = GROUPS_PER_TILE
    nb = 2 if bsz % 2 == 0 else 1
    m, wst, wo, apa, apb = prep
    tokens = pl.BlockSpec((None, nb, rows, LANES), lambda t, b: (t, b, 0, 0))
    grp = lambda *s: pl.BlockSpec((None, gpt) + s, lambda t, b: (l, t) + (0,) * len(s))
    return pl.pallas_call(
        functools.partial(_s5_mix_kernel, nsteps=cps.bit_length() - 1),
        grid=(tiles, bsz // nb),
        in_specs=[tokens, _const_spec(gpt * LANES, gpt * LANES), grp(CHUNK_W, CHUNK_W),
                  grp(2 * SSM_STATE, CHUNK_W), grp(2 * SSM_STATE, CHUNK_W), grp(8, LANES), grp(8, LANES),
                  grp(1, CHUNK_W)],
        out_specs=tokens,
        out_shape=jax.ShapeDtypeStruct(u.shape, F32),
        compiler_params=_params("parallel", "parallel"),
        name="s5_mix",
    )(u, perm, m, wst, wo, apa, apb, dx)


FULL = 0


def _flash_kernel(qt_ref, kt_ref, kind_ref, q_ref, k_ref, v_ref, o_ref, m_scr, acc_scr, *, heads):
    p_idx = pl.program_id(2)
    ki = kt_ref[p_idx]
    kind = kind_ref[p_idx]
    tq, tk = q_ref.shape[2], k_ref.shape[2]

    @pl.when(ki == 0)
    def _():
        m_scr[...] = jnp.full(m_scr.shape, -jnp.inf, F32)
        acc_scr[...] = jnp.zeros(acc_scr.shape, F32)

    def attend(j, nk, diag_offset):
        s = _dot_nt(q_ref[0, j], k_ref[0, j, 0:nk, :])
        if diag_offset is not None:
            row = lax.broadcasted_iota(jnp.int32, s.shape, 0)
            col = lax.broadcasted_iota(jnp.int32, s.shape, 1)
            s = jnp.where(col <= row + diag_offset, s, -jnp.inf)
        m_prev = m_scr[j]
        m_new = jnp.maximum(m_prev, jnp.max(s, axis=-1, keepdims=True))
        alpha = jnp.exp2(m_prev - m_new)
        p = jnp.exp2(s - jnp.tile(m_new, (1, nk // LANES)))
        acc_scr[j] = alpha * acc_scr[j] + _dot(p.astype(BF16), v_ref[0, j, 0:nk, :])
        m_scr[j] = m_new

    def all_heads(nk, diag_offset):
        def group(g, carry):
            for j in range(FLASH_UNROLL):
                attend(g * FLASH_UNROLL + j, nk, diag_offset)
            return carry
        lax.fori_loop(0, heads // FLASH_UNROLL, group, 0)

    @pl.when(kind == FULL)
    def _():
        all_heads(tk, None)

    for r in range(1, tk // tq + 1):
        @pl.when(kind == r)
        def _(r=r):
            all_heads(r * tq, (r - 1) * tq)

    @pl.when(kind != FULL)
    def _():
        outs = []
        for j in range(heads):
            acc = acc_scr[j]
            outs.append(acc[:, :V_DIM] / acc[:, V_DIM:V_DIM + 1])
        o_ref[0] = jnp.concatenate(outs, axis=-1).astype(BF16)


def _flash(q, k, v, tq, heads_per_step):
    bsz, heads, seq, _ = q.shape
    hp = heads_per_step
    ratio = FLASH_KEY_RATIO
    tk = ratio * tq
    assert seq % tk == 0
    steps = []
    for i in range(seq // tq):
        steps += [(i, j, FULL) for j in range(i // ratio)] + [(i, i // ratio, i % ratio + 1)]
    qt, kt, kind = (jnp.asarray(col, jnp.int32) for col in zip(*steps))
    qspec = pl.BlockSpec((1, hp, tq, HEAD_PAD), lambda b, h, p, qt, kt, kind: (b, h, qt[p], 0))
    kspec = pl.BlockSpec((1, hp, tk, HEAD_PAD), lambda b, h, p, qt, kt, kind: (b, h, kt[p], 0))
    grid_spec = pltpu.PrefetchScalarGridSpec(
        num_scalar_prefetch=3,
        grid=(bsz, heads // hp, len(steps)),
        in_specs=[qspec, kspec, kspec],
        out_specs=pl.BlockSpec((1, tq, hp * V_DIM), lambda b, h, p, qt, kt, kind: (b, qt[p], h)),
        scratch_shapes=[pltpu.VMEM((hp, tq, LANES), F32), pltpu.VMEM((hp, tq, HEAD_PAD), F32)])
    return pl.pallas_call(
        functools.partial(_flash_kernel, heads=hp),
        grid_spec=grid_spec,
        out_shape=jax.ShapeDtypeStruct((bsz, seq, heads * V_DIM), BF16),
        compiler_params=_params("parallel", "parallel", "arbitrary"),
        name="flash",
    )(qt, kt, kind, q, k, v)


def _head_sums(v2, lane, width):
    out = jnp.zeros_like(v2)
    for hd in range(v2.shape[-1] // width):
        msk = (lane >= hd * width) & (lane < (hd + 1) * width)
        out = out + jnp.where(msk, jnp.sum(jnp.where(msk, v2, 0.0), axis=-1, keepdims=True), 0.0)
    return out


def _mem_prep_kernel(mem_ref, g_ref, w_ref, kg_ref, kt_ref, vm_ref):
    hm = _rms(mem_ref[0], g_ref[0]).astype(BF16)
    kv = _dot(hm, w_ref[0])
    k, v = kv[:, :MEM_WIDTH], kv[:, MEM_WIDTH:]
    lane = lax.broadcasted_iota(jnp.int32, k.shape, 1)
    ss = _head_sums(k * k, lane, MEM_HEAD_DIM)
    kn = k * lax.rsqrt(ss * (1.0 / MEM_HEAD_DIM) + EPS) * kg_ref[0] * (math.log2(math.e) / math.sqrt(MEM_HEAD_DIM))
    knt = kn.T
    row = lax.broadcasted_iota(jnp.int32, knt.shape, 0)
    for hd in range(MEM_HEADS):
        lo, hi = hd * MEM_HEAD_DIM, (hd + 1) * MEM_HEAD_DIM
        kt_ref[0, 0, hd] = jnp.where((row >= lo) & (row < hi), knt, 0.0).astype(BF16)
        vm_ref[0, 0, hd] = jnp.where((lane >= lo) & (lane < hi), v, 0.0).astype(BF16)


def _mem_prep(mem, g, w_kv, k_gain):
    nl = g.shape[0]
    bsz = mem.shape[0]
    out = jax.ShapeDtypeStruct((nl, bsz, MEM_HEADS, N_MEM, MEM_WIDTH), BF16)
    return pl.pallas_call(
        _mem_prep_kernel,
        grid=(nl, bsz),
        in_specs=[pl.BlockSpec((1, N_MEM, D_MODEL), lambda l, b: (b, 0, 0)),
                  pl.BlockSpec((1, 1, D_MODEL), lambda l, b: (l, 0, 0)),
                  pl.BlockSpec((1, D_MODEL, 2 * MEM_WIDTH), lambda l, b: (l, 0, 0)),
                  pl.BlockSpec((1, 1, MEM_WIDTH), lambda l, b: (l, 0, 0))],
        out_specs=[pl.BlockSpec((1, 1, MEM_HEADS, MEM_WIDTH, N_MEM), lambda l, b: (l, b, 0, 0, 0)),
                   pl.BlockSpec((1, 1, MEM_HEADS, N_MEM, MEM_WIDTH), lambda l, b: (l, b, 0, 0, 0))],
        out_shape=[out, out],
        compiler_params=_params("parallel", "parallel"),
        name="mem_prep",
    )(mem, g, w_kv, k_gain)


def _outproj_kernel(x_ref, ys_ref, om_ref, wglu_ref, bglu_ref, wout_ref,
                    wmq_ref, qg_ref, bdm_ref, kt_ref, vm_ref, wmo_ref,
                    x2_ref, h3_ref):
    nchunk = x_ref.shape[0] // SSM_CHUNK
    ys = jnp.concatenate([
        jnp.concatenate([ys_ref[t, c * CHUNK_PITCH:c * CHUNK_PITCH + SSM_CHUNK, :] for c in range(nchunk)], axis=0)
        for t in range(LANE_TILES)], axis=1)
    yg = ys * jax.nn.sigmoid(_dot(ys.astype(BF16), wglu_ref[...]) + bglu_ref[...])
    n1 = _rms(yg).astype(BF16)
    n2 = _rms(om_ref[...].astype(F32)).astype(BF16)
    x1 = x_ref[...] + _dot(n1, wout_ref[:SSM_WIDTH, :]) + _dot(n2, wout_ref[SSM_WIDTH:, :])

    r1 = lax.rsqrt(jnp.mean(x1 * x1, axis=-1, keepdims=True) + EPS)
    q = _dot(x1.astype(BF16), wmq_ref[...]) * r1
    ss = _dot((q * q).astype(BF16), bdm_ref[...])
    qn = (q * lax.rsqrt(ss * (1.0 / MEM_HEAD_DIM) + EPS) * qg_ref[...]).astype(BF16)
    o = jnp.zeros(q.shape, F32)
    for hd in range(MEM_HEADS):
        s = _dot(qn, kt_ref[hd])
        p = jnp.exp2(s - jnp.max(s, axis=-1, keepdims=True))
        inv = 1.0 / jnp.sum(p, axis=-1, keepdims=True)
        o = o + _dot(p.astype(BF16), vm_ref[hd]) * inv
    x2 = x1 + _dot(o.astype(BF16), wmo_ref[...])
    x2_ref[...] = x2
    h3_ref[...] = _rms(x2).astype(BF16)


def _outproj(l, x, ys, om, w, kt, vm, tm):
    bsz, seq, _ = x.shape
    tok = lambda width: pl.BlockSpec((None, tm, width), lambda b, i: (b, i, 0))
    memb = pl.BlockSpec((None, None, MEM_HEADS, N_MEM, MEM_WIDTH), lambda b, i: (l, b, 0, 0, 0))
    return pl.pallas_call(
        _outproj_kernel,
        grid=(bsz, seq // tm),
        in_specs=[tok(D_MODEL), pl.BlockSpec((LANE_TILES, None, _pitched(tm), LANES), lambda b, i: (0, b, i, 0)),
                  tok(MLA_WIDTH),
                  _layer_spec(l, SSM_WIDTH, SSM_WIDTH), _layer_spec(l, 1, SSM_WIDTH),
                  _layer_spec(l, D_MODEL, D_MODEL),
                  _layer_spec(l, D_MODEL, MEM_WIDTH), _layer_spec(l, 1, MEM_WIDTH),
                  _const_spec(MEM_WIDTH, MEM_WIDTH),
                  memb, memb, _layer_spec(l, MEM_WIDTH, D_MODEL)],
        out_specs=[tok(D_MODEL), tok(D_MODEL)],
        out_shape=[jax.ShapeDtypeStruct((bsz, seq, D_MODEL), F32),
                   jax.ShapeDtypeStruct((bsz, seq, D_MODEL), BF16)],
        compiler_params=_params("parallel", "parallel"),
        name="outproj",
    )(x, ys, om, w["w_glu"], w["b_glu"], w["w_out"],
      w["w_mq"], w["mem_q_gain"], w["bd_mem"], kt, vm, w["w_mo"])


def _mlp_kernel(x_ref, h_ref, w1_ref, w2_ref, o_ref, *, ff_tile):
    h = h_ref[...]
    acc = x_ref[...]
    for c in range(D_FF // ff_tile):
        a = jnp.maximum(_dot(h, w1_ref[:, c * ff_tile:(c + 1) * ff_tile]), 0.0)
        acc = acc + _dot((a * a).astype(BF16), w2_ref[c * ff_tile:(c + 1) * ff_tile, :])
    o_ref[...] = acc


def _mlp(l, x, h, w1, w2, tm):
    t = x.shape[0]
    tok = pl.BlockSpec((tm, D_MODEL), lambda i: (i, 0))
    return pl.pallas_call(
        functools.partial(_mlp_kernel, ff_tile=1024),
        grid=(t // tm,),
        in_specs=[tok, tok,
                  pl.BlockSpec((None, D_MODEL, D_FF), lambda i: (l, 0, 0), pipeline_mode=pl.Buffered(1)),
                  pl.BlockSpec((None, D_FF, D_MODEL), lambda i: (l, 0, 0), pipeline_mode=pl.Buffered(1))],
        out_specs=tok,
        out_shape=jax.ShapeDtypeStruct((t, D_MODEL), F32),
        compiler_params=_params("parallel"),
        name="mlp",
    )(x, h, w1, w2)


def _half_swap(w):
    h = w.shape[-1] // 2
    return jnp.concatenate([-w[..., h:], w[..., :h]], axis=-1)


def _half_swap_unsigned(w):
    h = w.shape[-1] // 2
    return jnp.concatenate([w[..., h:], w[..., :h]], axis=-1)


def _pad_last(w, before, after):
    pads = [(0, 0)] * (w.ndim - 1) + [(before, after)]
    return jnp.pad(w, pads)


def _trunk(tm, tq, x, mem, positions, norm_mix, w_in, ssm_lambda_re, ssm_lambda_im, ssm_log_step, ssm_b_re, ssm_b_im, ssm_c_re, ssm_c_im, ssm_d, ssm_w_glu, ssm_b_glu, mla_q_norm, mla_w_uq, mla_kv_norm, mla_w_ukv, mla_q_gain, mla_k_gain, out_norm_ssm, out_norm_mla, w_out, norm_mem_q, norm_mem_kv, mem_w_q, mem_w_kv, mem_q_gain, mem_k_gain, mem_w_o, norm_mlp, mlp_w1, mlp_w2):
    bsz, seq, _ = x.shape
    depth = norm_mix.shape[0]
    row = lambda a: a[:, None, :]
    tail = HEAD_PAD - QK_DIM

    s3 = SSM_WIDTH + Q_LORA + KV_LORA
    rows = lambda g: g[:, :, None]
    w_in_g = (rows(norm_mix) * w_in).astype(BF16)
    k_rope_w = w_in_g[..., s3:]
    w_in_x = jnp.concatenate([w_in_g[..., :s3], _pad_last(k_rope_w, QK_NOPE, tail),
                              _pad_last(_half_swap(k_rope_w), QK_NOPE, tail)], axis=-1)
    wq = mla_w_uq.reshape(depth, Q_LORA, MLA_HEADS, QK_DIM)
    hw = MLA_HEADS * HEAD_PAD
    gq = rows(mla_q_norm)
    w_qa = (gq * _pad_last(wq, 0, tail).reshape(depth, Q_LORA, hw)).astype(BF16)
    w_qb = (gq * _pad_last(_half_swap(wq[..., QK_NOPE:]), QK_NOPE, tail).reshape(depth, Q_LORA, hw)).astype(BF16)
    wkv = mla_w_ukv.reshape(depth, KV_LORA, MLA_HEADS, QK_NOPE + V_DIM)
    gkv = rows(mla_kv_norm)
    w_k = (gkv * _pad_last(wkv[..., :QK_NOPE], 0, HEAD_PAD - QK_NOPE).reshape(depth, KV_LORA, hw)).astype(BF16)
    w_v = (gkv * _pad_last(wkv[..., QK_NOPE:], 0, HEAD_PAD - V_DIM).reshape(depth, KV_LORA, hw)).astype(BF16)
    q_rope_g, k_rope_g = mla_q_gain[:, QK_NOPE:], mla_k_gain[:, QK_NOPE:]
    mla_gains = jnp.stack([
        _pad_last(mla_q_gain, 0, tail),
        _pad_last(_half_swap_unsigned(q_rope_g), QK_NOPE, tail),
        _pad_last(mla_k_gain[:, :QK_NOPE], 0, HEAD_PAD - QK_NOPE),
        _pad_last(k_rope_g, QK_NOPE, tail),
        _pad_last(_half_swap_unsigned(k_rope_g), QK_NOPE, tail)], axis=1)
    mla_gains = jnp.pad(mla_gains, ((0, 0), (0, 3), (0, 0)))
    blk = jnp.arange(2 * HEAD_PAD) // HEAD_PAD
    bd = (blk[:, None] == blk[None, :]).astype(BF16)
    mblk = jnp.arange(MEM_WIDTH) // MEM_HEAD_DIM
    bd_mem = (mblk[:, None] == mblk[None, :]).astype(BF16)
    wmkv = mem_w_kv.reshape(depth, D_MODEL, MEM_HEADS, 2, MEM_HEAD_DIM)
    w_mkv = wmkv.transpose(0, 1, 3, 2, 4).reshape(depth, D_MODEL, 2 * MEM_WIDTH).astype(BF16)
    idx = jnp.arange(GROUPS_PER_TILE * LANES)
    dest = ((idx // SSM_GROUP) % GROUPS_PER_TILE) * LANES + (idx // LANES) * SSM_GROUP + idx % SSM_GROUP
    perm = (dest[:, None] == idx[None, :]).astype(BF16)

    g_out = jnp.concatenate([out_norm_ssm, out_norm_mla], axis=-1)
    w = dict(w_in=w_in_x, w_qa=w_qa, w_qb=w_qb, w_k=w_k, w_v=w_v, mla_gains=mla_gains, bd=bd, bd_mem=bd_mem,
             w_glu=ssm_w_glu.astype(BF16), b_glu=row(ssm_b_glu),
             w_out=(rows(g_out) * w_out).astype(BF16),
             w_mq=(rows(norm_mem_q) * mem_w_q).astype(BF16),
             mem_q_gain=row(jnp.tile(mem_q_gain, (1, MEM_HEADS))),
             w_mo=mem_w_o.astype(BF16))
    w1_b = (rows(norm_mlp) * mlp_w1).astype(BF16)
    w2_b = mlp_w2.astype(BF16)

    ra, rb = _rope_tables(positions)
    prep = _s5_prep(ssm_lambda_re, ssm_lambda_im, ssm_log_step, ssm_b_re, ssm_b_im, ssm_c_re, ssm_c_im)
    kt_all, vm_all = _mem_prep(mem, row(norm_mem_kv), w_mkv, row(jnp.tile(mem_k_gain, (1, MEM_HEADS))))
    dx = jnp.tile(ssm_d.reshape(depth, SSM_GROUPS, 1, SSM_GROUP), (1, 1, 1, SSM_CHUNK))

    for l in range(depth):
        u, q, k, v = _inproj(l, x, w, ra, rb, min(2 * tm, seq))
        ys = _s5_mix(l, u, prep, dx, perm)
        om = _flash(q, k, v, tq, FLASH_HEADS)
        x2, h3 = _outproj(l, x, ys, om, w, kt_all, vm_all, min(2 * tm, seq))
        x = _mlp(l, x2.reshape(bsz * seq, D_MODEL), h3.reshape(bsz * seq, D_MODEL),
                 w1_b, w2_b, min(2 * tm, seq)).reshape(bsz, seq, D_MODEL)
    return x


def kernel(x, mem, positions, norm_mix, w_in, ssm_lambda_re, ssm_lambda_im, ssm_log_step, ssm_b_re, ssm_b_im, ssm_c_re, ssm_c_im, ssm_d, ssm_w_glu, ssm_b_glu, mla_q_norm, mla_w_uq, mla_kv_norm, mla_w_ukv, mla_q_gain, mla_k_gain, out_norm_ssm, out_norm_mla, w_out, norm_mem_q, norm_mem_kv, mem_w_q, mem_w_kv, mem_q_gain, mem_k_gain, mem_w_o, norm_mlp, mlp_w1, mlp_w2):
    seq = x.shape[1]
    return _trunk(min(512, seq), min(512, seq // 2), x, mem, positions, norm_mix, w_in, ssm_lambda_re, ssm_lambda_im, ssm_log_step, ssm_b_re, ssm_b_im, ssm_c_re, ssm_c_im, ssm_d, ssm_w_glu, ssm_b_glu, mla_q_norm, mla_w_uq, mla_kv_norm, mla_w_ukv, mla_q_gain, mla_k_gain, out_norm_ssm, out_norm_mla, w_out, norm_mem_q, norm_mem_kv, mem_w_q, mem_w_kv, mem_q_gain, mem_k_gain, mem_w_o, norm_mlp, mlp_w1, mlp_w2)
```

```python
import functools
import math

import jax
import jax.numpy as jnp
from jax import lax
from jax.experimental import pallas as pl
from jax.experimental.pallas import tpu as pltpu

D_MODEL = 1024
N_MEM = 256
MEM_HEADS = 4
MEM_HEAD_DIM = 64
MEM_WIDTH = MEM_HEADS * MEM_HEAD_DIM
SSM_WIDTH = 512
MLA_WIDTH = 512
SSM_GROUP = 16
SSM_GROUPS = 32
SSM_STATE = 64
MLA_HEADS = 8
QK_NOPE = 64
QK_ROPE = 32
QK_DIM = QK_NOPE + QK_ROPE
V_DIM = 64
Q_LORA = 256
KV_LORA = 128
ROPE_THETA = 10000.0
D_FF = 4 * D_MODEL
EPS = 1e-6

LANES = 128
HEAD_PAD = 128
SSM_CHUNK = 32
CHUNK_W = SSM_CHUNK * SSM_GROUP
CHUNK_PITCH = SSM_CHUNK + 4
GROUPS_PER_TILE = LANES // SSM_GROUP
LANE_TILES = SSM_WIDTH // LANES
IN_COLS_PAD = SSM_WIDTH + Q_LORA + KV_LORA + 2 * LANES
FLASH_UNROLL = 4
FLASH_KEY_RATIO = 4
FLASH_HEADS = 8
VMEM_LIMIT = 56 * 1024 * 1024

F32 = jnp.float32
BF16 = jnp.bfloat16
HIGHEST = lax.Precision.HIGHEST


def _dot(a, b):
    return jnp.dot(a, b, preferred_element_type=F32)


def _dot_exact(a, b):
    return jnp.dot(a, b, precision=HIGHEST, preferred_element_type=F32)


def _select_cols(a, onehot):
    hi = a.astype(BF16)
    lo = (a - hi.astype(F32)).astype(BF16)
    return _dot(hi, onehot) + _dot(lo, onehot)


def _dot_nt(a, b):
    return lax.dot_general(a, b, (((1,), (1,)), ((), ())), preferred_element_type=F32)


def _rms(v, gain=None):
    y = v * lax.rsqrt(jnp.mean(v * v, axis=-1, keepdims=True) + EPS)
    return y if gain is None else y * gain


def _params(*sem, flags=None):
    return pltpu.CompilerParams(dimension_semantics=sem, vmem_limit_bytes=VMEM_LIMIT, flags=flags)


def _layer_spec(l, *shape):
    return pl.BlockSpec((None,) + shape, lambda *_: (l,) + (0,) * len(shape))


def _pitched(tokens):
    return tokens // SSM_CHUNK * CHUNK_PITCH


def _const_spec(*shape):
    return pl.BlockSpec(shape, lambda *_: (0,) * len(shape))


def _rope_table_kernel(pos_ref, freq_ref, a_ref, b_ref):
    ang = pos_ref[...].astype(F32) * freq_ref[...]
    lane = lax.broadcasted_iota(jnp.int32, ang.shape, 1)
    in_rope = (lane >= QK_NOPE) & (lane < QK_DIM)
    a_ref[...] = jnp.where(lane < QK_NOPE, 1.0, jnp.where(in_rope, jnp.cos(ang), 0.0))
    b_ref[...] = jnp.where(in_rope, jnp.sin(ang), 0.0)


def _rope_tables(positions):
    bsz, seq = positions.shape
    t = bsz * seq
    tm = min(1024, t)
    half = QK_ROPE // 2
    inv_freq = ROPE_THETA ** (-jnp.arange(half, dtype=F32) / half)
    freq = jnp.pad(jnp.tile(inv_freq, 2), (QK_NOPE, LANES - QK_DIM))[None, :]
    a, b = pl.pallas_call(
        _rope_table_kernel,
        grid=(t // tm,),
        in_specs=[pl.BlockSpec((tm, 1), lambda i: (i, 0)), _const_spec(1, LANES)],
        out_specs=[pl.BlockSpec((tm, LANES), lambda i: (i, 0))] * 2,
        out_shape=[jax.ShapeDtypeStruct((t, LANES), F32)] * 2,
        compiler_params=_params("parallel"),
        name="rope_tables",
    )(positions.reshape(t, 1), freq)
    return a.reshape(bsz, seq, LANES), b.reshape(bsz, seq, LANES)


def _inproj_kernel(x_ref, win_ref, wqa_ref, wqb_ref, wk_ref, wv_ref,
                   gains_ref, bd_ref, ra_ref, rb_ref, u_ref, q_ref, k_ref, v_ref):
    h = _rms(x_ref[...]).astype(BF16)
    proj = _dot(h, win_ref[...])
    pad_rows = jnp.zeros((CHUNK_PITCH - SSM_CHUNK, LANES), F32)
    for t in range(LANE_TILES):
        for c in range(proj.shape[0] // SSM_CHUNK):
            u_ref[t, c * CHUNK_PITCH:c * CHUNK_PITCH + SSM_CHUNK, :] = (
                proj[c * SSM_CHUNK:(c + 1) * SSM_CHUNK, t * LANES:(t + 1) * LANES])
            u_ref[t, c * CHUNK_PITCH + SSM_CHUNK:(c + 1) * CHUNK_PITCH, :] = pad_rows

    two = lambda a: jnp.concatenate([a, a], axis=1)
    ra, rb = ra_ref[...], rb_ref[...]
    gains = gains_ref[...]
    bd = bd_ref[...]
    scale = math.log2(math.e) / math.sqrt(QK_DIM)
    inv_d = 1.0 / QK_DIM

    o1 = SSM_WIDTH
    o2 = o1 + Q_LORA
    o3 = o2 + KV_LORA
    hq = _rms(proj[:, o1:o2]).astype(BF16)
    qa = _dot(hq, wqa_ref[...])
    qb = _dot(hq, wqb_ref[...])
    hkv = _rms(proj[:, o2:o3]).astype(BF16)
    kn = _dot(hkv, wk_ref[...])
    vm = _dot(hkv, wv_ref[...])

    kr = proj[:, o3:o3 + LANES]
    kr_sw = proj[:, o3 + LANES:o3 + 2 * LANES]
    kr_ss = two(_dot((kr * kr).astype(BF16), bd[:LANES, :LANES]))
    krot = two(kr * (gains[3:4] * ra) + kr_sw * (gains[4:5] * rb))
    ga, gb, gk = two(ra * gains[0:1]), two(rb * gains[1:2]), two(gains[2:3])
    lane = lax.broadcasted_iota(jnp.int32, ga.shape, 1)
    ones_col = jnp.where((lane >= V_DIM) & (lane < HEAD_PAD + V_DIM), 1.0, 0.0)

    for hp in range(MLA_HEADS // 2):
        sl = slice(2 * hp * HEAD_PAD, (2 * hp + 2) * HEAD_PAD)
        q2 = qa[:, sl]
        rq = lax.rsqrt(_dot((q2 * q2).astype(BF16), bd) * inv_d + EPS) * scale
        qo = ((q2 * ga + qb[:, sl] * gb) * rq).astype(BF16)
        k2 = kn[:, sl]
        rk = lax.rsqrt((_dot((k2 * k2).astype(BF16), bd) + kr_ss) * inv_d + EPS)
        ko = ((k2 * gk + krot) * rk).astype(BF16)
        vo = (vm[:, sl] + ones_col).astype(BF16)
        for j in range(2):
            q_ref[2 * hp + j] = qo[:, j * HEAD_PAD:(j + 1) * HEAD_PAD]
            k_ref[2 * hp + j] = ko[:, j * HEAD_PAD:(j + 1) * HEAD_PAD]
            v_ref[2 * hp + j] = vo[:, j * HEAD_PAD:(j + 1) * HEAD_PAD]


def _inproj(l, x, w, ra, rb, tm):
    bsz, seq, _ = x.shape
    tok = lambda width: pl.BlockSpec((None, tm, width), lambda b, i: (b, i, 0))
    head = pl.BlockSpec((None, MLA_HEADS, tm, HEAD_PAD), lambda b, i: (b, 0, i, 0))
    hw = MLA_HEADS * HEAD_PAD
    head_shape = jax.ShapeDtypeStruct((bsz, MLA_HEADS, seq, HEAD_PAD), BF16)
    return pl.pallas_call(
        _inproj_kernel,
        grid=(bsz, seq // tm),
        in_specs=[tok(D_MODEL), _layer_spec(l, D_MODEL, IN_COLS_PAD),
                  _layer_spec(l, Q_LORA, hw), _layer_spec(l, Q_LORA, hw),
                  _layer_spec(l, KV_LORA, hw), _layer_spec(l, KV_LORA, hw),
                  _layer_spec(l, 8, HEAD_PAD), _const_spec(2 * HEAD_PAD, 2 * HEAD_PAD),
                  tok(LANES), tok(LANES)],
        out_specs=[pl.BlockSpec((LANE_TILES, None, _pitched(tm), LANES), lambda b, i: (0, b, i, 0)),
                   head, head, head],
        out_shape=[jax.ShapeDtypeStruct((LANE_TILES, bsz, _pitched(seq), LANES), F32),
                   head_shape, head_shape, head_shape],
        compiler_params=_params("parallel", "parallel"),
        name="inproj",
    )(x, w["w_in"], w["w_qa"], w["w_qb"], w["w_k"], w["w_v"], w["mla_gains"], w["bd"], ra, rb)


S5_PREP_GROUPS = 8


def _s5_prep_kernel(*refs):
    per_group_in, shared, outs = refs[:8], refs[8:10], refs[10:]
    for g in range(S5_PREP_GROUPS):
        _s5_prep_group(*[r.at[g] for r in per_group_in], *shared, *[r.at[g] for r in outs])


def _s5_prep_group(pcol_ref, prow_ref, b_re_ref, b_im_ref, bt_re_ref, bt_im_ref,
                   ct_re_ref, ct_im_ref, sel_ref, reps_ref,
                   m_ref, wst_ref, wo_ref, apa_ref, apb_ref):
    def zoh_coeff(a_re, a_im, lr, li):
        xr = a_re - 1.0
        den = lr * lr + li * li
        return (xr * lr + a_im * li) / den, (a_im * lr - xr * li) / den

    def zoh(lr, li, ls):
        step = jnp.exp(ls)
        mag = jnp.exp(lr * step)
        ang = li * step
        return zoh_coeff(mag * jnp.cos(ang), mag * jnp.sin(ang), lr, li) + (step,)

    pc = pcol_ref[...]
    lr, li = pc[:, 0:1], pc[:, 1:2]
    step = jnp.exp(pc[:, 2:3])
    n = lax.broadcasted_iota(jnp.int32, (SSM_STATE, LANES), 1).astype(F32)
    mag = jnp.exp(lr * step * n)
    th = li * step * n
    pw_re, pw_im = mag * jnp.cos(th), mag * jnp.sin(th)
    cfr, cfi = zoh_coeff(pw_re[:, 1:2], pw_im[:, 1:2], lr, li)
    b_re, b_im = b_re_ref[...], b_im_ref[...]
    bb_re = cfr * b_re - cfi * b_im
    bb_im = cfr * b_im + cfi * b_re

    pr = prow_ref[...]
    lr_row, li_row, ls_row = pr[0:1], pr[1:2], pr[2:3]
    cfr_row, cfi_row, step_row = zoh(lr_row, li_row, ls_row)
    bt_re, bt_im = bt_re_ref[...], bt_im_ref[...]
    bbt_re = cfr_row[:, :SSM_STATE] * bt_re - cfi_row[:, :SSM_STATE] * bt_im
    bbt_im = cfr_row[:, :SSM_STATE] * bt_im + cfi_row[:, :SSM_STATE] * bt_re

    sel = sel_ref[...]
    brep_re, brep_im = _select_cols(bb_re, sel), _select_cols(bb_im, sel)
    crep_re, crep_im = _select_cols(ct_re_ref[...], sel), _select_cols(ct_im_ref[...], sel)

    apow = lambda i: (_select_cols(pw_re, reps_ref[i]), _select_cols(pw_im, reps_ref[i]))

    e_re, e_im = apow(0)
    g_re = e_re * crep_re - e_im * crep_im
    g_im = e_re * crep_im + e_im * crep_re
    kt = _dot_exact(bbt_re, g_re) - _dot_exact(bbt_im, g_im)
    klane = lax.broadcasted_iota(jnp.int32, kt.shape, 1)
    m_ref[0:SSM_GROUP, :] = kt.astype(BF16)
    for s in range(1, SSM_CHUNK):
        shifted = jnp.where(klane >= s * SSM_GROUP, pltpu.roll(kt, s * SSM_GROUP, axis=1), 0.0)
        m_ref[s * SSM_GROUP:(s + 1) * SSM_GROUP, :] = shifted.astype(BF16)

    e_re, e_im = apow(1)
    wst_ref[0:SSM_STATE, :] = (e_re * brep_re - e_im * brep_im).astype(BF16)
    wst_ref[SSM_STATE:, :] = (e_re * brep_im + e_im * brep_re).astype(BF16)

    e_re, e_im = apow(2)
    wo_ref[0:SSM_STATE, :] = (e_re * crep_re - e_im * crep_im).astype(BF16)
    wo_ref[SSM_STATE:, :] = (-(e_re * crep_im + e_im * crep_re)).astype(BF16)

    krow = lax.broadcasted_iota(jnp.int32, (8, LANES), 0)
    klan = lax.broadcasted_iota(jnp.int32, (8, LANES), 1)
    nn = (SSM_CHUNK * jnp.left_shift(1, krow)).astype(F32)
    mag = jnp.exp(lr_row * step_row * nn)
    th = li_row * step_row * nn
    p_re, p_im = mag * jnp.cos(th), mag * jnp.sin(th)
    apa_ref[...] = p_re
    apb_ref[...] = jnp.where(klan < SSM_STATE, -p_im, p_im)


def _s5_prep(lam_re, lam_im, log_step, b_re, b_im, c_re, c_im):
    nl, g, p = lam_re.shape
    hh = SSM_GROUP
    ls = jnp.broadcast_to(log_step[..., None], (nl, g, p))
    params = jnp.stack([lam_re, lam_im, ls], axis=-1)
    pcol = jnp.pad(params, ((0, 0), (0, 0), (0, 0), (0, 5)))
    dup = lambda v: jnp.concatenate([v, v], axis=-1)
    prow = jnp.pad(jnp.stack([dup(lam_re), dup(lam_im), dup(ls)], axis=2),
                   ((0, 0), (0, 0), (0, 5), (0, 0)))
    sel = jnp.tile(jnp.eye(hh, dtype=BF16), (1, SSM_CHUNK))
    tau = jnp.arange(CHUNK_W) // hh
    nrow = jnp.arange(LANES)[:, None]
    reps = jnp.stack([nrow == tau, nrow == SSM_CHUNK - 1 - tau, nrow == tau + 1]).astype(BF16)
    blk = lambda *s: pl.BlockSpec((None, S5_PREP_GROUPS) + s, lambda l, i: (l, i) + (0,) * len(s))
    return pl.pallas_call(
        _s5_prep_kernel,
        grid=(nl, g // S5_PREP_GROUPS),
        in_specs=[blk(p, 8), blk(8, LANES), blk(p, hh), blk(p, hh), blk(hh, p), blk(hh, p),
                  blk(p, hh), blk(p, hh), _const_spec(hh, CHUNK_W), _const_spec(3, LANES, CHUNK_W)],
        out_specs=[blk(CHUNK_W, CHUNK_W), blk(2 * p, CHUNK_W), blk(2 * p, CHUNK_W),
                   blk(8, LANES), blk(8, LANES)],
        out_shape=[jax.ShapeDtypeStruct((nl, g, CHUNK_W, CHUNK_W), BF16),
                   jax.ShapeDtypeStruct((nl, g, 2 * p, CHUNK_W), BF16),
                   jax.ShapeDtypeStruct((nl, g, 2 * p, CHUNK_W), BF16),
                   jax.ShapeDtypeStruct((nl, g, 8, LANES), F32),
                   jax.ShapeDtypeStruct((nl, g, 8, LANES), F32)],
        compiler_params=_params("parallel", "parallel"),
        name="s5_prep",
    )(pcol, prow, b_re, b_im, jnp.swapaxes(b_re, -1, -2), jnp.swapaxes(b_im, -1, -2),
      jnp.swapaxes(c_re, -1, -2), jnp.swapaxes(c_im, -1, -2), sel, reps)


def _s5_mix_kernel(u_ref, perm_ref, m_ref, wst_ref, wo_ref, apa_ref, apb_ref, dx_ref, y_ref,
                   *, nsteps):
    nb, rows, _ = u_ref.shape
    cps = rows // CHUNK_PITCH
    nc = nb * cps
    gpt = GROUPS_PER_TILE
    nslab = SSM_CHUNK // gpt
    perm = perm_ref[...]

    def slab(j):
        return jnp.concatenate([
            jnp.concatenate([u_ref[b, pl.ds(gpt * j + s, cps, stride=CHUNK_PITCH), :] for s in range(gpt)], axis=1)
            for b in range(nb)], axis=0)

    z = _dot(jnp.concatenate([slab(j) for j in range(nslab)], axis=0).astype(BF16), perm).astype(BF16)

    c = lax.broadcasted_iota(jnp.int32, (nc, 2 * SSM_STATE), 0) & (cps - 1)
    ys = []
    for g in range(gpt):
        x = jnp.concatenate([z[j * nc:(j + 1) * nc, g * LANES:(g + 1) * LANES] for j in range(nslab)],
                            axis=1)
        y = _dot(x, m_ref[g])
        st = _dot_nt(x, wst_ref[g])
        apa, apb = apa_ref[g], apb_ref[g]
        for k in range(nsteps):
            d = 1 << k
            sh = jnp.where(c >= d, pltpu.roll(st, d, axis=0), 0.0)
            st = st + sh * apa[k:k + 1] + pltpu.roll(sh, SSM_STATE, axis=1) * apb[k:k + 1]
        carried = jnp.where(c >= 1, pltpu.roll(st, 1, axis=0), 0.0)
        y = y + _dot(carried.astype(BF16), wo_ref[g]) + dx_ref[g] * x.astype(F32)
        ys.append(jax.nn.gelu(y).astype(BF16))

    w = jnp.concatenate([
        jnp.concatenate([ys[g][:, j * LANES:(j + 1) * LANES] for g in range(gpt)], axis=1)
        for j in range(nslab)], axis=0)
    o = _dot(w, perm)
    for b in range(nb):
        for j in range(nslab):
            for t in range(gpt):
                y_ref[b, pl.ds(gpt * j + t, cps, stride=CHUNK_PITCH), :] = (
                    o[j * nc + b * cps:j * nc + (b + 1) * cps, t * LANES:(t + 1) * LANES])
        for r in range(SSM_CHUNK, CHUNK_PITCH):
            y_ref[b, pl.ds(r, cps, stride=CHUNK_PITCH), :] = jnp.zeros((cps, LANES), F32)


def _s5_mix(l, u, prep, dx, perm):
    tiles, bsz, rows, _ = u.shape
    cps = rows // CHUNK_PITCH
    assert cps & (cps - 1) == 0 and cps <= 256
    gpt = GROUPS_PER_TILE
    nb = 2 if bsz % 2 == 0 else 1
    m, wst, wo, apa, apb = prep
    tokens = pl.BlockSpec((None, nb, rows, LANES), lambda t, b: (t, b, 0, 0))
    grp = lambda *s: pl.BlockSpec((None, gpt) + s, lambda t, b: (l, t) + (0,) * len(s))
    return pl.pallas_call(
        functools.partial(_s5_mix_kernel, nsteps=cps.bit_length() - 1),
        grid=(tiles, bsz // nb),
        in_specs=[tokens, _const_spec(gpt * LANES, gpt * LANES), grp(CHUNK_W, CHUNK_W),
                  grp(2 * SSM_STATE, CHUNK_W), grp(2 * SSM_STATE, CHUNK_W), grp(8, LANES), grp(8, LANES),
                  grp(1, CHUNK_W)],
        out_specs=tokens,
        out_shape=jax.ShapeDtypeStruct(u.shape, F32),
        compiler_params=_params("parallel", "parallel"),
        name="s5_mix",
    )(u, perm, m, wst, wo, apa, apb, dx)


FULL = 0


def _flash_kernel(qt_ref, kt_ref, kind_ref, q_ref, k_ref, v_ref, o_ref, m_scr, acc_scr, *, heads):
    p_idx = pl.program_id(2)
    ki = kt_ref[p_idx]
    kind = kind_ref[p_idx]
    tq, tk = q_ref.shape[2], k_ref.shape[2]

    @pl.when(ki == 0)
    def _():
        m_scr[...] = jnp.full(m_scr.shape, -jnp.inf, F32)
        acc_scr[...] = jnp.zeros(acc_scr.shape, F32)

    def attend(j, nk, diag_offset):
        s = _dot_nt(q_ref[0, j], k_ref[0, j, 0:nk, :])
        if diag_offset is not None:
            row = lax.broadcasted_iota(jnp.int32, s.shape, 0)
            col = lax.broadcasted_iota(jnp.int32, s.shape, 1)
            s = jnp.where(col <= row + diag_offset, s, -jnp.inf)
        m_prev = m_scr[j]
        m_new = jnp.maximum(m_prev, jnp.max(s, axis=-1, keepdims=True))
        alpha = jnp.exp2(m_prev - m_new)
        p = jnp.exp2(s - jnp.tile(m_new, (1, nk // LANES)))
        acc_scr[j] = alpha * acc_scr[j] + _dot(p.astype(BF16), v_ref[0, j, 0:nk, :])
        m_scr[j] = m_new

    def all_heads(nk, diag_offset):
        def group(g, carry):
            for j in range(FLASH_UNROLL):
                attend(g * FLASH_UNROLL + j, nk, diag_offset)
            return carry
        lax.fori_loop(0, heads // FLASH_UNROLL, group, 0)

    @pl.when(kind == FULL)
    def _():
        all_heads(tk, None)

    for r in range(1, tk // tq + 1):
        @pl.when(kind == r)
        def _(r=r):
            all_heads(r * tq, (r - 1) * tq)

    @pl.when(kind != FULL)
    def _():
        lane = lax.broadcasted_iota(jnp.int32, (tq, HEAD_PAD), 1)
        outs = []
        for j in range(0, heads, 2):
            even, odd = acc_scr[j], acc_scr[j + 1]
            pv = jnp.where(lane < V_DIM, even, odd)
            sums = pltpu.roll(jnp.where(lane < V_DIM, odd, even), V_DIM, axis=1)
            outs.append(pv / sums)
        o_ref[0] = jnp.concatenate(outs, axis=-1).astype(BF16)


def _flash(q, k, v, tq, heads_per_step):
    bsz, heads, seq, _ = q.shape
    hp = heads_per_step
    ratio = FLASH_KEY_RATIO
    tk = ratio * tq
    assert seq % tk == 0
    steps = []
    for i in range(seq // tq):
        steps += [(i, j, FULL) for j in range(i // ratio)] + [(i, i // ratio, i % ratio + 1)]
    qt, kt, kind = (jnp.asarray(col, jnp.int32) for col in zip(*steps))
    qspec = pl.BlockSpec((1, hp, tq, HEAD_PAD), lambda b, h, p, qt, kt, kind: (b, h, qt[p], 0))
    kspec = pl.BlockSpec((1, hp, tk, HEAD_PAD), lambda b, h, p, qt, kt, kind: (b, h, kt[p], 0))
    grid_spec = pltpu.PrefetchScalarGridSpec(
        num_scalar_prefetch=3,
        grid=(bsz, heads // hp, len(steps)),
        in_specs=[qspec, kspec, kspec],
        out_specs=pl.BlockSpec((1, tq, hp * V_DIM), lambda b, h, p, qt, kt, kind: (b, qt[p], h)),
        scratch_shapes=[pltpu.VMEM((hp, tq, LANES), F32), pltpu.VMEM((hp, tq, HEAD_PAD), F32)])
    return pl.pallas_call(
        functools.partial(_flash_kernel, heads=hp),
        grid_spec=grid_spec,
        out_shape=jax.ShapeDtypeStruct((bsz, seq, heads * V_DIM), BF16),
        compiler_params=_params("parallel", "parallel", "arbitrary"),
        name="flash",
    )(qt, kt, kind, q, k, v)


def _head_sums(v2, lane, width):
    out = jnp.zeros_like(v2)
    for hd in range(v2.shape[-1] // width):
        msk = (lane >= hd * width) & (lane < (hd + 1) * width)
        out = out + jnp.where(msk, jnp.sum(jnp.where(msk, v2, 0.0), axis=-1, keepdims=True), 0.0)
    return out


def _mem_prep_kernel(mem_ref, g_ref, w_ref, kg_ref, kt_ref, vm_ref):
    hm = _rms(mem_ref[0], g_ref[0]).astype(BF16)
    kv = _dot(hm, w_ref[0])
    k, v = kv[:, :MEM_WIDTH], kv[:, MEM_WIDTH:]
    lane = lax.broadcasted_iota(jnp.int32, k.shape, 1)
    ss = _head_sums(k * k, lane, MEM_HEAD_DIM)
    kn = k * lax.rsqrt(ss * (1.0 / MEM_HEAD_DIM) + EPS) * kg_ref[0] * (math.log2(math.e) / math.sqrt(MEM_HEAD_DIM))
    knt = kn.T
    row = lax.broadcasted_iota(jnp.int32, knt.shape, 0)
    for hd in range(MEM_HEADS):
        lo, hi = hd * MEM_HEAD_DIM, (hd + 1) * MEM_HEAD_DIM
        kt_ref[0, 0, hd] = jnp.where((row >= lo) & (row < hi), knt, 0.0).astype(BF16)
        vm_ref[0, 0, hd] = jnp.where((lane >= lo) & (lane < hi), v, 0.0).astype(BF16)


def _mem_prep(mem, g, w_kv, k_gain):
    nl = g.shape[0]
    bsz = mem.shape[0]
    out = jax.ShapeDtypeStruct((nl, bsz, MEM_HEADS, N_MEM, MEM_WIDTH), BF16)
    return pl.pallas_call(
        _mem_prep_kernel,
        grid=(nl, bsz),
        in_specs=[pl.BlockSpec((1, N_MEM, D_MODEL), lambda l, b: (b, 0, 0)),
                  pl.BlockSpec((1, 1, D_MODEL), lambda l, b: (l, 0, 0)),
                  pl.BlockSpec((1, D_MODEL, 2 * MEM_WIDTH), lambda l, b: (l, 0, 0)),
                  pl.BlockSpec((1, 1, MEM_WIDTH), lambda l, b: (l, 0, 0))],
        out_specs=[pl.BlockSpec((1, 1, MEM_HEADS, MEM_WIDTH, N_MEM), lambda l, b: (l, b, 0, 0, 0)),
                   pl.BlockSpec((1, 1, MEM_HEADS, N_MEM, MEM_WIDTH), lambda l, b: (l, b, 0, 0, 0))],
        out_shape=[out, out],
        compiler_params=_params("parallel", "parallel"),
        name="mem_prep",
    )(mem, g, w_kv, k_gain)


def _outproj_kernel(x_ref, ys_ref, om_ref, wglu_ref, bglu_ref, wout_ref,
                    wmq_ref, qg_ref, bdm_ref, kt_ref, vm_ref, wmo_ref,
                    x2_ref, h3_ref):
    nchunk = x_ref.shape[0] // SSM_CHUNK
    ys = jnp.concatenate([
        jnp.concatenate([ys_ref[t, c * CHUNK_PITCH:c * CHUNK_PITCH + SSM_CHUNK, :] for c in range(nchunk)], axis=0)
        for t in range(LANE_TILES)], axis=1)
    yg = ys * jax.nn.sigmoid(_dot(ys.astype(BF16), wglu_ref[...]) + bglu_ref[...])
    n1 = _rms(yg).astype(BF16)
    n2 = _rms(om_ref[...].astype(F32)).astype(BF16)
    x1 = x_ref[...] + _dot(n1, wout_ref[:SSM_WIDTH, :]) + _dot(n2, wout_ref[SSM_WIDTH:, :])

    r1 = lax.rsqrt(jnp.mean(x1 * x1, axis=-1, keepdims=True) + EPS)
    q = _dot(x1.astype(BF16), wmq_ref[...]) * r1
    ss = _dot((q * q).astype(BF16), bdm_ref[...])
    qn = (q * lax.rsqrt(ss * (1.0 / MEM_HEAD_DIM) + EPS) * qg_ref[...]).astype(BF16)
    o = jnp.zeros(q.shape, F32)
    for hd in range(MEM_HEADS):
        s = _dot(qn, kt_ref[hd])
        p = jnp.exp2(s - jnp.max(s, axis=-1, keepdims=True))
        inv = 1.0 / jnp.sum(p, axis=-1, keepdims=True)
        o = o + _dot(p.astype(BF16), vm_ref[hd]) * inv
    x2 = x1 + _dot(o.astype(BF16), wmo_ref[...])
    x2_ref[...] = x2
    h3_ref[...] = _rms(x2).astype(BF16)


def _outproj(l, x, ys, om, w, kt, vm, tm):
    bsz, seq, _ = x.shape
    tok = lambda width: pl.BlockSpec((None, tm, width), lambda b, i: (b, i, 0))
    memb = pl.BlockSpec((None, None, MEM_HEADS, N_MEM, MEM_WIDTH), lambda b, i: (l, b, 0, 0, 0))
    return pl.pallas_call(
        _outproj_kernel,
        grid=(bsz, seq // tm),
        in_specs=[tok(D_MODEL), pl.BlockSpec((LANE_TILES, None, _pitched(tm), LANES), lambda b, i: (0, b, i, 0)),
                  tok(MLA_WIDTH),
                  _layer_spec(l, SSM_WIDTH, SSM_WIDTH), _layer_spec(l, 1, SSM_WIDTH),
                  _layer_spec(l, D_MODEL, D_MODEL),
                  _layer_spec(l, D_MODEL, MEM_WIDTH), _layer_spec(l, 1, MEM_WIDTH),
                  _const_spec(MEM_WIDTH, MEM_WIDTH),
                  memb, memb, _layer_spec(l, MEM_WIDTH, D_MODEL)],
        out_specs=[tok(D_MODEL), tok(D_MODEL)],
        out_shape=[jax.ShapeDtypeStruct((bsz, seq, D_MODEL), F32),
                   jax.ShapeDtypeStruct((bsz, seq, D_MODEL), BF16)],
        compiler_params=_params("parallel", "parallel"),
        name="outproj",
    )(x, ys, om, w["w_glu"], w["b_glu"], w["w_out"],
      w["w_mq"], w["mem_q_gain"], w["bd_mem"], kt, vm, w["w_mo"])


def _mlp_kernel(x_ref, h_ref, w1_ref, w2_ref, o_ref, *, ff_tile):
    h = h_ref[...]
    acc = x_ref[...]
    for c in range(D_FF // ff_tile):
        a = jnp.maximum(_dot(h, w1_ref[:, c * ff_tile:(c + 1) * ff_tile]), 0.0)
        acc = acc + _dot((a * a).astype(BF16), w2_ref[c * ff_tile:(c + 1) * ff_tile, :])
    o_ref[...] = acc


def _mlp(l, x, h, w1, w2, tm):
    t = x.shape[0]
    tok = pl.BlockSpec((tm, D_MODEL), lambda i: (i, 0))
    return pl.pallas_call(
        functools.partial(_mlp_kernel, ff_tile=1024),
        grid=(t // tm,),
        in_specs=[tok, tok,
                  pl.BlockSpec((None, D_MODEL, D_FF), lambda i: (l, 0, 0), pipeline_mode=pl.Buffered(1)),
                  pl.BlockSpec((None, D_FF, D_MODEL), lambda i: (l, 0, 0), pipeline_mode=pl.Buffered(1))],
        out_specs=tok,
        out_shape=jax.ShapeDtypeStruct((t, D_MODEL), F32),
        compiler_params=_params("parallel"),
        name="mlp",
    )(x, h, w1, w2)


def _half_swap(w):
    h = w.shape[-1] // 2
    return jnp.concatenate([-w[..., h:], w[..., :h]], axis=-1)


def _half_swap_unsigned(w):
    h = w.shape[-1] // 2
    return jnp.concatenate([w[..., h:], w[..., :h]], axis=-1)


def _pad_last(w, before, after):
    pads = [(0, 0)] * (w.ndim - 1) + [(before, after)]
    return jnp.pad(w, pads)


def _trunk(tm, tq, x, mem, positions, norm_mix, w_in, ssm_lambda_re, ssm_lambda_im, ssm_log_step, ssm_b_re, ssm_b_im, ssm_c_re, ssm_c_im, ssm_d, ssm_w_glu, ssm_b_glu, mla_q_norm, mla_w_uq, mla_kv_norm, mla_w_ukv, mla_q_gain, mla_k_gain, out_norm_ssm, out_norm_mla, w_out, norm_mem_q, norm_mem_kv, mem_w_q, mem_w_kv, mem_q_gain, mem_k_gain, mem_w_o, norm_mlp, mlp_w1, mlp_w2):
    bsz, seq, _ = x.shape
    depth = norm_mix.shape[0]
    row = lambda a: a[:, None, :]
    tail = HEAD_PAD - QK_DIM

    s3 = SSM_WIDTH + Q_LORA + KV_LORA
    rows = lambda g: g[:, :, None]
    w_in_g = (rows(norm_mix) * w_in).astype(BF16)
    k_rope_w = w_in_g[..., s3:]
    w_in_x = jnp.concatenate([w_in_g[..., :s3], _pad_last(k_rope_w, QK_NOPE, tail),
                              _pad_last(_half_swap(k_rope_w), QK_NOPE, tail)], axis=-1)
    wq = mla_w_uq.reshape(depth, Q_LORA, MLA_HEADS, QK_DIM)
    hw = MLA_HEADS * HEAD_PAD
    gq = rows(mla_q_norm)
    w_qa = (gq * _pad_last(wq, 0, tail).reshape(depth, Q_LORA, hw)).astype(BF16)
    w_qb = (gq * _pad_last(_half_swap(wq[..., QK_NOPE:]), QK_NOPE, tail).reshape(depth, Q_LORA, hw)).astype(BF16)
    wkv = mla_w_ukv.reshape(depth, KV_LORA, MLA_HEADS, QK_NOPE + V_DIM)
    gkv = rows(mla_kv_norm)
    w_k = (gkv * _pad_last(wkv[..., :QK_NOPE], 0, HEAD_PAD - QK_NOPE).reshape(depth, KV_LORA, hw)).astype(BF16)
    wv = wkv[..., QK_NOPE:]
    wv = jnp.stack([_pad_last(wv[:, :, 0::2], 0, HEAD_PAD - V_DIM), _pad_last(wv[:, :, 1::2], HEAD_PAD - V_DIM, 0)],
                   axis=3)
    w_v = (gkv * wv.reshape(depth, KV_LORA, hw)).astype(BF16)
    q_rope_g, k_rope_g = mla_q_gain[:, QK_NOPE:], mla_k_gain[:, QK_NOPE:]
    mla_gains = jnp.stack([
        _pad_last(mla_q_gain, 0, tail),
        _pad_last(_half_swap_unsigned(q_rope_g), QK_NOPE, tail),
        _pad_last(mla_k_gain[:, :QK_NOPE], 0, HEAD_PAD - QK_NOPE),
        _pad_last(k_rope_g, QK_NOPE, tail),
        _pad_last(_half_swap_unsigned(k_rope_g), QK_NOPE, tail)], axis=1)
    mla_gains = jnp.pad(mla_gains, ((0, 0), (0, 3), (0, 0)))
    blk = jnp.arange(2 * HEAD_PAD) // HEAD_PAD
    bd = (blk[:, None] == blk[None, :]).astype(BF16)
    mblk = jnp.arange(MEM_WIDTH) // MEM_HEAD_DIM
    bd_mem = (mblk[:, None] == mblk[None, :]).astype(BF16)
    wmkv = mem_w_kv.reshape(depth, D_MODEL, MEM_HEADS, 2, MEM_HEAD_DIM)
    w_mkv = wmkv.transpose(0, 1, 3, 2, 4).reshape(depth, D_MODEL, 2 * MEM_WIDTH).astype(BF16)
    idx = jnp.arange(GROUPS_PER_TILE * LANES)
    dest = ((idx // SSM_GROUP) % GROUPS_PER_TILE) * LANES + (idx // LANES) * SSM_GROUP + idx % SSM_GROUP
    perm = (dest[:, None] == idx[None, :]).astype(BF16)

    g_out = jnp.concatenate([out_norm_ssm, out_norm_mla], axis=-1)
    w = dict(w_in=w_in_x, w_qa=w_qa, w_qb=w_qb, w_k=w_k, w_v=w_v, mla_gains=mla_gains, bd=bd, bd_mem=bd_mem,
             w_glu=ssm_w_glu.astype(BF16), b_glu=row(ssm_b_glu),
             w_out=(rows(g_out) * w_out).astype(BF16),
             w_mq=(rows(norm_mem_q) * mem_w_q).astype(BF16),
             mem_q_gain=row(jnp.tile(mem_q_gain, (1, MEM_HEADS))),
             w_mo=mem_w_o.astype(BF16))
    w1_b = (rows(norm_mlp) * mlp_w1).astype(BF16)
    w2_b = mlp_w2.astype(BF16)

    ra, rb = _rope_tables(positions)
    prep = _s5_prep(ssm_lambda_re, ssm_lambda_im, ssm_log_step, ssm_b_re, ssm_b_im, ssm_c_re, ssm_c_im)
    kt_all, vm_all = _mem_prep(mem, row(norm_mem_kv), w_mkv, row(jnp.tile(mem_k_gain, (1, MEM_HEADS))))
    dx = jnp.tile(ssm_d.reshape(depth, SSM_GROUPS, 1, SSM_GROUP), (1, 1, 1, SSM_CHUNK))

    for l in range(depth):
        u, q, k, v = _inproj(l, x, w, ra, rb, min(2 * tm, seq))
        ys = _s5_mix(l, u, prep, dx, perm)
        om = _flash(q, k, v, tq, FLASH_HEADS)
        x2, h3 = _outproj(l, x, ys, om, w, kt_all, vm_all, min(2 * tm, seq))
        x = _mlp(l, x2.reshape(bsz * seq, D_MODEL), h3.reshape(bsz * seq, D_MODEL),
                 w1_b, w2_b, min(2 * tm, seq)).reshape(bsz, seq, D_MODEL)
    return x


def kernel(x, mem, positions, norm_mix, w_in, ssm_lambda_re, ssm_lambda_im, ssm_log_step, ssm_b_re, ssm_b_im, ssm_c_re, ssm_c_im, ssm_d, ssm_w_glu, ssm_b_glu, mla_q_norm, mla_w_uq, mla_kv_norm, mla_w_ukv, mla_q_gain, mla_k_gain, out_norm_ssm, out_norm_mla, w_out, norm_mem_q, norm_mem_kv, mem_w_q, mem_w_kv, mem_q_gain, mem_k_gain, mem_w_o, norm_mlp, mlp_w1, mlp_w2):
    seq = x.shape[1]
    return _trunk(min(512, seq), min(512, seq // 2), x, mem, positions, norm_mix, w_in, ssm_lambda_re, ssm_lambda_im, ssm_log_step, ssm_b_re, ssm_b_im, ssm_c_re, ssm_c_im, ssm_d, ssm_w_glu, ssm_b_glu, mla_q_norm, mla_w_uq, mla_kv_norm, mla_w_ukv, mla_q_gain, mla_k_gain, out_norm_ssm, out_norm_mla, w_out, norm_mem_q, norm_mem_kv, mem_w_q, mem_w_kv, mem_q_gain, mem_k_gain, mem_w_o, norm_mlp, mlp_w1, mlp_w2)
```

```python
import functools
import math

import jax
import jax.numpy as jnp
from jax import lax
from jax.experimental import pallas as pl
from jax.experimental.pallas import tpu as pltpu

D_MODEL = 1024
N_MEM = 256
MEM_HEADS = 4
MEM_HEAD_DIM = 64
MEM_WIDTH = MEM_HEADS * MEM_HEAD_DIM
SSM_WIDTH = 512
MLA_WIDTH = 512
SSM_GROUP = 16
SSM_GROUPS = 32
SSM_STATE = 64
MLA_HEADS = 8
QK_NOPE = 64
QK_ROPE = 32
QK_DIM = QK_NOPE + QK_ROPE
V_DIM = 64
Q_LORA = 256
KV_LORA = 128
ROPE_THETA = 10000.0
D_FF = 4 * D_MODEL
EPS = 1e-6

LANES = 128
HEAD_PAD = 128
SSM_CHUNK = 32
CHUNK_W = SSM_CHUNK * SSM_GROUP
CHUNK_PITCH = SSM_CHUNK + 4
GROUPS_PER_TILE = LANES // SSM_GROUP
LANE_TILES = SSM_WIDTH // LANES
IN_COLS_PAD = SSM_WIDTH + Q_LORA + KV_LORA + 2 * LANES
FLASH_UNROLL = 4
FLASH_KEY_RATIO = 4
FLASH_HEADS = 8
VMEM_LIMIT = 56 * 1024 * 1024

F32 = jnp.float32
BF16 = jnp.bfloat16
HIGHEST = lax.Precision.HIGHEST


def _dot(a, b):
    return jnp.dot(a, b, preferred_element_type=F32)


def _dot_exact(a, b):
    return jnp.dot(a, b, precision=HIGHEST, preferred_element_type=F32)


def _select_cols(a, onehot):
    hi = a.astype(BF16)
    lo = (a - hi.astype(F32)).astype(BF16)
    return _dot(hi, onehot) + _dot(lo, onehot)


def _dot_nt(a, b):
    return lax.dot_general(a, b, (((1,), (1,)), ((), ())), preferred_element_type=F32)


def _rms(v, gain=None):
    y = v * lax.rsqrt(jnp.mean(v * v, axis=-1, keepdims=True) + EPS)
    return y if gain is None else y * gain


def _params(*sem, flags=None):
    return pltpu.CompilerParams(dimension_semantics=sem, vmem_limit_bytes=VMEM_LIMIT, flags=flags)


def _layer_spec(l, *shape):
    return pl.BlockSpec((None,) + shape, lambda *_: (l,) + (0,) * len(shape))


def _pitched(tokens):
    return tokens // SSM_CHUNK * CHUNK_PITCH


def _const_spec(*shape):
    return pl.BlockSpec(shape, lambda *_: (0,) * len(shape))


ROPE_TOKENS_PER_ROW = LANES // QK_ROPE


def _rope_table_kernel(pos_ref, freq_ref, a_ref, b_ref):
    ang = pos_ref[...].astype(F32) * freq_ref[...]
    cos, sin = jnp.cos(ang), jnp.sin(ang)
    rows = ang.shape[0]
    lane = lax.broadcasted_iota(jnp.int32, ang.shape, 1)
    in_rope = (lane >= QK_NOPE) & (lane < QK_DIM)
    for j in range(ROPE_TOKENS_PER_ROW):
        shift = (QK_NOPE - QK_ROPE * j) % LANES
        cj = pltpu.roll(cos, shift, axis=1) if shift else cos
        sj = pltpu.roll(sin, shift, axis=1) if shift else sin
        a_ref[pl.ds(j, rows, stride=ROPE_TOKENS_PER_ROW), :] = (
            jnp.where(lane < QK_NOPE, 1.0, jnp.where(in_rope, cj, 0.0)))
        b_ref[pl.ds(j, rows, stride=ROPE_TOKENS_PER_ROW), :] = jnp.where(in_rope, sj, 0.0)


def _rope_tables(positions):
    bsz, seq = positions.shape
    t = bsz * seq
    tm = min(1024, t)
    per_row = ROPE_TOKENS_PER_ROW
    half = QK_ROPE // 2
    inv_freq = ROPE_THETA ** (-jnp.arange(half, dtype=F32) / half)
    freq = jnp.tile(inv_freq, LANES // half)[None, :]
    pos = jnp.repeat(positions.reshape(t // per_row, per_row), QK_ROPE, axis=1)
    a, b = pl.pallas_call(
        _rope_table_kernel,
        grid=(t // tm,),
        in_specs=[pl.BlockSpec((tm // per_row, LANES), lambda i: (i, 0)), _const_spec(1, LANES)],
        out_specs=[pl.BlockSpec((tm, LANES), lambda i: (i, 0))] * 2,
        out_shape=[jax.ShapeDtypeStruct((t, LANES), F32)] * 2,
        compiler_params=_params("parallel"),
        name="rope_tables",
    )(pos, freq)
    return a.reshape(bsz, seq, LANES), b.reshape(bsz, seq, LANES)


def _inproj_kernel(x_ref, win_ref, wqa_ref, wqb_ref, wk_ref, wv_ref,
                   gains_ref, bd_ref, ra_ref, rb_ref, u_ref, q_ref, k_ref, v_ref):
    h = _rms(x_ref[...]).astype(BF16)
    proj = _dot(h, win_ref[...])
    pad_rows = jnp.zeros((CHUNK_PITCH - SSM_CHUNK, LANES), F32)
    for t in range(LANE_TILES):
        for c in range(proj.shape[0] // SSM_CHUNK):
            u_ref[t, c * CHUNK_PITCH:c * CHUNK_PITCH + SSM_CHUNK, :] = (
                proj[c * SSM_CHUNK:(c + 1) * SSM_CHUNK, t * LANES:(t + 1) * LANES])
            u_ref[t, c * CHUNK_PITCH + SSM_CHUNK:(c + 1) * CHUNK_PITCH, :] = pad_rows

    two = lambda a: jnp.concatenate([a, a], axis=1)
    ra, rb = ra_ref[...], rb_ref[...]
    gains = gains_ref[...]
    bd = bd_ref[...]
    scale = math.log2(math.e) / math.sqrt(QK_DIM)
    inv_d = 1.0 / QK_DIM

    o1 = SSM_WIDTH
    o2 = o1 + Q_LORA
    o3 = o2 + KV_LORA
    hq = _rms(proj[:, o1:o2]).astype(BF16)
    qa = _dot(hq, wqa_ref[...])
    qb = _dot(hq, wqb_ref[...])
    hkv = _rms(proj[:, o2:o3]).astype(BF16)
    kn = _dot(hkv, wk_ref[...])
    vm = _dot(hkv, wv_ref[...])

    kr = proj[:, o3:o3 + LANES]
    kr_sw = proj[:, o3 + LANES:o3 + 2 * LANES]
    kr_ss = two(_dot((kr * kr).astype(BF16), bd[:LANES, :LANES]))
    krot = two(kr * (gains[3:4] * ra) + kr_sw * (gains[4:5] * rb))
    ga, gb, gk = two(ra * gains[0:1]), two(rb * gains[1:2]), two(gains[2:3])
    lane = lax.broadcasted_iota(jnp.int32, ga.shape, 1)
    ones_col = jnp.where((lane >= V_DIM) & (lane < HEAD_PAD + V_DIM), 1.0, 0.0)

    for hp in range(MLA_HEADS // 2):
        sl = slice(2 * hp * HEAD_PAD, (2 * hp + 2) * HEAD_PAD)
        q2 = qa[:, sl]
        rq = lax.rsqrt(_dot((q2 * q2).astype(BF16), bd) * inv_d + EPS) * scale
        qo = ((q2 * ga + qb[:, sl] * gb) * rq).astype(BF16)
        k2 = kn[:, sl]
        rk = lax.rsqrt((_dot((k2 * k2).astype(BF16), bd) + kr_ss) * inv_d + EPS)
        ko = ((k2 * gk + krot) * rk).astype(BF16)
        vo = (vm[:, sl] + ones_col).astype(BF16)
        for j in range(2):
            q_ref[2 * hp + j] = qo[:, j * HEAD_PAD:(j + 1) * HEAD_PAD]
            k_ref[2 * hp + j] = ko[:, j * HEAD_PAD:(j + 1) * HEAD_PAD]
            v_ref[2 * hp + j] = vo[:, j * HEAD_PAD:(j + 1) * HEAD_PAD]


def _inproj(l, x, w, ra, rb, tm):
    bsz, seq, _ = x.shape
    tok = lambda width: pl.BlockSpec((None, tm, width), lambda b, i: (b, i, 0))
    head = pl.BlockSpec((None, MLA_HEADS, tm, HEAD_PAD), lambda b, i: (b, 0, i, 0))
    hw = MLA_HEADS * HEAD_PAD
    head_shape = jax.ShapeDtypeStruct((bsz, MLA_HEADS, seq, HEAD_PAD), BF16)
    return pl.pallas_call(
        _inproj_kernel,
        grid=(bsz, seq // tm),
        in_specs=[tok(D_MODEL), _layer_spec(l, D_MODEL, IN_COLS_PAD),
                  _layer_spec(l, Q_LORA, hw), _layer_spec(l, Q_LORA, hw),
                  _layer_spec(l, KV_LORA, hw), _layer_spec(l, KV_LORA, hw),
                  _layer_spec(l, 8, HEAD_PAD), _const_spec(2 * HEAD_PAD, 2 * HEAD_PAD),
                  tok(LANES), tok(LANES)],
        out_specs=[pl.BlockSpec((LANE_TILES, None, _pitched(tm), LANES), lambda b, i: (0, b, i, 0)),
                   head, head, head],
        out_shape=[jax.ShapeDtypeStruct((LANE_TILES, bsz, _pitched(seq), LANES), F32),
                   head_shape, head_shape, head_shape],
        compiler_params=_params("parallel", "parallel"),
        name="inproj",
    )(x, w["w_in"], w["w_qa"], w["w_qb"], w["w_k"], w["w_v"], w["mla_gains"], w["bd"], ra, rb)


S5_PREP_GROUPS = 8


def _s5_prep_kernel(*refs):
    per_group_in, shared, outs = refs[:8], refs[8:10], refs[10:]
    for g in range(S5_PREP_GROUPS):
        _s5_prep_group(*[r.at[g] for r in per_group_in], *shared, *[r.at[g] for r in outs])


def _s5_prep_group(pcol_ref, prow_ref, b_re_ref, b_im_ref, bt_re_ref, bt_im_ref,
                   ct_re_ref, ct_im_ref, sel_ref, reps_ref,
                   m_ref, wst_ref, wo_ref, apa_ref, apb_ref):
    def zoh_coeff(a_re, a_im, lr, li):
        xr = a_re - 1.0
        den = lr * lr + li * li
        return (xr * lr + a_im * li) / den, (a_im * lr - xr * li) / den

    def zoh(lr, li, ls):
        step = jnp.exp(ls)
        mag = jnp.exp(lr * step)
        ang = li * step
        return zoh_coeff(mag * jnp.cos(ang), mag * jnp.sin(ang), lr, li) + (step,)

    pc = pcol_ref[...]
    lr, li = pc[:, 0:1], pc[:, 1:2]
    step = jnp.exp(pc[:, 2:3])
    n = lax.broadcasted_iota(jnp.int32, (SSM_STATE, LANES), 1).astype(F32)
    mag = jnp.exp(lr * step * n)
    th = li * step * n
    pw_re, pw_im = mag * jnp.cos(th), mag * jnp.sin(th)
    cfr, cfi = zoh_coeff(pw_re[:, 1:2], pw_im[:, 1:2], lr, li)
    b_re, b_im = b_re_ref[...], b_im_ref[...]
    bb_re = cfr * b_re - cfi * b_im
    bb_im = cfr * b_im + cfi * b_re

    pr = prow_ref[...]
    lr_row, li_row, ls_row = pr[0:1], pr[1:2], pr[2:3]
    cfr_row, cfi_row, step_row = zoh(lr_row, li_row, ls_row)
    bt_re, bt_im = bt_re_ref[...], bt_im_ref[...]
    bbt_re = cfr_row[:, :SSM_STATE] * bt_re - cfi_row[:, :SSM_STATE] * bt_im
    bbt_im = cfr_row[:, :SSM_STATE] * bt_im + cfi_row[:, :SSM_STATE] * bt_re

    sel = sel_ref[...]
    brep_re, brep_im = _select_cols(bb_re, sel), _select_cols(bb_im, sel)
    crep_re, crep_im = _select_cols(ct_re_ref[...], sel), _select_cols(ct_im_ref[...], sel)

    apow = lambda i: (_select_cols(pw_re, reps_ref[i]), _select_cols(pw_im, reps_ref[i]))

    e_re, e_im = apow(0)
    g_re = e_re * crep_re - e_im * crep_im
    g_im = e_re * crep_im + e_im * crep_re
    kt = _dot_exact(bbt_re, g_re) - _dot_exact(bbt_im, g_im)
    klane = lax.broadcasted_iota(jnp.int32, kt.shape, 1)
    m_ref[0:SSM_GROUP, :] = kt.astype(BF16)
    for s in range(1, SSM_CHUNK):
        shifted = jnp.where(klane >= s * SSM_GROUP, pltpu.roll(kt, s * SSM_GROUP, axis=1), 0.0)
        m_ref[s * SSM_GROUP:(s + 1) * SSM_GROUP, :] = shifted.astype(BF16)

    e_re, e_im = apow(1)
    wst_ref[0:SSM_STATE, :] = (e_re * brep_re - e_im * brep_im).astype(BF16)
    wst_ref[SSM_STATE:, :] = (e_re * brep_im + e_im * brep_re).astype(BF16)

    e_re, e_im = apow(2)
    wo_ref[0:SSM_STATE, :] = (e_re * crep_re - e_im * crep_im).astype(BF16)
    wo_ref[SSM_STATE:, :] = (-(e_re * crep_im + e_im * crep_re)).astype(BF16)

    krow = lax.broadcasted_iota(jnp.int32, (8, LANES), 0)
    klan = lax.broadcasted_iota(jnp.int32, (8, LANES), 1)
    nn = (SSM_CHUNK * jnp.left_shift(1, krow)).astype(F32)
    mag = jnp.exp(lr_row * step_row * nn)
    th = li_row * step_row * nn
    p_re, p_im = mag * jnp.cos(th), mag * jnp.sin(th)
    apa_ref[...] = p_re
    apb_ref[...] = jnp.where(klan < SSM_STATE, -p_im, p_im)


def _s5_prep(lam_re, lam_im, log_step, b_re, b_im, c_re, c_im):
    nl, g, p = lam_re.shape
    hh = SSM_GROUP
    ls = jnp.broadcast_to(log_step[..., None], (nl, g, p))
    params = jnp.stack([lam_re, lam_im, ls], axis=-1)
    pcol = jnp.pad(params, ((0, 0), (0, 0), (0, 0), (0, 5)))
    dup = lambda v: jnp.concatenate([v, v], axis=-1)
    prow = jnp.pad(jnp.stack([dup(lam_re), dup(lam_im), dup(ls)], axis=2),
                   ((0, 0), (0, 0), (0, 5), (0, 0)))
    sel = jnp.tile(jnp.eye(hh, dtype=BF16), (1, SSM_CHUNK))
    tau = jnp.arange(CHUNK_W) // hh
    nrow = jnp.arange(LANES)[:, None]
    reps = jnp.stack([nrow == tau, nrow == SSM_CHUNK - 1 - tau, nrow == tau + 1]).astype(BF16)
    blk = lambda *s: pl.BlockSpec((None, S5_PREP_GROUPS) + s, lambda l, i: (l, i) + (0,) * len(s))
    return pl.pallas_call(
        _s5_prep_kernel,
        grid=(nl, g // S5_PREP_GROUPS),
        in_specs=[blk(p, 8), blk(8, LANES), blk(p, hh), blk(p, hh), blk(hh, p), blk(hh, p),
                  blk(p, hh), blk(p, hh), _const_spec(hh, CHUNK_W), _const_spec(3, LANES, CHUNK_W)],
        out_specs=[blk(CHUNK_W, CHUNK_W), blk(2 * p, CHUNK_W), blk(2 * p, CHUNK_W),
                   blk(8, LANES), blk(8, LANES)],
        out_shape=[jax.ShapeDtypeStruct((nl, g, CHUNK_W, CHUNK_W), BF16),
                   jax.ShapeDtypeStruct((nl, g, 2 * p, CHUNK_W), BF16),
                   jax.ShapeDtypeStruct((nl, g, 2 * p, CHUNK_W), BF16),
                   jax.ShapeDtypeStruct((nl, g, 8, LANES), F32),
                   jax.ShapeDtypeStruct((nl, g, 8, LANES), F32)],
        compiler_params=_params("parallel", "parallel"),
        name="s5_prep",
    )(pcol, prow, b_re, b_im, jnp.swapaxes(b_re, -1, -2), jnp.swapaxes(b_im, -1, -2),
      jnp.swapaxes(c_re, -1, -2), jnp.swapaxes(c_im, -1, -2), sel, reps)


def _s5_mix_kernel(u_ref, perm_ref, m_ref, wst_ref, wo_ref, apa_ref, apb_ref, dx_ref, y_ref,
                   *, nsteps):
    nb, rows, _ = u_ref.shape
    cps = rows // CHUNK_PITCH
    nc = nb * cps
    gpt = GROUPS_PER_TILE
    nslab = SSM_CHUNK // gpt
    perm = perm_ref[...]

    def slab(j):
        return jnp.concatenate([
            jnp.concatenate([u_ref[b, pl.ds(gpt * j + s, cps, stride=CHUNK_PITCH), :] for s in range(gpt)], axis=1)
            for b in range(nb)], axis=0)

    z = _dot(jnp.concatenate([slab(j) for j in range(nslab)], axis=0).astype(BF16), perm).astype(BF16)

    c = lax.broadcasted_iota(jnp.int32, (nc, 2 * SSM_STATE), 0) & (cps - 1)
    ys = []
    for g in range(gpt):
        x = jnp.concatenate([z[j * nc:(j + 1) * nc, g * LANES:(g + 1) * LANES] for j in range(nslab)],
                            axis=1)
        y = _dot(x, m_ref[g])
        st = _dot_nt(x, wst_ref[g])
        apa, apb = apa_ref[g], apb_ref[g]
        for k in range(nsteps):
            d = 1 << k
            sh = jnp.where(c >= d, pltpu.roll(st, d, axis=0), 0.0)
            st = st + sh * apa[k:k + 1] + pltpu.roll(sh, SSM_STATE, axis=1) * apb[k:k + 1]
        carried = jnp.where(c >= 1, pltpu.roll(st, 1, axis=0), 0.0)
        y = y + _dot(carried.astype(BF16), wo_ref[g]) + dx_ref[g] * x.astype(F32)
        ys.append(jax.nn.gelu(y).astype(BF16))

    w = jnp.concatenate([
        jnp.concatenate([ys[g][:, j * LANES:(j + 1) * LANES] for g in range(gpt)], axis=1)
        for j in range(nslab)], axis=0)
    o = _dot(w, perm)
    for b in range(nb):
        for j in range(nslab):
            for t in range(gpt):
                y_ref[b, pl.ds(gpt * j + t, cps, stride=CHUNK_PITCH), :] = (
                    o[j * nc + b * cps:j * nc + (b + 1) * cps, t * LANES:(t + 1) * LANES])
        for r in range(SSM_CHUNK, CHUNK_PITCH):
            y_ref[b, pl.ds(r, cps, stride=CHUNK_PITCH), :] = jnp.zeros((cps, LANES), F32)


def _s5_mix(l, u, prep, dx, perm):
    tiles, bsz, rows, _ = u.shape
    cps = rows // CHUNK_PITCH
    assert cps & (cps - 1) == 0 and cps <= 256
    gpt = GROUPS_PER_TILE
    nb = 2 if bsz % 2 == 0 else 1
    m, wst, wo, apa, apb = prep
    tokens = pl.BlockSpec((None, nb, rows, LANES), lambda t, b: (t, b, 0, 0))
    grp = lambda *s: pl.BlockSpec((None, gpt) + s, lambda t, b: (l, t) + (0,) * len(s))
    return pl.pallas_call(
        functools.partial(_s5_mix_kernel, nsteps=cps.bit_length() - 1),
        grid=(tiles, bsz // nb),
        in_specs=[tokens, _const_spec(gpt * LANES, gpt * LANES), grp(CHUNK_W, CHUNK_W),
                  grp(2 * SSM_STATE, CHUNK_W), grp(2 * SSM_STATE, CHUNK_W), grp(8, LANES), grp(8, LANES),
                  grp(1, CHUNK_W)],
        out_specs=tokens,
        out_shape=jax.ShapeDtypeStruct(u.shape, F32),
        compiler_params=_params("parallel", "parallel"),
        name="s5_mix",
    )(u, perm, m, wst, wo, apa, apb, dx)


FULL = 0


def _flash_kernel(qt_ref, kt_ref, kind_ref, q_ref, k_ref, v_ref, o_ref, m_scr, acc_scr, *, heads):
    p_idx = pl.program_id(2)
    ki = kt_ref[p_idx]
    kind = kind_ref[p_idx]
    tq, tk = q_ref.shape[2], k_ref.shape[2]

    @pl.when(ki == 0)
    def _():
        m_scr[...] = jnp.full(m_scr.shape, -jnp.inf, F32)
        acc_scr[...] = jnp.zeros(acc_scr.shape, F32)

    def attend(j, nk, diag_offset):
        s = _dot_nt(q_ref[0, j], k_ref[0, j, 0:nk, :])
        if diag_offset is not None:
            row = lax.broadcasted_iota(jnp.int32, s.shape, 0)
            col = lax.broadcasted_iota(jnp.int32, s.shape, 1)
            s = jnp.where(col <= row + diag_offset, s, -jnp.inf)
        m_prev = m_scr[j]
        m_new = jnp.maximum(m_prev, jnp.max(s, axis=-1, keepdims=True))
        alpha = jnp.exp2(m_prev - m_new)
        p = jnp.exp2(s - jnp.tile(m_new, (1, nk // LANES)))
        acc_scr[j] = alpha * acc_scr[j] + _dot(p.astype(BF16), v_ref[0, j, 0:nk, :])
        m_scr[j] = m_new

    def all_heads(nk, diag_offset):
        def group(g, carry):
            for j in range(FLASH_UNROLL):
                attend(g * FLASH_UNROLL + j, nk, diag_offset)
            return carry
        lax.fori_loop(0, heads // FLASH_UNROLL, group, 0)

    @pl.when(kind == FULL)
    def _():
        all_heads(tk, None)

    for r in range(1, tk // tq + 1):
        @pl.when(kind == r)
        def _(r=r):
            all_heads(r * tq, (r - 1) * tq)

    @pl.when(kind != FULL)
    def _():
        lane = lax.broadcasted_iota(jnp.int32, (tq, HEAD_PAD), 1)
        outs = []
        for j in range(0, heads, 2):
            even, odd = acc_scr[j], acc_scr[j + 1]
            pv = jnp.where(lane < V_DIM, even, odd)
            sums = pltpu.roll(jnp.where(lane < V_DIM, odd, even), V_DIM, axis=1)
            outs.append(pv / sums)
        o_ref[0] = jnp.concatenate(outs, axis=-1).astype(BF16)


def _flash(q, k, v, tq, heads_per_step):
    bsz, heads, seq, _ = q.shape
    hp = heads_per_step
    ratio = FLASH_KEY_RATIO
    tk = ratio * tq
    assert seq % tk == 0
    steps = []
    for i in range(seq // tq):
        steps += [(i, j, FULL) for j in range(i // ratio)] + [(i, i // ratio, i % ratio + 1)]
    qt, kt, kind = (jnp.asarray(col, jnp.int32) for col in zip(*steps))
    qspec = pl.BlockSpec((1, hp, tq, HEAD_PAD), lambda b, h, p, qt, kt, kind: (b, h, qt[p], 0))
    kspec = pl.BlockSpec((1, hp, tk, HEAD_PAD), lambda b, h, p, qt, kt, kind: (b, h, kt[p], 0))
    grid_spec = pltpu.PrefetchScalarGridSpec(
        num_scalar_prefetch=3,
        grid=(bsz, heads // hp, len(steps)),
        in_specs=[qspec, kspec, kspec],
        out_specs=pl.BlockSpec((1, tq, hp * V_DIM), lambda b, h, p, qt, kt, kind: (b, qt[p], h)),
        scratch_shapes=[pltpu.VMEM((hp, tq, LANES), F32), pltpu.VMEM((hp, tq, HEAD_PAD), F32)])
    return pl.pallas_call(
        functools.partial(_flash_kernel, heads=hp),
        grid_spec=grid_spec,
        out_shape=jax.ShapeDtypeStruct((bsz, seq, heads * V_DIM), BF16),
        compiler_params=_params("parallel", "parallel", "arbitrary"),
        name="flash",
    )(qt, kt, kind, q, k, v)


def _head_sums(v2, lane, width):
    out = jnp.zeros_like(v2)
    for hd in range(v2.shape[-1] // width):
        msk = (lane >= hd * width) & (lane < (hd + 1) * width)
        out = out + jnp.where(msk, jnp.sum(jnp.where(msk, v2, 0.0), axis=-1, keepdims=True), 0.0)
    return out


def _mem_prep_kernel(mem_ref, g_ref, w_ref, kg_ref, kt_ref, vm_ref):
    hm = _rms(mem_ref[0], g_ref[0]).astype(BF16)
    kv = _dot(hm, w_ref[0])
    k, v = kv[:, :MEM_WIDTH], kv[:, MEM_WIDTH:]
    lane = lax.broadcasted_iota(jnp.int32, k.shape, 1)
    ss = _head_sums(k * k, lane, MEM_HEAD_DIM)
    kn = k * lax.rsqrt(ss * (1.0 / MEM_HEAD_DIM) + EPS) * kg_ref[0] * (math.log2(math.e) / math.sqrt(MEM_HEAD_DIM))
    knt = kn.T
    row = lax.broadcasted_iota(jnp.int32, knt.shape, 0)
    for hd in range(MEM_HEADS):
        lo, hi = hd * MEM_HEAD_DIM, (hd + 1) * MEM_HEAD_DIM
        kt_ref[0, 0, hd] = jnp.where((row >= lo) & (row < hi), knt, 0.0).astype(BF16)
        vm_ref[0, 0, hd] = jnp.where((lane >= lo) & (lane < hi), v, 0.0).astype(BF16)


def _mem_prep(mem, g, w_kv, k_gain):
    nl = g.shape[0]
    bsz = mem.shape[0]
    out = jax.ShapeDtypeStruct((nl, bsz, MEM_HEADS, N_MEM, MEM_WIDTH), BF16)
    return pl.pallas_call(
        _mem_prep_kernel,
        grid=(nl, bsz),
        in_specs=[pl.BlockSpec((1, N_MEM, D_MODEL), lambda l, b: (b, 0, 0)),
                  pl.BlockSpec((1, 1, D_MODEL), lambda l, b: (l, 0, 0)),
                  pl.BlockSpec((1, D_MODEL, 2 * MEM_WIDTH), lambda l, b: (l, 0, 0)),
                  pl.BlockSpec((1, 1, MEM_WIDTH), lambda l, b: (l, 0, 0))],
        out_specs=[pl.BlockSpec((1, 1, MEM_HEADS, MEM_WIDTH, N_MEM), lambda l, b: (l, b, 0, 0, 0)),
                   pl.BlockSpec((1, 1, MEM_HEADS, N_MEM, MEM_WIDTH), lambda l, b: (l, b, 0, 0, 0))],
        out_shape=[out, out],
        compiler_params=_params("parallel", "parallel"),
        name="mem_prep",
    )(mem, g, w_kv, k_gain)


def _outproj_kernel(x_ref, ys_ref, om_ref, wglu_ref, bglu_ref, wout_ref,
                    wmq_ref, qg_ref, bdm_ref, kt_ref, vm_ref, wmo_ref,
                    x2_ref, h3_ref):
    nchunk = x_ref.shape[0] // SSM_CHUNK
    ys = jnp.concatenate([
        jnp.concatenate([ys_ref[t, c * CHUNK_PITCH:c * CHUNK_PITCH + SSM_CHUNK, :] for c in range(nchunk)], axis=0)
        for t in range(LANE_TILES)], axis=1)
    yg = ys * jax.nn.sigmoid(_dot(ys.astype(BF16), wglu_ref[...]) + bglu_ref[...])
    n1 = _rms(yg).astype(BF16)
    n2 = _rms(om_ref[...].astype(F32)).astype(BF16)
    x1 = x_ref[...] + _dot(n1, wout_ref[:SSM_WIDTH, :]) + _dot(n2, wout_ref[SSM_WIDTH:, :])

    r1 = lax.rsqrt(jnp.mean(x1 * x1, axis=-1, keepdims=True) + EPS)
    q = _dot(x1.astype(BF16), wmq_ref[...]) * r1
    ss = _dot((q * q).astype(BF16), bdm_ref[...])
    qn = (q * lax.rsqrt(ss * (1.0 / MEM_HEAD_DIM) + EPS) * qg_ref[...]).astype(BF16)
    o = jnp.zeros(q.shape, F32)
    for hd in range(MEM_HEADS):
        s = _dot(qn, kt_ref[hd])
        p = jnp.exp2(s - jnp.max(s, axis=-1, keepdims=True))
        inv = 1.0 / jnp.sum(p, axis=-1, keepdims=True)
        o = o + _dot(p.astype(BF16), vm_ref[hd]) * inv
    x2 = x1 + _dot(o.astype(BF16), wmo_ref[...])
    x2_ref[...] = x2
    h3_ref[...] = _rms(x2).astype(BF16)


def _outproj(l, x, ys, om, w, kt, vm, tm):
    bsz, seq, _ = x.shape
    tok = lambda width: pl.BlockSpec((None, tm, width), lambda b, i: (b, i, 0))
    memb = pl.BlockSpec((None, None, MEM_HEADS, N_MEM, MEM_WIDTH), lambda b, i: (l, b, 0, 0, 0))
    return pl.pallas_call(
        _outproj_kernel,
        grid=(bsz, seq // tm),
        in_specs=[tok(D_MODEL), pl.BlockSpec((LANE_TILES, None, _pitched(tm), LANES), lambda b, i: (0, b, i, 0)),
                  tok(MLA_WIDTH),
                  _layer_spec(l, SSM_WIDTH, SSM_WIDTH), _layer_spec(l, 1, SSM_WIDTH),
                  _layer_spec(l, D_MODEL, D_MODEL),
                  _layer_spec(l, D_MODEL, MEM_WIDTH), _layer_spec(l, 1, MEM_WIDTH),
                  _const_spec(MEM_WIDTH, MEM_WIDTH),
                  memb, memb, _layer_spec(l, MEM_WIDTH, D_MODEL)],
        out_specs=[tok(D_MODEL), tok(D_MODEL)],
        out_shape=[jax.ShapeDtypeStruct((bsz, seq, D_MODEL), F32),
                   jax.ShapeDtypeStruct((bsz, seq, D_MODEL), BF16)],
        compiler_params=_params("parallel", "parallel"),
        name="outproj",
    )(x, ys, om, w["w_glu"], w["b_glu"], w["w_out"],
      w["w_mq"], w["mem_q_gain"], w["bd_mem"], kt, vm, w["w_mo"])


def _mlp_kernel(x_ref, h_ref, w1_ref, w2_ref, o_ref, *, ff_tile):
    h = h_ref[...]
    acc = x_ref[...]
    for c in range(D_FF // ff_tile):
        a = jnp.maximum(_dot(h, w1_ref[:, c * ff_tile:(c + 1) * ff_tile]), 0.0)
        acc = acc + _dot((a * a).astype(BF16), w2_ref[c * ff_tile:(c + 1) * ff_tile, :])
    o_ref[...] = acc


def _mlp(l, x, h, w1, w2, tm):
    t = x.shape[0]
    tok = pl.BlockSpec((tm, D_MODEL), lambda i: (i, 0))
    return pl.pallas_call(
        functools.partial(_mlp_kernel, ff_tile=1024),
        grid=(t // tm,),
        in_specs=[tok, tok,
                  pl.BlockSpec((None, D_MODEL, D_FF), lambda i: (l, 0, 0), pipeline_mode=pl.Buffered(1)),
                  pl.BlockSpec((None, D_FF, D_MODEL), lambda i: (l, 0, 0), pipeline_mode=pl.Buffered(1))],
        out_specs=tok,
        out_shape=jax.ShapeDtypeStruct((t, D_MODEL), F32),
        compiler_params=_params("parallel"),
        name="mlp",
    )(x, h, w1, w2)


def _half_swap(w):
    h = w.shape[-1] // 2
    return jnp.concatenate([-w[..., h:], w[..., :h]], axis=-1)


def _half_swap_unsigned(w):
    h = w.shape[-1] // 2
    return jnp.concatenate([w[..., h:], w[..., :h]], axis=-1)


def _pad_last(w, before, after):
    pads = [(0, 0)] * (w.ndim - 1) + [(before, after)]
    return jnp.pad(w, pads)


def _trunk(tm, tq, x, mem, positions, norm_mix, w_in, ssm_lambda_re, ssm_lambda_im, ssm_log_step, ssm_b_re, ssm_b_im, ssm_c_re, ssm_c_im, ssm_d, ssm_w_glu, ssm_b_glu, mla_q_norm, mla_w_uq, mla_kv_norm, mla_w_ukv, mla_q_gain, mla_k_gain, out_norm_ssm, out_norm_mla, w_out, norm_mem_q, norm_mem_kv, mem_w_q, mem_w_kv, mem_q_gain, mem_k_gain, mem_w_o, norm_mlp, mlp_w1, mlp_w2):
    bsz, seq, _ = x.shape
    depth = norm_mix.shape[0]
    row = lambda a: a[:, None, :]
    tail = HEAD_PAD - QK_DIM

    s3 = SSM_WIDTH + Q_LORA + KV_LORA
    rows = lambda g: g[:, :, None]
    w_in_g = (rows(norm_mix) * w_in).astype(BF16)
    k_rope_w = w_in_g[..., s3:]
    w_in_x = jnp.concatenate([w_in_g[..., :s3], _pad_last(k_rope_w, QK_NOPE, tail),
                              _pad_last(_half_swap(k_rope_w), QK_NOPE, tail)], axis=-1)
    wq = mla_w_uq.reshape(depth, Q_LORA, MLA_HEADS, QK_DIM)
    hw = MLA_HEADS * HEAD_PAD
    gq = rows(mla_q_norm)
    w_qa = (gq * _pad_last(wq, 0, tail).reshape(depth, Q_LORA, hw)).astype(BF16)
    w_qb = (gq * _pad_last(_half_swap(wq[..., QK_NOPE:]), QK_NOPE, tail).reshape(depth, Q_LORA, hw)).astype(BF16)
    wkv = mla_w_ukv.reshape(depth, KV_LORA, MLA_HEADS, QK_NOPE + V_DIM)
    gkv = rows(mla_kv_norm)
    w_k = (gkv * _pad_last(wkv[..., :QK_NOPE], 0, HEAD_PAD - QK_NOPE).reshape(depth, KV_LORA, hw)).astype(BF16)
    wv = wkv[..., QK_NOPE:]
    wv = jnp.stack([_pad_last(wv[:, :, 0::2], 0, HEAD_PAD - V_DIM), _pad_last(wv[:, :, 1::2], HEAD_PAD - V_DIM, 0)],
                   axis=3)
    w_v = (gkv * wv.reshape(depth, KV_LORA, hw)).astype(BF16)
    q_rope_g, k_rope_g = mla_q_gain[:, QK_NOPE:], mla_k_gain[:, QK_NOPE:]
    mla_gains = jnp.stack([
        _pad_last(mla_q_gain, 0, tail),
        _pad_last(_half_swap_unsigned(q_rope_g), QK_NOPE, tail),
        _pad_last(mla_k_gain[:, :QK_NOPE], 0, HEAD_PAD - QK_NOPE),
        _pad_last(k_rope_g, QK_NOPE, tail),
        _pad_last(_half_swap_unsigned(k_rope_g), QK_NOPE, tail)], axis=1)
    mla_gains = jnp.pad(mla_gains, ((0, 0), (0, 3), (0, 0)))
    blk = jnp.arange(2 * HEAD_PAD) // HEAD_PAD
    bd = (blk[:, None] == blk[None, :]).astype(BF16)
    mblk = jnp.arange(MEM_WIDTH) // MEM_HEAD_DIM
    bd_mem = (mblk[:, None] == mblk[None, :]).astype(BF16)
    wmkv = mem_w_kv.reshape(depth, D_MODEL, MEM_HEADS, 2, MEM_HEAD_DIM)
    w_mkv = wmkv.transpose(0, 1, 3, 2, 4).reshape(depth, D_MODEL, 2 * MEM_WIDTH).astype(BF16)
    idx = jnp.arange(GROUPS_PER_TILE * LANES)
    dest = ((idx // SSM_GROUP) % GROUPS_PER_TILE) * LANES + (idx // LANES) * SSM_GROUP + idx % SSM_GROUP
    perm = (dest[:, None] == idx[None, :]).astype(BF16)

    g_out = jnp.concatenate([out_norm_ssm, out_norm_mla], axis=-1)
    w = dict(w_in=w_in_x, w_qa=w_qa, w_qb=w_qb, w_k=w_k, w_v=w_v, mla_gains=mla_gains, bd=bd, bd_mem=bd_mem,
             w_glu=ssm_w_glu.astype(BF16), b_glu=row(ssm_b_glu),
             w_out=(rows(g_out) * w_out).astype(BF16),
             w_mq=(rows(norm_mem_q) * mem_w_q).astype(BF16),
             mem_q_gain=row(jnp.tile(mem_q_gain, (1, MEM_HEADS))),
             w_mo=mem_w_o.astype(BF16))
    w1_b = (rows(norm_mlp) * mlp_w1).astype(BF16)
    w2_b = mlp_w2.astype(BF16)

    ra, rb = _rope_tables(positions)
    prep = _s5_prep(ssm_lambda_re, ssm_lambda_im, ssm_log_step, ssm_b_re, ssm_b_im, ssm_c_re, ssm_c_im)
    kt_all, vm_all = _mem_prep(mem, row(norm_mem_kv), w_mkv, row(jnp.tile(mem_k_gain, (1, MEM_HEADS))))
    dx = jnp.tile(ssm_d.reshape(depth, SSM_GROUPS, 1, SSM_GROUP), (1, 1, 1, SSM_CHUNK))

    for l in range(depth):
        u, q, k, v = _inproj(l, x, w, ra, rb, min(2 * tm, seq))
        ys = _s5_mix(l, u, prep, dx, perm)
        om = _flash(q, k, v, tq, FLASH_HEADS)
        x2, h3 = _outproj(l, x, ys, om, w, kt_all, vm_all, min(2 * tm, seq))
        x = _mlp(l, x2.reshape(bsz * seq, D_MODEL), h3.reshape(bsz * seq, D_MODEL),
                 w1_b, w2_b, min(2 * tm, seq)).reshape(bsz, seq, D_MODEL)
    return x


def kernel(x, mem, positions, norm_mix, w_in, ssm_lambda_re, ssm_lambda_im, ssm_log_step, ssm_b_re, ssm_b_im, ssm_c_re, ssm_c_im, ssm_d, ssm_w_glu, ssm_b_glu, mla_q_norm, mla_w_uq, mla_kv_norm, mla_w_ukv, mla_q_gain, mla_k_gain, out_norm_ssm, out_norm_mla, w_out, norm_mem_q, norm_mem_kv, mem_w_q, mem_w_kv, mem_q_gain, mem_k_gain, mem_w_o, norm_mlp, mlp_w1, mlp_w2):
    seq = x.shape[1]
    return _trunk(min(512, seq), min(512, seq // 2), x, mem, positions, norm_mix, w_in, ssm_lambda_re, ssm_lambda_im, ssm_log_step, ssm_b_re, ssm_b_im, ssm_c_re, ssm_c_im, ssm_d, ssm_w_glu, ssm_b_glu, mla_q_norm, mla_w_uq, mla_kv_norm, mla_w_ukv, mla_q_gain, mla_k_gain, out_norm_ssm, out_norm_mla, w_out, norm_mem_q, norm_mem_kv, mem_w_q, mem_w_kv, mem_q_gain, mem_k_gain, mem_w_o, norm_mlp, mlp_w1, mlp_w2)
```

```python
import functools
import math

import jax
import jax.numpy as jnp
from jax import lax
from jax.experimental import pallas as pl
from jax.experimental.pallas import tpu as pltpu

D_MODEL = 1024
N_MEM = 256
MEM_HEADS = 4
MEM_HEAD_DIM = 64
MEM_WIDTH = MEM_HEADS * MEM_HEAD_DIM
SSM_WIDTH = 512
MLA_WIDTH = 512
SSM_GROUP = 16
SSM_GROUPS = 32
SSM_STATE = 64
MLA_HEADS = 8
QK_NOPE = 64
QK_ROPE = 32
QK_DIM = QK_NOPE + QK_ROPE
V_DIM = 64
Q_LORA = 256
KV_LORA = 128
ROPE_THETA = 10000.0
D_FF = 4 * D_MODEL
EPS = 1e-6

LANES = 128
HEAD_PAD = 128
SSM_CHUNK = 32
CHUNK_W = SSM_CHUNK * SSM_GROUP
CHUNK_PITCH = SSM_CHUNK + 4
GROUPS_PER_TILE = LANES // SSM_GROUP
LANE_TILES = SSM_WIDTH // LANES
IN_COLS_PAD = SSM_WIDTH + Q_LORA + KV_LORA + 2 * LANES
FLASH_UNROLL = 4
FLASH_KEY_RATIO = 4
FLASH_HEADS = 8
VMEM_LIMIT = 56 * 1024 * 1024

F32 = jnp.float32
BF16 = jnp.bfloat16
HIGHEST = lax.Precision.HIGHEST


def _dot(a, b):
    return jnp.dot(a, b, preferred_element_type=F32)


def _dot_exact(a, b):
    return jnp.dot(a, b, precision=HIGHEST, preferred_element_type=F32)


def _select_cols(a, onehot):
    hi = a.astype(BF16)
    lo = (a - hi.astype(F32)).astype(BF16)
    return _dot(hi, onehot) + _dot(lo, onehot)


def _dot_nt(a, b):
    return lax.dot_general(a, b, (((1,), (1,)), ((), ())), preferred_element_type=F32)


def _rms(v, gain=None):
    y = v * lax.rsqrt(jnp.mean(v * v, axis=-1, keepdims=True) + EPS)
    return y if gain is None else y * gain


def _params(*sem):
    return pltpu.CompilerParams(dimension_semantics=sem, vmem_limit_bytes=VMEM_LIMIT)


def _layer_spec(l, *shape):
    return pl.BlockSpec((None,) + shape, lambda *_: (l,) + (0,) * len(shape))


def _pitched(tokens):
    return tokens // SSM_CHUNK * CHUNK_PITCH


def _const_spec(*shape):
    return pl.BlockSpec(shape, lambda *_: (0,) * len(shape))


ROPE_TOKENS_PER_ROW = LANES // QK_ROPE


def _rope_table_kernel(pos_ref, freq_ref, a_ref, b_ref):
    ang = pos_ref[...].astype(F32) * freq_ref[...]
    cos, sin = jnp.cos(ang), jnp.sin(ang)
    rows = ang.shape[0]
    lane = lax.broadcasted_iota(jnp.int32, ang.shape, 1)
    in_rope = (lane >= QK_NOPE) & (lane < QK_DIM)
    for j in range(ROPE_TOKENS_PER_ROW):
        shift = (QK_NOPE - QK_ROPE * j) % LANES
        cj = pltpu.roll(cos, shift, axis=1) if shift else cos
        sj = pltpu.roll(sin, shift, axis=1) if shift else sin
        a_ref[pl.ds(j, rows, stride=ROPE_TOKENS_PER_ROW), :] = (
            jnp.where(lane < QK_NOPE, 1.0, jnp.where(in_rope, cj, 0.0)))
        b_ref[pl.ds(j, rows, stride=ROPE_TOKENS_PER_ROW), :] = jnp.where(in_rope, sj, 0.0)


def _rope_tables(positions):
    bsz, seq = positions.shape
    t = bsz * seq
    tm = min(1024, t)
    per_row = ROPE_TOKENS_PER_ROW
    half = QK_ROPE // 2
    inv_freq = ROPE_THETA ** (-jnp.arange(half, dtype=F32) / half)
    freq = jnp.tile(inv_freq, LANES // half)[None, :]
    pos = jnp.repeat(positions.reshape(t // per_row, per_row), QK_ROPE, axis=1)
    a, b = pl.pallas_call(
        _rope_table_kernel,
        grid=(t // tm,),
        in_specs=[pl.BlockSpec((tm // per_row, LANES), lambda i: (i, 0)), _const_spec(1, LANES)],
        out_specs=[pl.BlockSpec((tm, LANES), lambda i: (i, 0))] * 2,
        out_shape=[jax.ShapeDtypeStruct((t, LANES), F32)] * 2,
        compiler_params=_params("parallel"),
        name="rope_tables",
    )(pos, freq)
    return a.reshape(bsz, seq, LANES), b.reshape(bsz, seq, LANES)


def _inproj_kernel(x_ref, win_ref, wqa_ref, wqb_ref, wk_ref, wv_ref,
                   gains_ref, bd_ref, ra_ref, rb_ref, u_ref, q_ref, k_ref, v_ref):
    h = _rms(x_ref[...]).astype(BF16)
    proj = _dot(h, win_ref[...])
    pad_rows = jnp.zeros((CHUNK_PITCH - SSM_CHUNK, LANES), F32)
    for t in range(LANE_TILES):
        for c in range(proj.shape[0] // SSM_CHUNK):
            u_ref[t, c * CHUNK_PITCH:c * CHUNK_PITCH + SSM_CHUNK, :] = (
                proj[c * SSM_CHUNK:(c + 1) * SSM_CHUNK, t * LANES:(t + 1) * LANES])
            u_ref[t, c * CHUNK_PITCH + SSM_CHUNK:(c + 1) * CHUNK_PITCH, :] = pad_rows

    two = lambda a: jnp.concatenate([a, a], axis=1)
    ra, rb = ra_ref[...], rb_ref[...]
    gains = gains_ref[...]
    bd = bd_ref[...]
    scale = math.log2(math.e) / math.sqrt(QK_DIM)
    inv_d = 1.0 / QK_DIM

    o1 = SSM_WIDTH
    o2 = o1 + Q_LORA
    o3 = o2 + KV_LORA
    hq = _rms(proj[:, o1:o2]).astype(BF16)
    qa = _dot(hq, wqa_ref[...])
    qb = _dot(hq, wqb_ref[...])
    hkv = _rms(proj[:, o2:o3]).astype(BF16)
    kn = _dot(hkv, wk_ref[...])
    vm = _dot(hkv, wv_ref[...])

    kr = proj[:, o3:o3 + LANES]
    kr_sw = proj[:, o3 + LANES:o3 + 2 * LANES]
    kr_ss = two(_dot((kr * kr).astype(BF16), bd[:LANES, :LANES]))
    krot = two(kr * (gains[3:4] * ra) + kr_sw * (gains[4:5] * rb))
    ga, gb, gk = two(ra * gains[0:1]), two(rb * gains[1:2]), two(gains[2:3])
    lane = lax.broadcasted_iota(jnp.int32, ga.shape, 1)
    ones_col = jnp.where((lane >= V_DIM) & (lane < HEAD_PAD + V_DIM), 1.0, 0.0)

    for hp in range(MLA_HEADS // 2):
        sl = slice(2 * hp * HEAD_PAD, (2 * hp + 2) * HEAD_PAD)
        q2 = qa[:, sl]
        rq = lax.rsqrt(_dot((q2 * q2).astype(BF16), bd) * inv_d + EPS) * scale
        qo = ((q2 * ga + qb[:, sl] * gb) * rq).astype(BF16)
        k2 = kn[:, sl]
        rk = lax.rsqrt((_dot((k2 * k2).astype(BF16), bd) + kr_ss) * inv_d + EPS)
        ko = ((k2 * gk + krot) * rk).astype(BF16)
        vo = (vm[:, sl] + ones_col).astype(BF16)
        for j in range(2):
            q_ref[2 * hp + j] = qo[:, j * HEAD_PAD:(j + 1) * HEAD_PAD]
            k_ref[2 * hp + j] = ko[:, j * HEAD_PAD:(j + 1) * HEAD_PAD]
            v_ref[2 * hp + j] = vo[:, j * HEAD_PAD:(j + 1) * HEAD_PAD]


def _inproj(l, x, w, ra, rb, tm):
    bsz, seq, _ = x.shape
    tok = lambda width: pl.BlockSpec((None, tm, width), lambda b, i: (b, i, 0))
    head = pl.BlockSpec((None, MLA_HEADS, tm, HEAD_PAD), lambda b, i: (b, 0, i, 0))
    hw = MLA_HEADS * HEAD_PAD
    head_shape = jax.ShapeDtypeStruct((bsz, MLA_HEADS, seq, HEAD_PAD), BF16)
    return pl.pallas_call(
        _inproj_kernel,
        grid=(bsz, seq // tm),
        in_specs=[tok(D_MODEL), _layer_spec(l, D_MODEL, IN_COLS_PAD),
                  _layer_spec(l, Q_LORA, hw), _layer_spec(l, Q_LORA, hw),
                  _layer_spec(l, KV_LORA, hw), _layer_spec(l, KV_LORA, hw),
                  _layer_spec(l, 8, HEAD_PAD), _const_spec(2 * HEAD_PAD, 2 * HEAD_PAD),
                  tok(LANES), tok(LANES)],
        out_specs=[pl.BlockSpec((LANE_TILES, None, _pitched(tm), LANES), lambda b, i: (0, b, i, 0)),
                   head, head, head],
        out_shape=[jax.ShapeDtypeStruct((LANE_TILES, bsz, _pitched(seq), LANES), F32),
                   head_shape, head_shape, head_shape],
        compiler_params=_params("parallel", "parallel"),
        name="inproj",
    )(x, w["w_in"], w["w_qa"], w["w_qb"], w["w_k"], w["w_v"], w["mla_gains"], w["bd"], ra, rb)


S5_PREP_GROUPS = 16


def _s5_prep_kernel(*refs):
    per_group_in, shared, outs = refs[:8], refs[8:10], refs[10:]
    for g in range(S5_PREP_GROUPS):
        _s5_prep_group(*[r.at[g] for r in per_group_in], *shared, *[r.at[g] for r in outs])


def _s5_prep_group(pcol_ref, prow_ref, b_re_ref, b_im_ref, bt_re_ref, bt_im_ref,
                   ct_re_ref, ct_im_ref, sel_ref, reps_ref,
                   m_ref, wst_ref, wo_ref, apa_ref, apb_ref):
    def zoh_coeff(a_re, a_im, lr, li):
        xr = a_re - 1.0
        den = lr * lr + li * li
        return (xr * lr + a_im * li) / den, (a_im * lr - xr * li) / den

    def zoh(lr, li, ls):
        step = jnp.exp(ls)
        mag = jnp.exp(lr * step)
        ang = li * step
        return zoh_coeff(mag * jnp.cos(ang), mag * jnp.sin(ang), lr, li) + (step,)

    pc = pcol_ref[...]
    lr, li = pc[:, 0:1], pc[:, 1:2]
    step = jnp.exp(pc[:, 2:3])
    n = lax.broadcasted_iota(jnp.int32, (SSM_STATE, LANES), 1).astype(F32)
    mag = jnp.exp(lr * step * n)
    th = li * step * n
    pw_re, pw_im = mag * jnp.cos(th), mag * jnp.sin(th)
    cfr, cfi = zoh_coeff(pw_re[:, 1:2], pw_im[:, 1:2], lr, li)
    b_re, b_im = b_re_ref[...], b_im_ref[...]
    bb_re = cfr * b_re - cfi * b_im
    bb_im = cfr * b_im + cfi * b_re

    pr = prow_ref[...]
    lr_row, li_row, ls_row = pr[0:1], pr[1:2], pr[2:3]
    cfr_row, cfi_row, step_row = zoh(lr_row, li_row, ls_row)
    bt_re, bt_im = bt_re_ref[...], bt_im_ref[...]
    bbt_re = cfr_row[:, :SSM_STATE] * bt_re - cfi_row[:, :SSM_STATE] * bt_im
    bbt_im = cfr_row[:, :SSM_STATE] * bt_im + cfi_row[:, :SSM_STATE] * bt_re

    sel = sel_ref[...]
    brep_re, brep_im = _select_cols(bb_re, sel), _select_cols(bb_im, sel)
    crep_re, crep_im = _select_cols(ct_re_ref[...], sel), _select_cols(ct_im_ref[...], sel)

    apow = lambda i: (_select_cols(pw_re, reps_ref[i]), _select_cols(pw_im, reps_ref[i]))

    e_re, e_im = apow(0)
    g_re = e_re * crep_re - e_im * crep_im
    g_im = e_re * crep_im + e_im * crep_re
    kt = _dot_exact(bbt_re, g_re) - _dot_exact(bbt_im, g_im)
    klane = lax.broadcasted_iota(jnp.int32, kt.shape, 1)
    m_ref[0:SSM_GROUP, :] = kt.astype(BF16)
    for s in range(1, SSM_CHUNK):
        shifted = jnp.where(klane >= s * SSM_GROUP, pltpu.roll(kt, s * SSM_GROUP, axis=1), 0.0)
        m_ref[s * SSM_GROUP:(s + 1) * SSM_GROUP, :] = shifted.astype(BF16)

    e_re, e_im = apow(1)
    wst_ref[0:SSM_STATE, :] = (e_re * brep_re - e_im * brep_im).astype(BF16)
    wst_ref[SSM_STATE:, :] = (e_re * brep_im + e_im * brep_re).astype(BF16)

    e_re, e_im = apow(2)
    wo_ref[0:SSM_STATE, :] = (e_re * crep_re - e_im * crep_im).astype(BF16)
    wo_ref[SSM_STATE:, :] = (-(e_re * crep_im + e_im * crep_re)).astype(BF16)

    krow = lax.broadcasted_iota(jnp.int32, (8, LANES), 0)
    klan = lax.broadcasted_iota(jnp.int32, (8, LANES), 1)
    nn = (SSM_CHUNK * jnp.left_shift(1, krow)).astype(F32)
    mag = jnp.exp(lr_row * step_row * nn)
    th = li_row * step_row * nn
    p_re, p_im = mag * jnp.cos(th), mag * jnp.sin(th)
    apa_ref[...] = p_re
    apb_ref[...] = jnp.where(klan < SSM_STATE, -p_im, p_im)


def _s5_prep(lam_re, lam_im, log_step, b_re, b_im, c_re, c_im):
    nl, g, p = lam_re.shape
    hh = SSM_GROUP
    ls = jnp.broadcast_to(log_step[..., None], (nl, g, p))
    params = jnp.stack([lam_re, lam_im, ls], axis=-1)
    pcol = jnp.pad(params, ((0, 0), (0, 0), (0, 0), (0, 5)))
    dup = lambda v: jnp.concatenate([v, v], axis=-1)
    prow = jnp.pad(jnp.stack([dup(lam_re), dup(lam_im), dup(ls)], axis=2),
                   ((0, 0), (0, 0), (0, 5), (0, 0)))
    sel = jnp.tile(jnp.eye(hh, dtype=BF16), (1, SSM_CHUNK))
    tau = jnp.arange(CHUNK_W) // hh
    nrow = jnp.arange(LANES)[:, None]
    reps = jnp.stack([nrow == tau, nrow == SSM_CHUNK - 1 - tau, nrow == tau + 1]).astype(BF16)
    blk = lambda *s: pl.BlockSpec((None, S5_PREP_GROUPS) + s, lambda l, i: (l, i) + (0,) * len(s))
    return pl.pallas_call(
        _s5_prep_kernel,
        grid=(nl, g // S5_PREP_GROUPS),
        in_specs=[blk(p, 8), blk(8, LANES), blk(p, hh), blk(p, hh), blk(hh, p), blk(hh, p),
                  blk(p, hh), blk(p, hh), _const_spec(hh, CHUNK_W), _const_spec(3, LANES, CHUNK_W)],
        out_specs=[blk(CHUNK_W, CHUNK_W), blk(2 * p, CHUNK_W), blk(2 * p, CHUNK_W),
                   blk(8, LANES), blk(8, LANES)],
        out_shape=[jax.ShapeDtypeStruct((nl, g, CHUNK_W, CHUNK_W), BF16),
                   jax.ShapeDtypeStruct((nl, g, 2 * p, CHUNK_W), BF16),
                   jax.ShapeDtypeStruct((nl, g, 2 * p, CHUNK_W), BF16),
                   jax.ShapeDtypeStruct((nl, g, 8, LANES), F32),
                   jax.ShapeDtypeStruct((nl, g, 8, LANES), F32)],
        compiler_params=_params("parallel", "parallel"),
        name="s5_prep",
    )(pcol, prow, b_re, b_im, jnp.swapaxes(b_re, -1, -2), jnp.swapaxes(b_im, -1, -2),
      jnp.swapaxes(c_re, -1, -2), jnp.swapaxes(c_im, -1, -2), sel, reps)


def _s5_mix_kernel(u_ref, perm_ref, m_ref, wst_ref, wo_ref, apa_ref, apb_ref, dx_ref, y_ref,
                   *, nsteps):
    nb, rows, _ = u_ref.shape
    cps = rows // CHUNK_PITCH
    nc = nb * cps
    gpt = GROUPS_PER_TILE
    nslab = SSM_CHUNK // gpt
    perm = perm_ref[...]

    def slab(j):
        return jnp.concatenate([
            jnp.concatenate([u_ref[b, pl.ds(gpt * j + s, cps, stride=CHUNK_PITCH), :] for s in range(gpt)], axis=1)
            for b in range(nb)], axis=0)

    z = _dot(jnp.concatenate([slab(j) for j in range(nslab)], axis=0).astype(BF16), perm).astype(BF16)

    c = lax.broadcasted_iota(jnp.int32, (nc, 2 * SSM_STATE), 0) & (cps - 1)
    ys = []
    for g in range(gpt):
        x = jnp.concatenate([z[j * nc:(j + 1) * nc, g * LANES:(g + 1) * LANES] for j in range(nslab)],
                            axis=1)
        y = _dot(x, m_ref[g])
        st = _dot_nt(x, wst_ref[g])
        apa, apb = apa_ref[g], apb_ref[g]
        for k in range(nsteps):
            d = 1 << k
            sh = jnp.where(c >= d, pltpu.roll(st, d, axis=0), 0.0)
            st = st + sh * apa[k:k + 1] + pltpu.roll(sh, SSM_STATE, axis=1) * apb[k:k + 1]
        carried = jnp.where(c >= 1, pltpu.roll(st, 1, axis=0), 0.0)
        y = y + _dot(carried.astype(BF16), wo_ref[g]) + dx_ref[g] * x.astype(F32)
        ys.append(jax.nn.gelu(y).astype(BF16))

    w = jnp.concatenate([
        jnp.concatenate([ys[g][:, j * LANES:(j + 1) * LANES] for g in range(gpt)], axis=1)
        for j in range(nslab)], axis=0)
    o = _dot(w, perm)
    for b in range(nb):
        for j in range(nslab):
            for t in range(gpt):
                y_ref[b, pl.ds(gpt * j + t, cps, stride=CHUNK_PITCH), :] = (
                    o[j * nc + b * cps:j * nc + (b + 1) * cps, t * LANES:(t + 1) * LANES])
        for r in range(SSM_CHUNK, CHUNK_PITCH):
            y_ref[b, pl.ds(r, cps, stride=CHUNK_PITCH), :] = jnp.zeros((cps, LANES), F32)


def _s5_mix(l, u, prep, dx, perm):
    tiles, bsz, rows, _ = u.shape
    cps = rows // CHUNK_PITCH
    assert cps & (cps - 1) == 0 and cps <= 256
    gpt = GROUPS_PER_TILE
    nb = 2 if bsz % 2 == 0 else 1
    m, wst, wo, apa, apb = prep
    tokens = pl.BlockSpec((None, nb, rows, LANES), lambda t, b: (t, b, 0, 0))
    grp = lambda *s: pl.BlockSpec((None, gpt) + s, lambda t, b: (l, t) + (0,) * len(s))
    return pl.pallas_call(
        functools.partial(_s5_mix_kernel, nsteps=cps.bit_length() - 1),
        grid=(tiles, bsz // nb),
        in_specs=[tokens, _const_spec(gpt * LANES, gpt * LANES), grp(CHUNK_W, CHUNK_W),
                  grp(2 * SSM_STATE, CHUNK_W), grp(2 * SSM_STATE, CHUNK_W), grp(8, LANES), grp(8, LANES),
                  grp(1, CHUNK_W)],
        out_specs=tokens,
        out_shape=jax.ShapeDtypeStruct(u.shape, F32),
        compiler_params=_params("parallel", "parallel"),
        name="s5_mix",
    )(u, perm, m, wst, wo, apa, apb, dx)


FULL = 0


def _flash_kernel(qt_ref, kt_ref, kind_ref, q_ref, k_ref, v_ref, o_ref, m_scr, acc_scr, *, heads):
    p_idx = pl.program_id(2)
    ki = kt_ref[p_idx]
    kind = kind_ref[p_idx]
    tq, tk = q_ref.shape[2], k_ref.shape[2]

    @pl.when(ki == 0)
    def _():
        m_scr[...] = jnp.full(m_scr.shape, -jnp.inf, F32)
        acc_scr[...] = jnp.zeros(acc_scr.shape, F32)

    def attend(j, nk, diag_offset):
        s = _dot_nt(q_ref[0, j], k_ref[0, j, 0:nk, :])
        if diag_offset is not None:
            row = lax.broadcasted_iota(jnp.int32, s.shape, 0)
            col = lax.broadcasted_iota(jnp.int32, s.shape, 1)
            s = jnp.where(col <= row + diag_offset, s, -jnp.inf)
        m_prev = m_scr[j]
        m_new = jnp.maximum(m_prev, jnp.max(s, axis=-1, keepdims=True))
        alpha = jnp.exp2(m_prev - m_new)
        p = jnp.exp2(s - jnp.tile(m_new, (1, nk // LANES)))
        acc_scr[j] = alpha * acc_scr[j] + _dot(p.astype(BF16), v_ref[0, j, 0:nk, :])
        m_scr[j] = m_new

    def all_heads(nk, diag_offset):
        def group(g, carry):
            for j in range(FLASH_UNROLL):
                attend(g * FLASH_UNROLL + j, nk, diag_offset)
            return carry
        lax.fori_loop(0, heads // FLASH_UNROLL, group, 0)

    @pl.when(kind == FULL)
    def _():
        all_heads(tk, None)

    for r in range(1, tk // tq + 1):
        @pl.when(kind == r)
        def _(r=r):
            all_heads(r * tq, (r - 1) * tq)

    @pl.when(kind != FULL)
    def _():
        lane = lax.broadcasted_iota(jnp.int32, (tq, HEAD_PAD), 1)
        outs = []
        for j in range(0, heads, 2):
            even, odd = acc_scr[j], acc_scr[j + 1]
            pv = jnp.where(lane < V_DIM, even, odd)
            sums = pltpu.roll(jnp.where(lane < V_DIM, odd, even), V_DIM, axis=1)
            outs.append(pv / sums)
        o_ref[0] = jnp.concatenate(outs, axis=-1).astype(BF16)


def _flash(q, k, v, tq, heads_per_step):
    bsz, heads, seq, _ = q.shape
    hp = heads_per_step
    ratio = FLASH_KEY_RATIO
    tk = ratio * tq
    assert seq % tk == 0
    steps = []
    for i in range(seq // tq):
        steps += [(i, j, FULL) for j in range(i // ratio)] + [(i, i // ratio, i % ratio + 1)]
    qt, kt, kind = (jnp.asarray(col, jnp.int32) for col in zip(*steps))
    qspec = pl.BlockSpec((1, hp, tq, HEAD_PAD), lambda b, h, p, qt, kt, kind: (b, h, qt[p], 0))
    kspec = pl.BlockSpec((1, hp, tk, HEAD_PAD), lambda b, h, p, qt, kt, kind: (b, h, kt[p], 0))
    grid_spec = pltpu.PrefetchScalarGridSpec(
        num_scalar_prefetch=3,
        grid=(bsz, heads // hp, len(steps)),
        in_specs=[qspec, kspec, kspec],
        out_specs=pl.BlockSpec((1, tq, hp * V_DIM), lambda b, h, p, qt, kt, kind: (b, qt[p], h)),
        scratch_shapes=[pltpu.VMEM((hp, tq, LANES), F32), pltpu.VMEM((hp, tq, HEAD_PAD), F32)])
    return pl.pallas_call(
        functools.partial(_flash_kernel, heads=hp),
        grid_spec=grid_spec,
        out_shape=jax.ShapeDtypeStruct((bsz, seq, heads * V_DIM), BF16),
        compiler_params=_params("parallel", "parallel", "arbitrary"),
        name="flash",
    )(qt, kt, kind, q, k, v)


def _head_sums(v2, lane, width):
    out = jnp.zeros_like(v2)
    for hd in range(v2.shape[-1] // width):
        msk = (lane >= hd * width) & (lane < (hd + 1) * width)
        out = out + jnp.where(msk, jnp.sum(jnp.where(msk, v2, 0.0), axis=-1, keepdims=True), 0.0)
    return out


def _mem_prep_kernel(mem_ref, g_ref, w_ref, kg_ref, kt_ref, vm_ref):
    hm = _rms(mem_ref[0], g_ref[0]).astype(BF16)
    kv = _dot(hm, w_ref[0])
    k, v = kv[:, :MEM_WIDTH], kv[:, MEM_WIDTH:]
    lane = lax.broadcasted_iota(jnp.int32, k.shape, 1)
    ss = _head_sums(k * k, lane, MEM_HEAD_DIM)
    kn = k * lax.rsqrt(ss * (1.0 / MEM_HEAD_DIM) + EPS) * kg_ref[0] * (math.log2(math.e) / math.sqrt(MEM_HEAD_DIM))
    knt = kn.T
    row = lax.broadcasted_iota(jnp.int32, knt.shape, 0)
    for hd in range(MEM_HEADS):
        lo, hi = hd * MEM_HEAD_DIM, (hd + 1) * MEM_HEAD_DIM
        kt_ref[0, 0, hd] = jnp.where((row >= lo) & (row < hi), knt, 0.0).astype(BF16)
        vm_ref[0, 0, hd] = jnp.where((lane >= lo) & (lane < hi), v, 0.0).astype(BF16)


def _mem_prep(mem, g, w_kv, k_gain):
    nl = g.shape[0]
    bsz = mem.shape[0]
    out = jax.ShapeDtypeStruct((nl, bsz, MEM_HEADS, N_MEM, MEM_WIDTH), BF16)
    return pl.pallas_call(
        _mem_prep_kernel,
        grid=(nl, bsz),
        in_specs=[pl.BlockSpec((1, N_MEM, D_MODEL), lambda l, b: (b, 0, 0)),
                  pl.BlockSpec((1, 1, D_MODEL), lambda l, b: (l, 0, 0)),
                  pl.BlockSpec((1, D_MODEL, 2 * MEM_WIDTH), lambda l, b: (l, 0, 0)),
                  pl.BlockSpec((1, 1, MEM_WIDTH), lambda l, b: (l, 0, 0))],
        out_specs=[pl.BlockSpec((1, 1, MEM_HEADS, MEM_WIDTH, N_MEM), lambda l, b: (l, b, 0, 0, 0)),
                   pl.BlockSpec((1, 1, MEM_HEADS, N_MEM, MEM_WIDTH), lambda l, b: (l, b, 0, 0, 0))],
        out_shape=[out, out],
        compiler_params=_params("parallel", "parallel"),
        name="mem_prep",
    )(mem, g, w_kv, k_gain)


def _outproj_kernel(x_ref, ys_ref, om_ref, wglu_ref, bglu_ref, wout_ref,
                    wmq_ref, qg_ref, bdm_ref, kt_ref, vm_ref, wmo_ref,
                    x2_ref, h3_ref):
    nchunk = x_ref.shape[0] // SSM_CHUNK
    ys = jnp.concatenate([
        jnp.concatenate([ys_ref[t, c * CHUNK_PITCH:c * CHUNK_PITCH + SSM_CHUNK, :] for c in range(nchunk)], axis=0)
        for t in range(LANE_TILES)], axis=1)
    yg = ys * jax.nn.sigmoid(_dot(ys.astype(BF16), wglu_ref[...]) + bglu_ref[...])
    n1 = _rms(yg).astype(BF16)
    n2 = _rms(om_ref[...].astype(F32)).astype(BF16)
    x1 = x_ref[...] + _dot(n1, wout_ref[:SSM_WIDTH, :]) + _dot(n2, wout_ref[SSM_WIDTH:, :])

    r1 = lax.rsqrt(jnp.mean(x1 * x1, axis=-1, keepdims=True) + EPS)
    q = _dot(x1.astype(BF16), wmq_ref[...]) * r1
    ss = _dot((q * q).astype(BF16), bdm_ref[...])
    qn = (q * lax.rsqrt(ss * (1.0 / MEM_HEAD_DIM) + EPS) * qg_ref[...]).astype(BF16)
    o = jnp.zeros(q.shape, F32)
    for hd in range(MEM_HEADS):
        s = _dot(qn, kt_ref[hd])
        p = jnp.exp2(s - jnp.max(s, axis=-1, keepdims=True))
        inv = 1.0 / jnp.sum(p, axis=-1, keepdims=True)
        o = o + _dot(p.astype(BF16), vm_ref[hd]) * inv
    x2 = x1 + _dot(o.astype(BF16), wmo_ref[...])
    x2_ref[...] = x2
    h3_ref[...] = _rms(x2).astype(BF16)


def _outproj(l, x, ys, om, w, kt, vm, tm):
    bsz, seq, _ = x.shape
    tok = lambda width: pl.BlockSpec((None, tm, width), lambda b, i: (b, i, 0))
    memb = pl.BlockSpec((None, None, MEM_HEADS, N_MEM, MEM_WIDTH), lambda b, i: (l, b, 0, 0, 0))
    return pl.pallas_call(
        _outproj_kernel,
        grid=(bsz, seq // tm),
        in_specs=[tok(D_MODEL), pl.BlockSpec((LANE_TILES, None, _pitched(tm), LANES), lambda b, i: (0, b, i, 0)),
                  tok(MLA_WIDTH),
                  _layer_spec(l, SSM_WIDTH, SSM_WIDTH), _layer_spec(l, 1, SSM_WIDTH),
                  _layer_spec(l, D_MODEL, D_MODEL),
                  _layer_spec(l, D_MODEL, MEM_WIDTH), _layer_spec(l, 1, MEM_WIDTH),
                  _const_spec(MEM_WIDTH, MEM_WIDTH),
                  memb, memb, _layer_spec(l, MEM_WIDTH, D_MODEL)],
        out_specs=[tok(D_MODEL), tok(D_MODEL)],
        out_shape=[jax.ShapeDtypeStruct((bsz, seq, D_MODEL), F32),
                   jax.ShapeDtypeStruct((bsz, seq, D_MODEL), BF16)],
        compiler_params=_params("parallel", "parallel"),
        name="outproj",
    )(x, ys, om, w["w_glu"], w["b_glu"], w["w_out"],
      w["w_mq"], w["mem_q_gain"], w["bd_mem"], kt, vm, w["w_mo"])


def _mlp_kernel(x_ref, h_ref, w1_ref, w2_ref, o_ref, *, ff_tile):
    h = h_ref[...]
    acc = x_ref[...]
    for c in range(D_FF // ff_tile):
        a = jnp.maximum(_dot(h, w1_ref[:, c * ff_tile:(c + 1) * ff_tile]), 0.0)
        acc = acc + _dot((a * a).astype(BF16), w2_ref[c * ff_tile:(c + 1) * ff_tile, :])
    o_ref[...] = acc


def _mlp(l, x, h, w1, w2, tm):
    t = x.shape[0]
    tok = pl.BlockSpec((tm, D_MODEL), lambda i: (i, 0))
    return pl.pallas_call(
        functools.partial(_mlp_kernel, ff_tile=1024),
        grid=(t // tm,),
        in_specs=[tok, tok,
                  pl.BlockSpec((None, D_MODEL, D_FF), lambda i: (l, 0, 0), pipeline_mode=pl.Buffered(1)),
                  pl.BlockSpec((None, D_FF, D_MODEL), lambda i: (l, 0, 0), pipeline_mode=pl.Buffered(1))],
        out_specs=tok,
        out_shape=jax.ShapeDtypeStruct((t, D_MODEL), F32),
        compiler_params=_params("parallel"),
        name="mlp",
    )(x, h, w1, w2)


def _half_swap(w):
    h = w.shape[-1] // 2
    return jnp.concatenate([-w[..., h:], w[..., :h]], axis=-1)


def _half_swap_unsigned(w):
    h = w.shape[-1] // 2
    return jnp.concatenate([w[..., h:], w[..., :h]], axis=-1)


def _pad_last(w, before, after):
    pads = [(0, 0)] * (w.ndim - 1) + [(before, after)]
    return jnp.pad(w, pads)


def _trunk(tm, tq, x, mem, positions, norm_mix, w_in, ssm_lambda_re, ssm_lambda_im, ssm_log_step, ssm_b_re, ssm_b_im, ssm_c_re, ssm_c_im, ssm_d, ssm_w_glu, ssm_b_glu, mla_q_norm, mla_w_uq, mla_kv_norm, mla_w_ukv, mla_q_gain, mla_k_gain, out_norm_ssm, out_norm_mla, w_out, norm_mem_q, norm_mem_kv, mem_w_q, mem_w_kv, mem_q_gain, mem_k_gain, mem_w_o, norm_mlp, mlp_w1, mlp_w2):
    bsz, seq, _ = x.shape
    depth = norm_mix.shape[0]
    row = lambda a: a[:, None, :]
    tail = HEAD_PAD - QK_DIM

    s3 = SSM_WIDTH + Q_LORA + KV_LORA
    rows = lambda g: g[:, :, None]
    w_in_g = (rows(norm_mix) * w_in).astype(BF16)
    k_rope_w = w_in_g[..., s3:]
    w_in_x = jnp.concatenate([w_in_g[..., :s3], _pad_last(k_rope_w, QK_NOPE, tail),
                              _pad_last(_half_swap(k_rope_w), QK_NOPE, tail)], axis=-1)
    wq = mla_w_uq.reshape(depth, Q_LORA, MLA_HEADS, QK_DIM)
    hw = MLA_HEADS * HEAD_PAD
    gq = rows(mla_q_norm)
    w_qa = (gq * _pad_last(wq, 0, tail).reshape(depth, Q_LORA, hw)).astype(BF16)
    w_qb = (gq * _pad_last(_half_swap(wq[..., QK_NOPE:]), QK_NOPE, tail).reshape(depth, Q_LORA, hw)).astype(BF16)
    wkv = mla_w_ukv.reshape(depth, KV_LORA, MLA_HEADS, QK_NOPE + V_DIM)
    gkv = rows(mla_kv_norm)
    w_k = (gkv * _pad_last(wkv[..., :QK_NOPE], 0, HEAD_PAD - QK_NOPE).reshape(depth, KV_LORA, hw)).astype(BF16)
    wv = wkv[..., QK_NOPE:]
    wv = jnp.stack([_pad_last(wv[:, :, 0::2], 0, HEAD_PAD - V_DIM), _pad_last(wv[:, :, 1::2], HEAD_PAD - V_DIM, 0)],
                   axis=3)
    w_v = (gkv * wv.reshape(depth, KV_LORA, hw)).astype(BF16)
    q_rope_g, k_rope_g = mla_q_gain[:, QK_NOPE:], mla_k_gain[:, QK_NOPE:]
    mla_gains = jnp.stack([
        _pad_last(mla_q_gain, 0, tail),
        _pad_last(_half_swap_unsigned(q_rope_g), QK_NOPE, tail),
        _pad_last(mla_k_gain[:, :QK_NOPE], 0, HEAD_PAD - QK_NOPE),
        _pad_last(k_rope_g, QK_NOPE, tail),
        _pad_last(_half_swap_unsigned(k_rope_g), QK_NOPE, tail)], axis=1)
    mla_gains = jnp.pad(mla_gains, ((0, 0), (0, 3), (0, 0)))
    blk = jnp.arange(2 * HEAD_PAD) // HEAD_PAD
    bd = (blk[:, None] == blk[None, :]).astype(BF16)
    mblk = jnp.arange(MEM_WIDTH) // MEM_HEAD_DIM
    bd_mem = (mblk[:, None] == mblk[None, :]).astype(BF16)
    wmkv = mem_w_kv.reshape(depth, D_MODEL, MEM_HEADS, 2, MEM_HEAD_DIM)
    w_mkv = wmkv.transpose(0, 1, 3, 2, 4).reshape(depth, D_MODEL, 2 * MEM_WIDTH).astype(BF16)
    idx = jnp.arange(GROUPS_PER_TILE * LANES)
    dest = ((idx // SSM_GROUP) % GROUPS_PER_TILE) * LANES + (idx // LANES) * SSM_GROUP + idx % SSM_GROUP
    perm = (dest[:, None] == idx[None, :]).astype(BF16)

    g_out = jnp.concatenate([out_norm_ssm, out_norm_mla], axis=-1)
    w = dict(w_in=w_in_x, w_qa=w_qa, w_qb=w_qb, w_k=w_k, w_v=w_v, mla_gains=mla_gains, bd=bd, bd_mem=bd_mem,
             w_glu=ssm_w_glu.astype(BF16), b_glu=row(ssm_b_glu),
             w_out=(rows(g_out) * w_out).astype(BF16),
             w_mq=(rows(norm_mem_q) * mem_w_q).astype(BF16),
             mem_q_gain=row(jnp.tile(mem_q_gain, (1, MEM_HEADS))),
             w_mo=mem_w_o.astype(BF16))
    w1_b = (rows(norm_mlp) * mlp_w1).astype(BF16)
    w2_b = mlp_w2.astype(BF16)

    ra, rb = _rope_tables(positions)
    prep = _s5_prep(ssm_lambda_re, ssm_lambda_im, ssm_log_step, ssm_b_re, ssm_b_im, ssm_c_re, ssm_c_im)
    kt_all, vm_all = _mem_prep(mem, row(norm_mem_kv), w_mkv, row(jnp.tile(mem_k_gain, (1, MEM_HEADS))))
    dx = jnp.tile(ssm_d.reshape(depth, SSM_GROUPS, 1, SSM_GROUP), (1, 1, 1, SSM_CHUNK))

    for l in range(depth):
        u, q, k, v = _inproj(l, x, w, ra, rb, min(2 * tm, seq))
        ys = _s5_mix(l, u, prep, dx, perm)
        om = _flash(q, k, v, tq, FLASH_HEADS)
        x2, h3 = _outproj(l, x, ys, om, w, kt_all, vm_all, min(2 * tm, seq))
        x = _mlp(l, x2.reshape(bsz * seq, D_MODEL), h3.reshape(bsz * seq, D_MODEL),
                 w1_b, w2_b, min(2 * tm, seq)).reshape(bsz, seq, D_MODEL)
    return x


def kernel(x, mem, positions, norm_mix, w_in, ssm_lambda_re, ssm_lambda_im, ssm_log_step, ssm_b_re, ssm_b_im, ssm_c_re, ssm_c_im, ssm_d, ssm_w_glu, ssm_b_glu, mla_q_norm, mla_w_uq, mla_kv_norm, mla_w_ukv, mla_q_gain, mla_k_gain, out_norm_ssm, out_norm_mla, w_out, norm_mem_q, norm_mem_kv, mem_w_q, mem_w_kv, mem_q_gain, mem_k_gain, mem_w_o, norm_mlp, mlp_w1, mlp_w2):
    seq = x.shape[1]
    return _trunk(min(512, seq), min(512, seq // 2), x, mem, positions, norm_mix, w_in, ssm_lambda_re, ssm_lambda_im, ssm_log_step, ssm_b_re, ssm_b_im, ssm_c_re, ssm_c_im, ssm_d, ssm_w_glu, ssm_b_glu, mla_q_norm, mla_w_uq, mla_kv_norm, mla_w_ukv, mla_q_gain, mla_k_gain, out_norm_ssm, out_norm_mla, w_out, norm_mem_q, norm_mem_kv, mem_w_q, mem_w_kv, mem_q_gain, mem_k_gain, mem_w_o, norm_mlp, mlp_w1, mlp_w2)
```

```python
import functools
import math

import jax
import jax.numpy as jnp
from jax import lax
from jax.experimental import pallas as pl
from jax.experimental.pallas import tpu as pltpu

D_MODEL = 1024
N_MEM = 256
MEM_HEADS = 4
MEM_HEAD_DIM = 64
MEM_WIDTH = MEM_HEADS * MEM_HEAD_DIM
SSM_WIDTH = 512
MLA_WIDTH = 512
SSM_GROUP = 16
SSM_GROUPS = 32
SSM_STATE = 64
MLA_HEADS = 8
QK_NOPE = 64
QK_ROPE = 32
QK_DIM = QK_NOPE + QK_ROPE
V_DIM = 64
Q_LORA = 256
KV_LORA = 128
ROPE_THETA = 10000.0
D_FF = 4 * D_MODEL
EPS = 1e-6

LANES = 128
HEAD_PAD = 128
SSM_CHUNK = 32
CHUNK_W = SSM_CHUNK * SSM_GROUP
CHUNK_PITCH = SSM_CHUNK + 4
GROUPS_PER_TILE = LANES // SSM_GROUP
LANE_TILES = SSM_WIDTH // LANES
IN_COLS_PAD = SSM_WIDTH + Q_LORA + KV_LORA + 2 * LANES
FLASH_UNROLL = 4
FLASH_KEY_RATIO = 4
FLASH_HEADS = 8
VMEM_LIMIT = 56 * 1024 * 1024

F32 = jnp.float32
BF16 = jnp.bfloat16
HIGHEST = lax.Precision.HIGHEST


def _dot(a, b):
    return jnp.dot(a, b, preferred_element_type=F32)


def _dot_exact(a, b):
    return jnp.dot(a, b, precision=HIGHEST, preferred_element_type=F32)


def _select_cols(a, onehot):
    hi = a.astype(BF16)
    lo = (a - hi.astype(F32)).astype(BF16)
    return _dot(hi, onehot) + _dot(lo, onehot)


def _dot_nt(a, b):
    return lax.dot_general(a, b, (((1,), (1,)), ((), ())), preferred_element_type=F32)


def _rms(v, gain=None):
    y = v * lax.rsqrt(jnp.mean(v * v, axis=-1, keepdims=True) + EPS)
    return y if gain is None else y * gain


def _params(*sem):
    return pltpu.CompilerParams(dimension_semantics=sem, vmem_limit_bytes=VMEM_LIMIT)


def _layer_spec(l, *shape):
    return pl.BlockSpec((None,) + shape, lambda *_: (l,) + (0,) * len(shape))


def _pitched(tokens):
    return tokens // SSM_CHUNK * CHUNK_PITCH


def _const_spec(*shape):
    return pl.BlockSpec(shape, lambda *_: (0,) * len(shape))


ROPE_TOKENS_PER_ROW = LANES // QK_ROPE


def _rope_table_kernel(pos_ref, freq_ref, a_ref, b_ref):
    ang = pos_ref[...].astype(F32) * freq_ref[...]
    cos, sin = jnp.cos(ang), jnp.sin(ang)
    rows = ang.shape[0]
    lane = lax.broadcasted_iota(jnp.int32, ang.shape, 1)
    in_rope = (lane >= QK_NOPE) & (lane < QK_DIM)
    for j in range(ROPE_TOKENS_PER_ROW):
        shift = (QK_NOPE - QK_ROPE * j) % LANES
        cj = pltpu.roll(cos, shift, axis=1) if shift else cos
        sj = pltpu.roll(sin, shift, axis=1) if shift else sin
        a_ref[pl.ds(j, rows, stride=ROPE_TOKENS_PER_ROW), :] = (
            jnp.where(lane < QK_NOPE, 1.0, jnp.where(in_rope, cj, 0.0)))
        b_ref[pl.ds(j, rows, stride=ROPE_TOKENS_PER_ROW), :] = jnp.where(in_rope, sj, 0.0)


def _rope_tables(positions):
    bsz, seq = positions.shape
    t = bsz * seq
    tm = min(1024, t)
    per_row = ROPE_TOKENS_PER_ROW
    half = QK_ROPE // 2
    inv_freq = ROPE_THETA ** (-jnp.arange(half, dtype=F32) / half)
    freq = jnp.tile(inv_freq, LANES // half)[None, :]
    pos = jnp.repeat(positions.reshape(t // per_row, per_row), QK_ROPE, axis=1)
    a, b = pl.pallas_call(
        _rope_table_kernel,
        grid=(t // tm,),
        in_specs=[pl.BlockSpec((tm // per_row, LANES), lambda i: (i, 0)), _const_spec(1, LANES)],
        out_specs=[pl.BlockSpec((tm, LANES), lambda i: (i, 0))] * 2,
        out_shape=[jax.ShapeDtypeStruct((t, LANES), F32)] * 2,
        compiler_params=_params("parallel"),
        name="rope_tables",
    )(pos, freq)
    return a.reshape(bsz, seq, LANES), b.reshape(bsz, seq, LANES)


def _inproj_kernel(x_ref, win_ref, wqa_ref, wqb_ref, wk_ref, wv_ref,
                   gains_ref, bd_ref, ra_ref, rb_ref, u_ref, q_ref, k_ref, v_ref):
    h = _rms(x_ref[...]).astype(BF16)
    proj = _dot(h, win_ref[...])
    pad_rows = jnp.zeros((CHUNK_PITCH - SSM_CHUNK, LANES), F32)
    for t in range(LANE_TILES):
        for c in range(proj.shape[0] // SSM_CHUNK):
            u_ref[t, c * CHUNK_PITCH:c * CHUNK_PITCH + SSM_CHUNK, :] = (
                proj[c * SSM_CHUNK:(c + 1) * SSM_CHUNK, t * LANES:(t + 1) * LANES])
            u_ref[t, c * CHUNK_PITCH + SSM_CHUNK:(c + 1) * CHUNK_PITCH, :] = pad_rows

    two = lambda a: jnp.concatenate([a, a], axis=1)
    ra, rb = ra_ref[...], rb_ref[...]
    gains = gains_ref[...]
    bd = bd_ref[...]
    scale = math.log2(math.e) / math.sqrt(QK_DIM)
    inv_d = 1.0 / QK_DIM

    o1 = SSM_WIDTH
    o2 = o1 + Q_LORA
    o3 = o2 + KV_LORA
    hq = _rms(proj[:, o1:o2]).astype(BF16)
    qa = _dot(hq, wqa_ref[...])
    qb = _dot(hq, wqb_ref[...])
    hkv = _rms(proj[:, o2:o3]).astype(BF16)
    kn = _dot(hkv, wk_ref[...])
    vm = _dot(hkv, wv_ref[...])

    kr = proj[:, o3:o3 + LANES]
    kr_sw = proj[:, o3 + LANES:o3 + 2 * LANES]
    kr_ss = two(_dot((kr * kr).astype(BF16), bd[:LANES, :LANES]))
    krot = two(kr * (gains[3:4] * ra) + kr_sw * (gains[4:5] * rb))
    ga, gb, gk = two(ra * gains[0:1]), two(rb * gains[1:2]), two(gains[2:3])
    lane = lax.broadcasted_iota(jnp.int32, ga.shape, 1)
    ones_col = jnp.where((lane >= V_DIM) & (lane < HEAD_PAD + V_DIM), 1.0, 0.0)

    for hp in range(MLA_HEADS // 2):
        sl = slice(2 * hp * HEAD_PAD, (2 * hp + 2) * HEAD_PAD)
        q2 = qa[:, sl]
        rq = lax.rsqrt(_dot((q2 * q2).astype(BF16), bd) * inv_d + EPS) * scale
        qo = ((q2 * ga + qb[:, sl] * gb) * rq).astype(BF16)
        k2 = kn[:, sl]
        rk = lax.rsqrt((_dot((k2 * k2).astype(BF16), bd) + kr_ss) * inv_d + EPS)
        ko = ((k2 * gk + krot) * rk).astype(BF16)
        vo = (vm[:, sl] + ones_col).astype(BF16)
        for j in range(2):
            q_ref[2 * hp + j] = qo[:, j * HEAD_PAD:(j + 1) * HEAD_PAD]
            k_ref[2 * hp + j] = ko[:, j * HEAD_PAD:(j + 1) * HEAD_PAD]
            v_ref[2 * hp + j] = vo[:, j * HEAD_PAD:(j + 1) * HEAD_PAD]


def _inproj(l, x, w, ra, rb, tm):
    bsz, seq, _ = x.shape
    tok = lambda width: pl.BlockSpec((None, tm, width), lambda b, i: (b, i, 0))
    head = pl.BlockSpec((None, MLA_HEADS, tm, HEAD_PAD), lambda b, i: (b, 0, i, 0))
    hw = MLA_HEADS * HEAD_PAD
    head_shape = jax.ShapeDtypeStruct((bsz, MLA_HEADS, seq, HEAD_PAD), BF16)
    return pl.pallas_call(
        _inproj_kernel,
        grid=(bsz, seq // tm),
        in_specs=[tok(D_MODEL), _layer_spec(l, D_MODEL, IN_COLS_PAD),
                  _layer_spec(l, Q_LORA, hw), _layer_spec(l, Q_LORA, hw),
                  _layer_spec(l, KV_LORA, hw), _layer_spec(l, KV_LORA, hw),
                  _layer_spec(l, 8, HEAD_PAD), _const_spec(2 * HEAD_PAD, 2 * HEAD_PAD),
                  tok(LANES), tok(LANES)],
        out_specs=[pl.BlockSpec((LANE_TILES, None, _pitched(tm), LANES), lambda b, i: (0, b, i, 0)),
                   head, head, head],
        out_shape=[jax.ShapeDtypeStruct((LANE_TILES, bsz, _pitched(seq), LANES), F32),
                   head_shape, head_shape, head_shape],
        compiler_params=_params("parallel", "parallel"),
        name="inproj",
    )(x, w["w_in"], w["w_qa"], w["w_qb"], w["w_k"], w["w_v"], w["mla_gains"], w["bd"], ra, rb)


S5_PREP_GROUPS = 16


def _s5_prep_kernel(*refs):
    per_group_in, shared, outs = refs[:8], refs[8:10], refs[10:]
    for g in range(S5_PREP_GROUPS):
        _s5_prep_group(*[r.at[g] for r in per_group_in], *shared, *[r.at[g] for r in outs])


def _s5_prep_group(pcol_ref, prow_ref, b_re_ref, b_im_ref, bt_re_ref, bt_im_ref,
                   ct_re_ref, ct_im_ref, sel_ref, reps_ref,
                   m_ref, wst_ref, wo_ref, apa_ref, apb_ref):
    def zoh_coeff(a_re, a_im, lr, li):
        xr = a_re - 1.0
        den = lr * lr + li * li
        return (xr * lr + a_im * li) / den, (a_im * lr - xr * li) / den

    def zoh(lr, li, ls):
        step = jnp.exp(ls)
        mag = jnp.exp(lr * step)
        ang = li * step
        return zoh_coeff(mag * jnp.cos(ang), mag * jnp.sin(ang), lr, li) + (step,)

    pc = pcol_ref[...]
    lr, li = pc[:, 0:1], pc[:, 1:2]
    step = jnp.exp(pc[:, 2:3])
    n = lax.broadcasted_iota(jnp.int32, (SSM_STATE, LANES), 1).astype(F32)
    mag = jnp.exp(lr * step * n)
    th = li * step * n
    pw_re, pw_im = mag * jnp.cos(th), mag * jnp.sin(th)
    cfr, cfi = zoh_coeff(pw_re[:, 1:2], pw_im[:, 1:2], lr, li)
    b_re, b_im = b_re_ref[...], b_im_ref[...]
    bb_re = cfr * b_re - cfi * b_im
    bb_im = cfr * b_im + cfi * b_re

    pr = prow_ref[...]
    lr_row, li_row, ls_row = pr[0:1], pr[1:2], pr[2:3]
    cfr_row, cfi_row, step_row = zoh(lr_row, li_row, ls_row)
    bt_re, bt_im = bt_re_ref[...], bt_im_ref[...]
    bbt_re = cfr_row[:, :SSM_STATE] * bt_re - cfi_row[:, :SSM_STATE] * bt_im
    bbt_im = cfr_row[:, :SSM_STATE] * bt_im + cfi_row[:, :SSM_STATE] * bt_re

    sel = sel_ref[...]
    brep_re, brep_im = _select_cols(bb_re, sel), _select_cols(bb_im, sel)
    crep_re, crep_im = _select_cols(ct_re_ref[...], sel), _select_cols(ct_im_ref[...], sel)

    apow = lambda i: (_select_cols(pw_re, reps_ref[i]), _select_cols(pw_im, reps_ref[i]))

    e_re, e_im = apow(0)
    g_re = e_re * crep_re - e_im * crep_im
    g_im = e_re * crep_im + e_im * crep_re
    kt = _dot_exact(bbt_re, g_re) - _dot_exact(bbt_im, g_im)
    klane = lax.broadcasted_iota(jnp.int32, kt.shape, 1)
    m_ref[0:SSM_GROUP, :] = kt.astype(BF16)
    for s in range(1, SSM_CHUNK):
        shifted = jnp.where(klane >= s * SSM_GROUP, pltpu.roll(kt, s * SSM_GROUP, axis=1), 0.0)
        m_ref[s * SSM_GROUP:(s + 1) * SSM_GROUP, :] = shifted.astype(BF16)

    e_re, e_im = apow(1)
    wst_ref[0:SSM_STATE, :] = (e_re * brep_re - e_im * brep_im).astype(BF16)
    wst_ref[SSM_STATE:, :] = (e_re * brep_im + e_im * brep_re).astype(BF16)

    e_re, e_im = apow(2)
    wo_ref[0:SSM_STATE, :] = (e_re * crep_re - e_im * crep_im).astype(BF16)
    wo_ref[SSM_STATE:, :] = (-(e_re * crep_im + e_im * crep_re)).astype(BF16)

    krow = lax.broadcasted_iota(jnp.int32, (8, LANES), 0)
    klan = lax.broadcasted_iota(jnp.int32, (8, LANES), 1)
    nn = (SSM_CHUNK * jnp.left_shift(1, krow)).astype(F32)
    mag = jnp.exp(lr_row * step_row * nn)
    th = li_row * step_row * nn
    p_re, p_im = mag * jnp.cos(th), mag * jnp.sin(th)
    apa_ref[...] = p_re
    apb_ref[...] = jnp.where(klan < SSM_STATE, -p_im, p_im)


def _s5_prep(lam_re, lam_im, log_step, b_re, b_im, c_re, c_im):
    nl, g, p = lam_re.shape
    hh = SSM_GROUP
    ls = jnp.broadcast_to(log_step[..., None], (nl, g, p))
    params = jnp.stack([lam_re, lam_im, ls], axis=-1)
    pcol = jnp.pad(params, ((0, 0), (0, 0), (0, 0), (0, 5)))
    dup = lambda v: jnp.concatenate([v, v], axis=-1)
    prow = jnp.pad(jnp.stack([dup(lam_re), dup(lam_im), dup(ls)], axis=2),
                   ((0, 0), (0, 0), (0, 5), (0, 0)))
    sel = jnp.tile(jnp.eye(hh, dtype=BF16), (1, SSM_CHUNK))
    tau = jnp.arange(CHUNK_W) // hh
    nrow = jnp.arange(LANES)[:, None]
    reps = jnp.stack([nrow == tau, nrow == SSM_CHUNK - 1 - tau, nrow == tau + 1]).astype(BF16)
    blk = lambda *s: pl.BlockSpec((None, S5_PREP_GROUPS) + s, lambda l, i: (l, i) + (0,) * len(s))
    return pl.pallas_call(
        _s5_prep_kernel,
        grid=(nl, g // S5_PREP_GROUPS),
        in_specs=[blk(p, 8), blk(8, LANES), blk(p, hh), blk(p, hh), blk(hh, p), blk(hh, p),
                  blk(p, hh), blk(p, hh), _const_spec(hh, CHUNK_W), _const_spec(3, LANES, CHUNK_W)],
        out_specs=[blk(CHUNK_W, CHUNK_W), blk(2 * p, CHUNK_W), blk(2 * p, CHUNK_W),
                   blk(8, LANES), blk(8, LANES)],
        out_shape=[jax.ShapeDtypeStruct((nl, g, CHUNK_W, CHUNK_W), BF16),
                   jax.ShapeDtypeStruct((nl, g, 2 * p, CHUNK_W), BF16),
                   jax.ShapeDtypeStruct((nl, g, 2 * p, CHUNK_W), BF16),
                   jax.ShapeDtypeStruct((nl, g, 8, LANES), F32),
                   jax.ShapeDtypeStruct((nl, g, 8, LANES), F32)],
        compiler_params=_params("parallel", "parallel"),
        name="s5_prep",
    )(pcol, prow, b_re, b_im, jnp.swapaxes(b_re, -1, -2), jnp.swapaxes(b_im, -1, -2),
      jnp.swapaxes(c_re, -1, -2), jnp.swapaxes(c_im, -1, -2), sel, reps)


def _s5_mix_kernel(u_ref, perm_ref, m_ref, wst_ref, wo_ref, apa_ref, apb_ref, dx_ref, y_ref,
                   *, nsteps):
    nb, rows, _ = u_ref.shape
    cps = rows // CHUNK_PITCH
    nc = nb * cps
    gpt = GROUPS_PER_TILE
    nslab = SSM_CHUNK // gpt
    perm = perm_ref[...]

    def slab(j):
        return jnp.concatenate([
            jnp.concatenate([u_ref[b, pl.ds(gpt * j + s, cps, stride=CHUNK_PITCH), :] for s in range(gpt)], axis=1)
            for b in range(nb)], axis=0)

    z = _dot(jnp.concatenate([slab(j) for j in range(nslab)], axis=0).astype(BF16), perm).astype(BF16)

    c = lax.broadcasted_iota(jnp.int32, (nc, 2 * SSM_STATE), 0) & (cps - 1)
    ys = []
    for g in range(gpt):
        x = jnp.concatenate([z[j * nc:(j + 1) * nc, g * LANES:(g + 1) * LANES] for j in range(nslab)],
                            axis=1)
        y = _dot(x, m_ref[g])
        st = _dot_nt(x, wst_ref[g])
        apa, apb = apa_ref[g], apb_ref[g]
        for k in range(nsteps):
            d = 1 << k
            sh = jnp.where(c >= d, pltpu.roll(st, d, axis=0), 0.0)
            st = st + sh * apa[k:k + 1] + pltpu.roll(sh, SSM_STATE, axis=1) * apb[k:k + 1]
        carried = jnp.where(c >= 1, pltpu.roll(st, 1, axis=0), 0.0)
        y = y + _dot(carried.astype(BF16), wo_ref[g]) + dx_ref[g] * x.astype(F32)
        ys.append(jax.nn.gelu(y).astype(BF16))

    w = jnp.concatenate([
        jnp.concatenate([ys[g][:, j * LANES:(j + 1) * LANES] for g in range(gpt)], axis=1)
        for j in range(nslab)], axis=0)
    o = _dot(w, perm)
    for b in range(nb):
        for j in range(nslab):
            for t in range(gpt):
                y_ref[b, pl.ds(gpt * j + t, cps, stride=CHUNK_PITCH), :] = (
                    o[j * nc + b * cps:j * nc + (b + 1) * cps, t * LANES:(t + 1) * LANES])
        for r in range(SSM_CHUNK, CHUNK_PITCH):
            y_ref[b, pl.ds(r, cps, stride=CHUNK_PITCH), :] = jnp.zeros((cps, LANES), F32)


def _s5_mix(l, u, prep, dx, perm):
    tiles, bsz, rows, _ = u.shape
    cps = rows // CHUNK_PITCH
    assert cps & (cps - 1) == 0 and cps <= 256
    gpt = GROUPS_PER_TILE
    nb = 2 if bsz % 2 == 0 else 1
    m, wst, wo, apa, apb = prep
    tokens = pl.BlockSpec((None, nb, rows, LANES), lambda t, b: (t, b, 0, 0))
    grp = lambda *s: pl.BlockSpec((None, gpt) + s, lambda t, b: (l, t) + (0,) * len(s))
    return pl.pallas_call(
        functools.partial(_s5_mix_kernel, nsteps=cps.bit_length() - 1),
        grid=(tiles, bsz // nb),
        in_specs=[tokens, _const_spec(gpt * LANES, gpt * LANES), grp(CHUNK_W, CHUNK_W),
                  grp(2 * SSM_STATE, CHUNK_W), grp(2 * SSM_STATE, CHUNK_W), grp(8, LANES), grp(8, LANES),
                  grp(1, CHUNK_W)],
        out_specs=tokens,
        out_shape=jax.ShapeDtypeStruct(u.shape, F32),
        compiler_params=_params("parallel", "parallel"),
        name="s5_mix",
    )(u, perm, m, wst, wo, apa, apb, dx)


FULL = 0


def _flash_kernel(qt_ref, kt_ref, kind_ref, q_ref, k_ref, v_ref, o_ref, m_scr, acc_scr, *, heads):
    p_idx = pl.program_id(2)
    ki = kt_ref[p_idx]
    kind = kind_ref[p_idx]
    tq, tk = q_ref.shape[2], k_ref.shape[2]

    @pl.when(ki == 0)
    def _():
        m_scr[...] = jnp.full(m_scr.shape, -jnp.inf, F32)
        acc_scr[...] = jnp.zeros(acc_scr.shape, F32)

    def attend(j, nk, diag_offset):
        s = _dot_nt(q_ref[0, j], k_ref[0, j, 0:nk, :])
        if diag_offset is not None:
            row = lax.broadcasted_iota(jnp.int32, s.shape, 0)
            col = lax.broadcasted_iota(jnp.int32, s.shape, 1)
            s = jnp.where(col <= row + diag_offset, s, -jnp.inf)
        m_prev = m_scr[j]
        m_new = jnp.maximum(m_prev, jnp.max(s, axis=-1, keepdims=True))
        alpha = jnp.exp2(m_prev - m_new)
        p = jnp.exp2(s - jnp.tile(m_new, (1, nk // LANES)))
        acc_scr[j] = alpha * acc_scr[j] + _dot(p.astype(BF16), v_ref[0, j, 0:nk, :])
        m_scr[j] = m_new

    def all_heads(nk, diag_offset):
        def group(g, carry):
            for j in range(FLASH_UNROLL):
                attend(g * FLASH_UNROLL + j, nk, diag_offset)
            return carry
        lax.fori_loop(0, heads // FLASH_UNROLL, group, 0)

    @pl.when(kind == FULL)
    def _():
        all_heads(tk, None)

    for r in range(1, tk // tq + 1):
        @pl.when(kind == r)
        def _(r=r):
            all_heads(r * tq, (r - 1) * tq)

    @pl.when(kind != FULL)
    def _():
        lane = lax.broadcasted_iota(jnp.int32, (tq, HEAD_PAD), 1)
        outs = []
        for j in range(0, heads, 2):
            even, odd = acc_scr[j], acc_scr[j + 1]
            pv = jnp.where(lane < V_DIM, even, odd)
            sums = pltpu.roll(jnp.where(lane < V_DIM, odd, even), V_DIM, axis=1)
            outs.append(pv / sums)
        o_ref[0] = jnp.concatenate(outs, axis=-1).astype(BF16)


def _flash(q, k, v, tq, heads_per_step):
    bsz, heads, seq, _ = q.shape
    hp = heads_per_step
    ratio = FLASH_KEY_RATIO
    tk = ratio * tq
    assert seq % tk == 0
    steps = []
    for i in range(seq // tq):
        steps += [(i, j, FULL) for j in range(i // ratio)] + [(i, i // ratio, i % ratio + 1)]
    qt, kt, kind = (jnp.asarray(col, jnp.int32) for col in zip(*steps))
    qspec = pl.BlockSpec((1, hp, tq, HEAD_PAD), lambda b, h, p, qt, kt, kind: (b, h, qt[p], 0))
    kspec = pl.BlockSpec((1, hp, tk, HEAD_PAD), lambda b, h, p, qt, kt, kind: (b, h, kt[p], 0))
    grid_spec = pltpu.PrefetchScalarGridSpec(
        num_scalar_prefetch=3,
        grid=(bsz, heads // hp, len(steps)),
        in_specs=[qspec, kspec, kspec],
        out_specs=pl.BlockSpec((1, tq, hp * V_DIM), lambda b, h, p, qt, kt, kind: (b, qt[p], h)),
        scratch_shapes=[pltpu.VMEM((hp, tq, LANES), F32), pltpu.VMEM((hp, tq, HEAD_PAD), F32)])
    return pl.pallas_call(
        functools.partial(_flash_kernel, heads=hp),
        grid_spec=grid_spec,
        out_shape=jax.ShapeDtypeStruct((bsz, seq, heads * V_DIM), BF16),
        compiler_params=_params("parallel", "parallel", "arbitrary"),
        name="flash",
    )(qt, kt, kind, q, k, v)


def _head_sums(v2, lane, width):
    out = jnp.zeros_like(v2)
    for hd in range(v2.shape[-1] // width):
        msk = (lane >= hd * width) & (lane < (hd + 1) * width)
        out = out + jnp.where(msk, jnp.sum(jnp.where(msk, v2, 0.0), axis=-1, keepdims=True), 0.0)
    return out


def _mem_prep_kernel(mem_ref, g_ref, w_ref, kg_ref, kt_ref, vm_ref):
    hm = _rms(mem_ref[0], g_ref[0]).astype(BF16)
    kv = _dot(hm, w_ref[0])
    k, v = kv[:, :MEM_WIDTH], kv[:, MEM_WIDTH:]
    lane = lax.broadcasted_iota(jnp.int32, k.shape, 1)
    ss = _head_sums(k * k, lane, MEM_HEAD_DIM)
    kn = k * lax.rsqrt(ss * (1.0 / MEM_HEAD_DIM) + EPS) * kg_ref[0] * (math.log2(math.e) / math.sqrt(MEM_HEAD_DIM))
    knt = kn.T
    row = lax.broadcasted_iota(jnp.int32, knt.shape, 0)
    for hd in range(MEM_HEADS):
        lo, hi = hd * MEM_HEAD_DIM, (hd + 1) * MEM_HEAD_DIM
        kt_ref[0, 0, hd] = jnp.where((row >= lo) & (row < hi), knt, 0.0).astype(BF16)
        vm_ref[0, 0, hd] = jnp.where((lane >= lo) & (lane < hi), v, 0.0).astype(BF16)


def _mem_prep(mem, g, w_kv, k_gain):
    nl = g.shape[0]
    bsz = mem.shape[0]
    out = jax.ShapeDtypeStruct((nl, bsz, MEM_HEADS, N_MEM, MEM_WIDTH), BF16)
    return pl.pallas_call(
        _mem_prep_kernel,
        grid=(nl, bsz),
        in_specs=[pl.BlockSpec((1, N_MEM, D_MODEL), lambda l, b: (b, 0, 0)),
                  pl.BlockSpec((1, 1, D_MODEL), lambda l, b: (l, 0, 0)),
                  pl.BlockSpec((1, D_MODEL, 2 * MEM_WIDTH), lambda l, b: (l, 0, 0)),
                  pl.BlockSpec((1, 1, MEM_WIDTH), lambda l, b: (l, 0, 0))],
        out_specs=[pl.BlockSpec((1, 1, MEM_HEADS, MEM_WIDTH, N_MEM), lambda l, b: (l, b, 0, 0, 0)),
                   pl.BlockSpec((1, 1, MEM_HEADS, N_MEM, MEM_WIDTH), lambda l, b: (l, b, 0, 0, 0))],
        out_shape=[out, out],
        compiler_params=_params("parallel", "parallel"),
        name="mem_prep",
    )(mem, g, w_kv, k_gain)


def _outproj_kernel(x_ref, ys_ref, om_ref, wglu_ref, bglu_ref, wout_ref,
                    wmq_ref, qg_ref, bdm_ref, kt_ref, vm_ref, wmo_ref,
                    x2_ref):
    nchunk = x_ref.shape[0] // SSM_CHUNK
    ys = jnp.concatenate([
        jnp.concatenate([ys_ref[t, c * CHUNK_PITCH:c * CHUNK_PITCH + SSM_CHUNK, :] for c in range(nchunk)], axis=0)
        for t in range(LANE_TILES)], axis=1)
    yg = ys * jax.nn.sigmoid(_dot(ys.astype(BF16), wglu_ref[...]) + bglu_ref[...])
    n1 = _rms(yg).astype(BF16)
    n2 = _rms(om_ref[...].astype(F32)).astype(BF16)
    x1 = x_ref[...] + _dot(n1, wout_ref[:SSM_WIDTH, :]) + _dot(n2, wout_ref[SSM_WIDTH:, :])

    r1 = lax.rsqrt(jnp.mean(x1 * x1, axis=-1, keepdims=True) + EPS)
    q = _dot(x1.astype(BF16), wmq_ref[...]) * r1
    ss = _dot((q * q).astype(BF16), bdm_ref[...])
    qn = (q * lax.rsqrt(ss * (1.0 / MEM_HEAD_DIM) + EPS) * qg_ref[...]).astype(BF16)
    o = jnp.zeros(q.shape, F32)
    for hd in range(MEM_HEADS):
        s = _dot(qn, kt_ref[hd])
        p = jnp.exp2(s - jnp.max(s, axis=-1, keepdims=True))
        inv = 1.0 / jnp.sum(p, axis=-1, keepdims=True)
        o = o + _dot(p.astype(BF16), vm_ref[hd]) * inv
    x2 = x1 + _dot(o.astype(BF16), wmo_ref[...])
    x2_ref[...] = x2


def _outproj(l, x, ys, om, w, kt, vm, tm):
    bsz, seq, _ = x.shape
    tok = lambda width: pl.BlockSpec((None, tm, width), lambda b, i: (b, i, 0))
    memb = pl.BlockSpec((None, None, MEM_HEADS, N_MEM, MEM_WIDTH), lambda b, i: (l, b, 0, 0, 0))
    return pl.pallas_call(
        _outproj_kernel,
        grid=(bsz, seq // tm),
        in_specs=[tok(D_MODEL), pl.BlockSpec((LANE_TILES, None, _pitched(tm), LANES), lambda b, i: (0, b, i, 0)),
                  tok(MLA_WIDTH),
                  _layer_spec(l, SSM_WIDTH, SSM_WIDTH), _layer_spec(l, 1, SSM_WIDTH),
                  _layer_spec(l, D_MODEL, D_MODEL),
                  _layer_spec(l, D_MODEL, MEM_WIDTH), _layer_spec(l, 1, MEM_WIDTH),
                  _const_spec(MEM_WIDTH, MEM_WIDTH),
                  memb, memb, _layer_spec(l, MEM_WIDTH, D_MODEL)],
        out_specs=tok(D_MODEL),
        out_shape=jax.ShapeDtypeStruct((bsz, seq, D_MODEL), F32),
        compiler_params=_params("parallel", "parallel"),
        name="outproj",
    )(x, ys, om, w["w_glu"], w["b_glu"], w["w_out"],
      w["w_mq"], w["mem_q_gain"], w["bd_mem"], kt, vm, w["w_mo"])


def _mlp_kernel(x_ref, w1_ref, w2_ref, o_ref, *, ff_tile):
    acc = x_ref[...]
    h = _rms(acc).astype(BF16)
    for c in range(D_FF // ff_tile):
        a = jnp.maximum(_dot(h, w1_ref[:, c * ff_tile:(c + 1) * ff_tile]), 0.0)
        acc = acc + _dot((a * a).astype(BF16), w2_ref[c * ff_tile:(c + 1) * ff_tile, :])
    o_ref[...] = acc


def _mlp(l, x, w1, w2, tm):
    t = x.shape[0]
    tok = pl.BlockSpec((tm, D_MODEL), lambda i: (i, 0))
    return pl.pallas_call(
        functools.partial(_mlp_kernel, ff_tile=1024),
        grid=(t // tm,),
        in_specs=[tok,
                  pl.BlockSpec((None, D_MODEL, D_FF), lambda i: (l, 0, 0), pipeline_mode=pl.Buffered(1)),
                  pl.BlockSpec((None, D_FF, D_MODEL), lambda i: (l, 0, 0), pipeline_mode=pl.Buffered(1))],
        out_specs=tok,
        out_shape=jax.ShapeDtypeStruct((t, D_MODEL), F32),
        compiler_params=_params("parallel"),
        name="mlp",
    )(x, w1, w2)


def _half_swap(w):
    h = w.shape[-1] // 2
    return jnp.concatenate([-w[..., h:], w[..., :h]], axis=-1)


def _half_swap_unsigned(w):
    h = w.shape[-1] // 2
    return jnp.concatenate([w[..., h:], w[..., :h]], axis=-1)


def _pad_last(w, before, after):
    pads = [(0, 0)] * (w.ndim - 1) + [(before, after)]
    return jnp.pad(w, pads)


def _trunk(tm, tq, x, mem, positions, norm_mix, w_in, ssm_lambda_re, ssm_lambda_im, ssm_log_step, ssm_b_re, ssm_b_im, ssm_c_re, ssm_c_im, ssm_d, ssm_w_glu, ssm_b_glu, mla_q_norm, mla_w_uq, mla_kv_norm, mla_w_ukv, mla_q_gain, mla_k_gain, out_norm_ssm, out_norm_mla, w_out, norm_mem_q, norm_mem_kv, mem_w_q, mem_w_kv, mem_q_gain, mem_k_gain, mem_w_o, norm_mlp, mlp_w1, mlp_w2):
    bsz, seq, _ = x.shape
    depth = norm_mix.shape[0]
    row = lambda a: a[:, None, :]
    tail = HEAD_PAD - QK_DIM

    s3 = SSM_WIDTH + Q_LORA + KV_LORA
    rows = lambda g: g[:, :, None]
    w_in_g = (rows(norm_mix) * w_in).astype(BF16)
    k_rope_w = w_in_g[..., s3:]
    w_in_x = jnp.concatenate([w_in_g[..., :s3], _pad_last(k_rope_w, QK_NOPE, tail),
                              _pad_last(_half_swap(k_rope_w), QK_NOPE, tail)], axis=-1)
    wq = mla_w_uq.reshape(depth, Q_LORA, MLA_HEADS, QK_DIM)
    hw = MLA_HEADS * HEAD_PAD
    gq = rows(mla_q_norm)
    w_qa = (gq * _pad_last(wq, 0, tail).reshape(depth, Q_LORA, hw)).astype(BF16)
    w_qb = (gq * _pad_last(_half_swap(wq[..., QK_NOPE:]), QK_NOPE, tail).reshape(depth, Q_LORA, hw)).astype(BF16)
    wkv = mla_w_ukv.reshape(depth, KV_LORA, MLA_HEADS, QK_NOPE + V_DIM)
    gkv = rows(mla_kv_norm)
    w_k = (gkv * _pad_last(wkv[..., :QK_NOPE], 0, HEAD_PAD - QK_NOPE).reshape(depth, KV_LORA, hw)).astype(BF16)
    wv = wkv[..., QK_NOPE:]
    wv = jnp.stack([_pad_last(wv[:, :, 0::2], 0, HEAD_PAD - V_DIM), _pad_last(wv[:, :, 1::2], HEAD_PAD - V_DIM, 0)],
                   axis=3)
    w_v = (gkv * wv.reshape(depth, KV_LORA, hw)).astype(BF16)
    q_rope_g, k_rope_g = mla_q_gain[:, QK_NOPE:], mla_k_gain[:, QK_NOPE:]
    mla_gains = jnp.stack([
        _pad_last(mla_q_gain, 0, tail),
        _pad_last(_half_swap_unsigned(q_rope_g), QK_NOPE, tail),
        _pad_last(mla_k_gain[:, :QK_NOPE], 0, HEAD_PAD - QK_NOPE),
        _pad_last(k_rope_g, QK_NOPE, tail),
        _pad_last(_half_swap_unsigned(k_rope_g), QK_NOPE, tail)], axis=1)
    mla_gains = jnp.pad(mla_gains, ((0, 0), (0, 3), (0, 0)))
    blk = jnp.arange(2 * HEAD_PAD) // HEAD_PAD
    bd = (blk[:, None] == blk[None, :]).astype(BF16)
    mblk = jnp.arange(MEM_WIDTH) // MEM_HEAD_DIM
    bd_mem = (mblk[:, None] == mblk[None, :]).astype(BF16)
    wmkv = mem_w_kv.reshape(depth, D_MODEL, MEM_HEADS, 2, MEM_HEAD_DIM)
    w_mkv = wmkv.transpose(0, 1, 3, 2, 4).reshape(depth, D_MODEL, 2 * MEM_WIDTH).astype(BF16)
    idx = jnp.arange(GROUPS_PER_TILE * LANES)
    dest = ((idx // SSM_GROUP) % GROUPS_PER_TILE) * LANES + (idx // LANES) * SSM_GROUP + idx % SSM_GROUP
    perm = (dest[:, None] == idx[None, :]).astype(BF16)

    g_out = jnp.concatenate([out_norm_ssm, out_norm_mla], axis=-1)
    w = dict(w_in=w_in_x, w_qa=w_qa, w_qb=w_qb, w_k=w_k, w_v=w_v, mla_gains=mla_gains, bd=bd, bd_mem=bd_mem,
             w_glu=ssm_w_glu.astype(BF16), b_glu=row(ssm_b_glu),
             w_out=(rows(g_out) * w_out).astype(BF16),
             w_mq=(rows(norm_mem_q) * mem_w_q).astype(BF16),
             mem_q_gain=row(jnp.tile(mem_q_gain, (1, MEM_HEADS))),
             w_mo=mem_w_o.astype(BF16))
    w1_b = (rows(norm_mlp) * mlp_w1).astype(BF16)
    w2_b = mlp_w2.astype(BF16)

    ra, rb = _rope_tables(positions)
    prep = _s5_prep(ssm_lambda_re, ssm_lambda_im, ssm_log_step, ssm_b_re, ssm_b_im, ssm_c_re, ssm_c_im)
    kt_all, vm_all = _mem_prep(mem, row(norm_mem_kv), w_mkv, row(jnp.tile(mem_k_gain, (1, MEM_HEADS))))
    dx = jnp.tile(ssm_d.reshape(depth, SSM_GROUPS, 1, SSM_GROUP), (1, 1, 1, SSM_CHUNK))

    for l in range(depth):
        u, q, k, v = _inproj(l, x, w, ra, rb, min(2 * tm, seq))
        ys = _s5_mix(l, u, prep, dx, perm)
        om = _flash(q, k, v, tq, FLASH_HEADS)
        x2 = _outproj(l, x, ys, om, w, kt_all, vm_all, min(2 * tm, seq))
        x = _mlp(l, x2.reshape(bsz * seq, D_MODEL), w1_b, w2_b, min(2 * tm, seq)).reshape(bsz, seq, D_MODEL)
    return x


def kernel(x, mem, positions, norm_mix, w_in, ssm_lambda_re, ssm_lambda_im, ssm_log_step, ssm_b_re, ssm_b_im, ssm_c_re, ssm_c_im, ssm_d, ssm_w_glu, ssm_b_glu, mla_q_norm, mla_w_uq, mla_kv_norm, mla_w_ukv, mla_q_gain, mla_k_gain, out_norm_ssm, out_norm_mla, w_out, norm_mem_q, norm_mem_kv, mem_w_q, mem_w_kv, mem_q_gain, mem_k_gain, mem_w_o, norm_mlp, mlp_w1, mlp_w2):
    seq = x.shape[1]
    return _trunk(min(512, seq), min(512, seq // 2), x, mem, positions, norm_mix, w_in, ssm_lambda_re, ssm_lambda_im, ssm_log_step, ssm_b_re, ssm_b_im, ssm_c_re, ssm_c_im, ssm_d, ssm_w_glu, ssm_b_glu, mla_q_norm, mla_w_uq, mla_kv_norm, mla_w_ukv, mla_q_gain, mla_k_gain, out_norm_ssm, out_norm_mla, w_out, norm_mem_q, norm_mem_kv, mem_w_q, mem_w_kv, mem_q_gain, mem_k_gain, mem_w_o, norm_mlp, mlp_w1, mlp_w2)
```
